```python
import math
import jax
import jax.numpy as jnp
from jax import lax
import numpy as np

D_MODEL = 1024
BATCH = 8
SEQ = 2048
DEPTH = 4
DEC_BATCH = 32
DEC_SEQ = 8
PAST_LEN = 8192
PAGE_SIZE = 128

N_EVEN = (DEPTH + 1) // 2
N_ODD = DEPTH // 2
HEAD_DIM = 64
H_FOX = 8
H_RET = 8
FOX_W = H_FOX * HEAD_DIM
RET_W = H_RET * HEAD_DIM
MIX_W = FOX_W + RET_W
EVEN_IN = 3 * FOX_W + H_FOX + 3 * RET_W + MIX_W
EVEN_SPLITS = [FOX_W, 2 * FOX_W, 3 * FOX_W, 3 * FOX_W + H_FOX, 3 * FOX_W + H_FOX + RET_W, 3 * FOX_W + H_FOX + 2 * RET_W, 3 * FOX_W + H_FOX + 3 * RET_W]
Q_BLOCK = 128
RET_CHUNK = 128
ROPE_BASE = 10000.0
LRU_W = D_MODEL
LRU_BLOCKS = 8
LRU_BW = LRU_W // LRU_BLOCKS
CONV_W = 4
LRU_C = 8.0
PLE_DIM = 256
NORM_EPS = 1e-6
NEG_INF = -1e30

kernel_name = 'fox_retention_rglru_hybrid_step'

F32 = jnp.float32


def rms_norm(x, w):
    x32 = x.astype(F32)
    y = x32 * lax.rsqrt(jnp.mean(x32 * x32, axis=-1, keepdims=True) + NORM_EPS)
    return (y * w.astype(F32)).astype(x.dtype)


def rotary(x, pos):
    half = HEAD_DIM // 2
    inv = ROPE_BASE ** (-jnp.arange(half, dtype=F32) / half)
    ang = pos[:, None] * inv[None, :]
    cos = jnp.cos(ang)[None, :, None, :]
    sin = jnp.sin(ang)[None, :, None, :]
    x32 = x.astype(F32)
    x1, x2 = x32[..., :half], x32[..., half:]
    return jnp.concatenate([x1 * cos - x2 * sin, x1 * sin + x2 * cos], axis=-1).astype(x.dtype)


def fox_prompt(q, k, v, logf):
    B, L, H, D = q.shape
    nb = L // Q_BLOCK
    scale = D ** -0.5
    cum = jnp.cumsum(logf, axis=1)
    cum_t = cum.transpose(0, 2, 1)
    kpos = jnp.arange(L)
    qb = q.reshape(B, nb, Q_BLOCK, H, D).swapaxes(0, 1)
    cb = cum.reshape(B, nb, Q_BLOCK, H).swapaxes(0, 1)

    def block(args):
        qi, ci, bi = args
        s = jnp.einsum('bqhd,bkhd->bhqk', qi, k).astype(F32) * scale
        s = s + ci.transpose(0, 2, 1)[..., None] - cum_t[:, :, None, :]
        qpos = bi * Q_BLOCK + jnp.arange(Q_BLOCK)
        s = jnp.where((qpos[:, None] >= kpos[None, :])[None, None], s, NEG_INF)
        p = jax.nn.softmax(s, axis=-1)
        return jnp.einsum('bhqk,bkhd->bqhd', p.astype(v.dtype), v)

    o = lax.map(block, (qb, cb, jnp.arange(nb)))
    return o.swapaxes(0, 1).reshape(B, L, H, D)


def fox_sample(q, k, v, logf, k_past, v_past, logf_past):
    D = q.shape[-1]
    T = q.shape[1]
    P = k_past.shape[1]
    scale = D ** -0.5
    lp = logf_past.astype(F32)
    rel = lax.cumsum(lp, axis=1, reverse=True) - lp
    cn = jnp.cumsum(logf, axis=1).transpose(0, 2, 1)
    s_past = jnp.einsum('bqhd,bkhd->bhqk', q, k_past.astype(q.dtype)).astype(F32) * scale
    s_past = s_past + cn[..., None] + rel.transpose(0, 2, 1)[:, :, None, :]
    s_new = jnp.einsum('bqhd,bkhd->bhqk', q, k).astype(F32) * scale
    s_new = s_new + cn[..., None] - cn[:, :, None, :]
    causal = jnp.arange(T)[:, None] >= jnp.arange(T)[None, :]
    s_new = jnp.where(causal[None, None], s_new, NEG_INF)
    p = jax.nn.softmax(jnp.concatenate([s_past, s_new], axis=-1), axis=-1)
    o = jnp.einsum('bhqk,bkhd->bqhd', p[..., :P].astype(v.dtype), v_past.astype(v.dtype))
    return o + jnp.einsum('bhqk,bkhd->bqhd', p[..., P:].astype(v.dtype), v)


def retention_chunk(S0, q, k, v, log_gamma):
    L = q.shape[1]
    idx = jnp.arange(L, dtype=F32)
    diff = idx[:, None] - idx[None, :]
    causal = diff >= 0
    decay = jnp.where(causal[None], jnp.exp(jnp.where(causal, diff, 0.0)[None] * log_gamma[:, None, None]), 0.0)
    scores = jnp.einsum('bihd,bjhd->bhij', q, k) * decay[None]
    o = jnp.einsum('bhij,bjhe->bihe', scores, v)
    q_dec = jnp.exp((idx[:, None] + 1.0) * log_gamma[None, :])
    o = o + jnp.einsum('bihd,bhde->bihe', q, S0) * q_dec[None, :, :, None]
    k_dec = jnp.exp((L - 1.0 - idx)[:, None] * log_gamma[None, :])
    S1 = S0 * jnp.exp(L * log_gamma)[None, :, None, None] + jnp.einsum('bjhd,bjhe->bhde', k * k_dec[None, :, :, None], v)
    return o, S1


def retention_prompt(q, k, v, log_gamma):
    B, L, H, Dk = q.shape
    Dv = v.shape[-1]
    nc = L // RET_CHUNK

    def chunks(t):
        return t.astype(F32).reshape(B, nc, RET_CHUNK, H, t.shape[-1]).swapaxes(0, 1)

    def step(S, qkv):
        o, S_new = retention_chunk(S, qkv[0], qkv[1], qkv[2], log_gamma)
        return S_new, o

    S0 = jnp.zeros((B, H, Dk, Dv), F32)
    S, o = lax.scan(step, S0, (chunks(q), chunks(k), chunks(v)))
    return o.swapaxes(0, 1).reshape(B, L, H, Dv), S


def even_in_proj(u, w_in, b_f, pos):
    B, L = u.shape[:2]
    z = u @ w_in
    fq, fk, fv, fl, rq, rk, rv, g = jnp.split(z, EVEN_SPLITS, axis=-1)
    logf = jax.nn.log_sigmoid((fl + b_f).astype(F32))
    rq = rotary(rq.reshape(B, L, H_RET, HEAD_DIM), pos)
    rk = rotary(rk.reshape(B, L, H_RET, HEAD_DIM), pos) * (HEAD_DIM ** -0.5)
    fq = fq.reshape(B, L, H_FOX, HEAD_DIM)
    fk = fk.reshape(B, L, H_FOX, HEAD_DIM)
    fv = fv.reshape(B, L, H_FOX, HEAD_DIM)
    rv = rv.reshape(B, L, H_RET, HEAD_DIM)
    return fq, fk, fv, logf, rq, rk, rv, g


def even_out(fo, ro, g, gn_w, w_out):
    B, L = fo.shape[:2]
    mu = jnp.mean(ro, axis=-1, keepdims=True)
    var = jnp.mean(jnp.square(ro - mu), axis=-1, keepdims=True)
    ro = ((ro - mu) * lax.rsqrt(var + NORM_EPS)).reshape(B, L, RET_W) * gn_w.astype(F32)
    cat = jnp.concatenate([fo.reshape(B, L, FOX_W).astype(F32), ro], axis=-1) * jax.nn.silu(g.astype(F32))
    return cat.astype(w_out.dtype) @ w_out


def gather_pages(pool, table):
    g = pool[table]
    return g.reshape(table.shape[0], table.shape[1] * pool.shape[1], *pool.shape[2:])


def _lin_combine(left, right):
    a1, b1 = left
    a2, b2 = right
    return a1 * a2, a2 * b1 + b2


def rglru_branch(xb, h0, conv_buf, cw, cb, grw, grb, giw, gib, lam):
    B, L, W = xb.shape
    xin = jnp.concatenate([conv_buf.astype(xb.dtype), xb], axis=1)
    xc = lax.conv_general_dilated(xin, cw[:, None, :].astype(xb.dtype), window_strides=(1,), padding='VALID', dimension_numbers=('NWC', 'WIO', 'NWC'), feature_group_count=W) + cb
    new_buf = xin[:, -(CONV_W - 1):]
    xg = xc.reshape(B, L, LRU_BLOCKS, LRU_BW)
    r = jax.nn.sigmoid((jnp.einsum('blnd,nde->blne', xg, grw).reshape(B, L, W) + grb).astype(F32))
    gi = jax.nn.sigmoid((jnp.einsum('blnd,nde->blne', xg, giw).reshape(B, L, W) + gib).astype(F32))
    log_a = -LRU_C * r * jax.nn.softplus(-lam.astype(F32))
    a = jnp.exp(log_a)
    bt = jnp.sqrt(-jnp.expm1(2.0 * log_a)) * gi * xc.astype(F32)
    bt = bt.at[:, 0].add(a[:, 0] * h0.astype(F32))
    _, h = lax.associative_scan(_lin_combine, (a, bt), axis=1)
    return h.astype(xb.dtype), h[:, -1], new_buf


def ple_add(h, p_i, proj, nw, gw):
    e = rms_norm(p_i.astype(proj.dtype) @ proj, nw)
    return h + jax.nn.sigmoid(h @ gw) * e


def setup_inputs(seed: int = 0) -> dict:
    key = jax.random.key(seed)
    keys = jax.random.split(key, 40)
    cnt = iter(range(40))

    def nrm(shape, scale=1.0):
        return scale * jax.random.normal(keys[next(cnt)], shape, F32)

    n_pages = PAST_LEN // PAGE_SIZE
    n_used = DEC_BATCH * n_pages
    n_pool = n_used + (n_used + 3) // 4
    perm = jax.random.permutation(keys[next(cnt)], n_pool)
    page_table = perm[:n_used].reshape(DEC_BATCH, n_pages).astype(jnp.int32)
    a0 = jax.random.uniform(keys[next(cnt)], (N_ODD, LRU_W), F32, minval=0.9, maxval=0.999)
    lru_lambda = jnp.log(a0) - jnp.log1p(-a0)
    return {
        'x_prompt': nrm((BATCH, SEQ, D_MODEL)),
        'x_sample': nrm((DEC_BATCH, DEC_SEQ, D_MODEL)),
        'cache_fox_k': nrm((N_EVEN, n_pool, PAGE_SIZE, H_FOX, HEAD_DIM)),
        'cache_fox_v': nrm((N_EVEN, n_pool, PAGE_SIZE, H_FOX, HEAD_DIM)),
        'cache_fox_logf': jax.nn.log_sigmoid(3.0 + nrm((N_EVEN, n_pool, PAGE_SIZE, H_FOX))),
        'page_table': page_table,
        'state_ret': nrm((N_EVEN, DEC_BATCH, H_RET, HEAD_DIM, HEAD_DIM), 0.5),
        'state_lru_h': nrm((N_ODD, DEC_BATCH, LRU_W), 0.5),
        'state_lru_conv': nrm((N_ODD, DEC_BATCH, CONV_W - 1, LRU_W)),
        'p_prompt': nrm((DEPTH, BATCH, SEQ, PLE_DIM)),
        'p_sample': nrm((DEPTH, DEC_BATCH, DEC_SEQ, PLE_DIM)),
        'norm_w': 1.0 + nrm((DEPTH, D_MODEL), 0.05),
        'w_in_even': nrm((N_EVEN, D_MODEL, EVEN_IN), D_MODEL ** -0.5),
        'b_forget': 3.0 + nrm((N_EVEN, H_FOX), 0.5),
        'ret_gn_w': 1.0 + nrm((N_EVEN, RET_W), 0.05),
        'w_out_even': nrm((N_EVEN, MIX_W, D_MODEL), MIX_W ** -0.5),
        'w_in_odd': nrm((N_ODD, D_MODEL, 2 * LRU_W), D_MODEL ** -0.5),
        'conv_w': nrm((N_ODD, CONV_W, LRU_W), CONV_W ** -0.5),
        'conv_b': nrm((N_ODD, LRU_W), 0.02),
        'gate_r_w': nrm((N_ODD, LRU_BLOCKS, LRU_BW, LRU_BW), LRU_BW ** -0.5),
        'gate_r_b': nrm((N_ODD, LRU_W), 0.02),
        'gate_i_w': nrm((N_ODD, LRU_BLOCKS, LRU_BW, LRU_BW), LRU_BW ** -0.5),
        'gate_i_b': nrm((N_ODD, LRU_W), 0.02),
        'lru_lambda': lru_lambda,
        'w_out_odd': nrm((N_ODD, LRU_W, D_MODEL), LRU_W ** -0.5),
        'ple_proj': nrm((DEPTH, PLE_DIM, D_MODEL), PLE_DIM ** -0.5),
        'ple_norm_w': 1.0 + nrm((DEPTH, D_MODEL), 0.05),
        'ple_gate_w': nrm((DEPTH, D_MODEL, D_MODEL), D_MODEL ** -0.5),
        'final_norm_w': 1.0 + nrm((D_MODEL,), 0.05),
    }


def reference(x_prompt, x_sample, cache_fox_k, cache_fox_v, cache_fox_logf, page_table, state_ret, state_lru_h, state_lru_conv, p_prompt, p_sample, norm_w, w_in_even, b_forget, ret_gn_w, w_out_even, w_in_odd, conv_w, conv_b, gate_r_w, gate_r_b, gate_i_w, gate_i_b, lru_lambda, w_out_odd, ple_proj, ple_norm_w, ple_gate_w, final_norm_w):
    log_gamma = jnp.log1p(-jnp.exp2(-5.0 - jnp.arange(H_RET, dtype=F32)))
    Bp, Lp = x_prompt.shape[:2]
    Bs, Ls = x_sample.shape[:2]
    past_len = page_table.shape[1] * PAGE_SIZE
    pos_p = jnp.arange(Lp, dtype=F32)
    pos_s = past_len + jnp.arange(Ls, dtype=F32)

    hp, hs = x_prompt, x_sample
    fk_p, fv_p, fl_p, rs_p, lh_p, lc_p = [], [], [], [], [], []
    fk_s, fv_s, fl_s, rs_s, lh_s, lc_s = [], [], [], [], [], []
    for i in range(DEPTH):
        j = i // 2
        up = rms_norm(hp, norm_w[i])
        us = rms_norm(hs, norm_w[i])
        if i % 2 == 0:
            fq, fk, fv, fl, rq, rk, rv, g = even_in_proj(up, w_in_even[j], b_forget[j], pos_p)
            fo = fox_prompt(fq, fk, fv, fl)
            ro, S = retention_prompt(rq, rk, rv, log_gamma)
            mp = even_out(fo, ro, g, ret_gn_w[j], w_out_even[j])
            fk_p.append(fk)
            fv_p.append(fv)
            fl_p.append(fl.astype(cache_fox_logf.dtype))
            rs_p.append(S.astype(state_ret.dtype))
            fq, fk, fv, fl, rq, rk, rv, g = even_in_proj(us, w_in_even[j], b_forget[j], pos_s)
            k_past = gather_pages(cache_fox_k[j], page_table)
            v_past = gather_pages(cache_fox_v[j], page_table)
            l_past = gather_pages(cache_fox_logf[j], page_table)
            fo = fox_sample(fq, fk, fv, fl, k_past, v_past, l_past)
            ro, S = retention_chunk(state_ret[j].astype(F32), rq.astype(F32), rk.astype(F32), rv.astype(F32), log_gamma)
            ms = even_out(fo, ro, g, ret_gn_w[j], w_out_even[j])
            fk_s.append(fk)
            fv_s.append(fv)
            fl_s.append(fl.astype(cache_fox_logf.dtype))
            rs_s.append(S.astype(state_ret.dtype))
        else:
            xb, g = jnp.split(up @ w_in_odd[j], [LRU_W], axis=-1)
            y, hT, buf = rglru_branch(xb, jnp.zeros((Bp, LRU_W), F32), jnp.zeros((Bp, CONV_W - 1, LRU_W), xb.dtype), conv_w[j], conv_b[j], gate_r_w[j], gate_r_b[j], gate_i_w[j], gate_i_b[j], lru_lambda[j])
            mp = (y * jax.nn.silu(g)) @ w_out_odd[j]
            lh_p.append(hT.astype(state_lru_h.dtype))
            lc_p.append(buf.astype(state_lru_conv.dtype))
            xb, g = jnp.split(us @ w_in_odd[j], [LRU_W], axis=-1)
            y, hT, buf = rglru_branch(xb, state_lru_h[j], state_lru_conv[j], conv_w[j], conv_b[j], gate_r_w[j], gate_r_b[j], gate_i_w[j], gate_i_b[j], lru_lambda[j])
            ms = (y * jax.nn.silu(g)) @ w_out_odd[j]
            lh_s.append(hT.astype(state_lru_h.dtype))
            lc_s.append(buf.astype(state_lru_conv.dtype))
        hp = ple_add(hp + mp, p_prompt[i], ple_proj[i], ple_norm_w[i], ple_gate_w[i])
        hs = ple_add(hs + ms, p_sample[i], ple_proj[i], ple_norm_w[i], ple_gate_w[i])

    y_prompt = rms_norm(hp, final_norm_w)
    y_sample = rms_norm(hs, final_norm_w)
    new_fox_k_prompt = jnp.stack(fk_p)
    new_fox_v_prompt = jnp.stack(fv_p)
    new_fox_logf_prompt = jnp.stack(fl_p)
    new_ret_state_prompt = jnp.stack(rs_p)
    new_lru_h_prompt = jnp.stack(lh_p)
    new_lru_conv_prompt = jnp.stack(lc_p)
    new_fox_k_sample = jnp.stack(fk_s)
    new_fox_v_sample = jnp.stack(fv_s)
    new_fox_logf_sample = jnp.stack(fl_s)
    new_ret_state_sample = jnp.stack(rs_s)
    new_lru_h_sample = jnp.stack(lh_s)
    new_lru_conv_sample = jnp.stack(lc_s)
    return (y_prompt, y_sample, new_fox_k_prompt, new_fox_v_prompt, new_fox_logf_prompt, new_ret_state_prompt, new_lru_h_prompt, new_lru_conv_prompt, new_fox_k_sample, new_fox_v_sample, new_fox_logf_sample, new_ret_state_sample, new_lru_h_sample, new_lru_conv_sample)
```

```python
import functools
import math

import jax
import jax.numpy as jnp
from jax import lax
from jax.experimental import pallas as pl
from jax.experimental.pallas import tpu as pltpu

F32 = jnp.float32
BF16 = jnp.bfloat16

HEAD_DIM = 64
HALF_DIM = HEAD_DIM // 2
N_HEADS = 8
HEAD_W = N_HEADS * HEAD_DIM
PAIR_W = 2 * HEAD_DIM
N_PAIRS = N_HEADS // 2
PAGE = 128
PAGES_PER_STEP = 8
RET_CHUNK = 128
ROPE_BASE = 10000.0
LRU_BLOCKS = 8
LRU_BW = 128
CONV_W = 4
LRU_C = 8.0
NORM_EPS = 1e-6
NEG_INF = -1e30
QK_SCALE = HEAD_DIM ** -0.5
VMEM_LIMIT = 56 * 1024 * 1024


def _params(*sem):
    return pltpu.CompilerParams(dimension_semantics=sem, vmem_limit_bytes=VMEM_LIMIT)


def _mm(a, b):
    return jnp.dot(a, b, preferred_element_type=F32)


def _mm_nt(a, b):
    return lax.dot_general(a, b, (((1,), (1,)), ((), ())), preferred_element_type=F32)


def _rms(x, w):
    return (x * lax.rsqrt(jnp.mean(x * x, axis=-1, keepdims=True) + NORM_EPS)) * w


def _const_spec(shape):
    nd = len(shape)
    return pl.BlockSpec(shape, lambda *_: (0,) * nd)


def _swap_halves(x):
    n = x.shape[-1]
    lane = lax.broadcasted_iota(jnp.int32, x.shape, x.ndim - 1)
    return jnp.where(lane % HEAD_DIM < HALF_DIM,
                     pltpu.roll(x, n - HALF_DIM, x.ndim - 1),
                     pltpu.roll(x, HALF_DIM, x.ndim - 1))


def _even_in_kernel(h_ref, nw_ref, wf_ref, wl_ref, wr_ref, wg_ref, bf_ref, cos_ref, sin_ref,
                    fq_ref, fk_ref, fv_ref, fkb_ref, fvb_ref, lf_ref, lft_ref,
                    rq_ref, rk_ref, rv_ref, g_ref):
    ub = _rms(h_ref[...], nw_ref[...]).astype(BF16)
    zf = _mm(ub, wf_ref[...])
    fq_ref[...] = (zf[:, :HEAD_W] * QK_SCALE).astype(BF16)
    fk = zf[:, HEAD_W:2 * HEAD_W]
    fv = zf[:, 2 * HEAD_W:]
    fk_ref[...] = fk
    fv_ref[...] = fv
    fkb_ref[...] = fk.astype(BF16)
    fvb_ref[...] = fv.astype(BF16)
    logf = jax.nn.log_sigmoid(_mm(ub, wl_ref[...]) + bf_ref[...])
    lf_ref[...] = logf[:, :N_HEADS]
    lft_ref[...] = logf.T[:N_HEADS]
    zr = _mm(ub, wr_ref[...])
    cos = cos_ref[...]
    sin = sin_ref[...]
    rq = zr[:, :HEAD_W]
    rk = zr[:, HEAD_W:2 * HEAD_W]
    rq_ref[...] = (rq * cos + _swap_halves(rq) * sin).astype(BF16)
    rk_ref[...] = (rk * cos + _swap_halves(rk) * sin) * QK_SCALE
    rv_ref[...] = zr[:, 2 * HEAD_W:].astype(BF16)
    g_ref[...] = _mm(ub, wg_ref[...])


def _even_in(h, nw, wf, wl, wr, wg, bf, cos, sin, *, tm, pos_blocks):
    rows, d = h.shape
    row = lambda w: pl.BlockSpec((tm, w), lambda i: (i, 0))
    pos = pl.BlockSpec((tm, HEAD_W), lambda i: (i % pos_blocks, 0))
    sd = jax.ShapeDtypeStruct
    return pl.pallas_call(
        _even_in_kernel,
        grid=(rows // tm,),
        in_specs=[row(d), _const_spec(nw.shape), _const_spec(wf.shape), _const_spec(wl.shape),
                  _const_spec(wr.shape), _const_spec(wg.shape), _const_spec(bf.shape), pos, pos],
        out_specs=[row(HEAD_W), row(HEAD_W), row(HEAD_W), row(HEAD_W), row(HEAD_W),
                   row(N_HEADS), pl.BlockSpec((N_HEADS, tm), lambda i: (0, i)),
                   row(HEAD_W), row(HEAD_W), row(HEAD_W), row(2 * HEAD_W)],
        out_shape=[sd((rows, HEAD_W), BF16), sd((rows, HEAD_W), F32), sd((rows, HEAD_W), F32),
                   sd((rows, HEAD_W), BF16), sd((rows, HEAD_W), BF16),
                   sd((rows, N_HEADS), F32), sd((N_HEADS, rows), F32),
                   sd((rows, HEAD_W), BF16), sd((rows, HEAD_W), F32), sd((rows, HEAD_W), BF16),
                   sd((rows, 2 * HEAD_W), F32)],
        compiler_params=_params("parallel"),
        name="even_in",
    )(h, nw, wf, wl, wr, wg, bf, cos, sin)


def _seg_cumsum_kernel(x_ref, o_ref, *, seg):
    x = x_ref[...]
    pos = lax.broadcasted_iota(jnp.int32, x.shape, 1) % seg
    s = 1
    while s < seg:
        x = x + jnp.where(pos >= s, pltpu.roll(x, s, 1), 0.0)
        s *= 2
    o_ref[...] = x


def _seg_cumsum(x, *, seg, blk):
    r, n = x.shape
    return pl.pallas_call(
        functools.partial(_seg_cumsum_kernel, seg=seg),
        grid=(n // blk,),
        in_specs=[pl.BlockSpec((r, blk), lambda i: (0, i))],
        out_specs=pl.BlockSpec((r, blk), lambda i: (0, i)),
        out_shape=jax.ShapeDtypeStruct((r, n), F32),
        compiler_params=_params("parallel"),
        name="seg_cumsum",
    )(x)


def _pair_masks(shape):
    lane = lax.broadcasted_iota(jnp.int32, shape, len(shape) - 1)
    return lane < HEAD_DIM


def _fox_prompt_kernel(q_ref, k_ref, v_ref, cr_ref, cc_ref, o_ref, *, tq):
    qi = pl.program_id(2)
    q = q_ref[...]
    first = _pair_masks(q.shape)
    zero = jnp.zeros_like(q)
    q_heads = (jnp.where(first, q, zero), jnp.where(first, zero, q))
    cc = cc_ref[...]
    c_query = (cc[:, 0:1], cc[:, 1:2])
    row = lax.broadcasted_iota(jnp.int32, (tq, tq), 0)
    col = lax.broadcasted_iota(jnp.int32, (tq, tq), 1)
    causal = row >= col

    def step(j, carry, diagonal):
        start = pl.multiple_of(j * tq, tq)
        ks = k_ref[pl.ds(start, tq), :]
        vs = v_ref[pl.ds(start, tq), :]
        c_key = cr_ref[:, pl.ds(start, tq)]
        new = []
        for h in range(2):
            m, l, acc = carry[h]
            s = (_mm_nt(q_heads[h], ks) + c_query[h]) - c_key[h:h + 1, :]
            if diagonal:
                s = jnp.where(causal, s, NEG_INF)
            m_new = jnp.maximum(m, jnp.max(s, axis=-1, keepdims=True))
            alpha = jnp.exp(m - m_new)
            p = jnp.exp(s - m_new)
            l = alpha * l + jnp.sum(p, axis=-1, keepdims=True)
            acc = alpha * acc + _mm(p.astype(BF16), vs)
            new.append((m_new, l, acc))
        return tuple(new)

    init_h = (jnp.full((tq, 1), NEG_INF, F32), jnp.zeros((tq, 1), F32), jnp.zeros((tq, PAIR_W), F32))
    carry = lax.fori_loop(0, qi, lambda j, c: step(j, c, False), (init_h, init_h))
    (_, l0, a0), (_, l1, a1) = step(qi, carry, True)
    o_ref[...] = jnp.where(first, a0 / l0, a1 / l1)


def _fox_prompt(q, kb, vb, cum_rows, cum_cols, *, tq):
    b, l, _ = q.shape
    return pl.pallas_call(
        functools.partial(_fox_prompt_kernel, tq=tq),
        grid=(b, N_PAIRS, l // tq),
        in_specs=[pl.BlockSpec((None, tq, PAIR_W), lambda bi, hp, qi: (bi, qi, hp)),
                  pl.BlockSpec((None, l, PAIR_W), lambda bi, hp, qi: (bi, 0, hp)),
                  pl.BlockSpec((None, l, PAIR_W), lambda bi, hp, qi: (bi, 0, hp)),
                  pl.BlockSpec((None, None, 2, l), lambda bi, hp, qi: (bi, hp, 0, 0)),
                  pl.BlockSpec((None, None, tq, 2), lambda bi, hp, qi: (bi, hp, qi, 0))],
        out_specs=pl.BlockSpec((None, tq, PAIR_W), lambda bi, hp, qi: (bi, qi, hp)),
        out_shape=jax.ShapeDtypeStruct((b, l, HEAD_W), F32),
        compiler_params=_params("parallel", "parallel", "arbitrary"),
        name="fox_prompt",
    )(q, kb, vb, cum_rows, cum_cols)


def _ret_tables(lg_ref, length):
    n = RET_CHUNK
    lg_a = lg_ref[0:1, :]
    lg_b = lg_ref[1:2, :]
    lg_m = lg_ref[2:3, :]
    row = lax.broadcasted_iota(jnp.int32, (n, n), 0)
    col = lax.broadcasted_iota(jnp.int32, (n, n), 1)
    diff = row - col
    lower = diff >= 0
    dpos = jnp.where(lower, diff, 0).astype(F32)
    decay = (jnp.where(lower, jnp.exp(dpos * lg_a), 0.0), jnp.where(lower, jnp.exp(dpos * lg_b), 0.0))
    rowf = row.astype(F32)
    q_dec = jnp.exp((rowf + 1.0) * lg_m)
    k_dec = jnp.exp((length - 1.0 - rowf) * lg_m)
    s_dec = jnp.exp(float(length) * jnp.where(row < HEAD_DIM, lg_a, lg_b))
    same_head = (row < HEAD_DIM) == (col < HEAD_DIM)
    return decay, q_dec, k_dec, s_dec, same_head


def _ret_chunk(q, k, v, state, tables):
    decay, q_dec, k_dec, s_dec, same_head = tables
    first = _pair_masks(q.shape)
    zero = jnp.zeros_like(q)
    kb = k.astype(BF16)
    o_heads = []
    for h, qh in enumerate((jnp.where(first, q, zero), jnp.where(first, zero, q))):
        scores = _mm_nt(qh, kb) * decay[h]
        o_heads.append(_mm(scores.astype(BF16), v))
    o = jnp.where(first, o_heads[0], o_heads[1])
    o = o + _mm(q, state.astype(BF16)) * q_dec
    kd_t = (k * k_dec).T.astype(BF16)
    state = state * s_dec + jnp.where(same_head, _mm(kd_t, v), 0.0)
    return o, state


def _group_norm(o, gn_w):
    first = _pair_masks(o.shape)
    inv = 1.0 / HEAD_DIM
    s0 = jnp.sum(jnp.where(first, o, 0.0), axis=-1, keepdims=True)
    s1 = jnp.sum(jnp.where(first, 0.0, o), axis=-1, keepdims=True)
    d = o - jnp.where(first, s0, s1) * inv
    dd = d * d
    v0 = jnp.sum(jnp.where(first, dd, 0.0), axis=-1, keepdims=True)
    v1 = jnp.sum(jnp.where(first, 0.0, dd), axis=-1, keepdims=True)
    var = jnp.where(first, v0, v1) * inv
    return (d * lax.rsqrt(var + NORM_EPS)) * gn_w


def _ret_prompt_kernel(q_ref, k_ref, v_ref, lg_ref, gn_ref, o_ref, s_ref, *, n_chunks):
    tables = _ret_tables(lg_ref, RET_CHUNK)
    gn_w = gn_ref[...]

    def body(c, state):
        rows = pl.ds(pl.multiple_of(c * RET_CHUNK, RET_CHUNK), RET_CHUNK)
        o, state = _ret_chunk(q_ref[rows, :], k_ref[rows, :], v_ref[rows, :], state, tables)
        o_ref[rows, :] = _group_norm(o, gn_w)
        return state

    state = lax.fori_loop(0, n_chunks, body, jnp.zeros((PAIR_W, PAIR_W), F32))
    s_ref[0] = state[:HEAD_DIM, :HEAD_DIM]
    s_ref[1] = state[HEAD_DIM:, HEAD_DIM:]


def _ret_prompt(rq, rk, rv, lg_tab, gn_w):
    b, l, _ = rq.shape
    seq = pl.BlockSpec((None, l, PAIR_W), lambda bi, hp: (bi, 0, hp))
    return pl.pallas_call(
        functools.partial(_ret_prompt_kernel, n_chunks=l // RET_CHUNK),
        grid=(b, N_PAIRS),
        in_specs=[seq, seq, seq,
                  pl.BlockSpec((None, 8, PAIR_W), lambda bi, hp: (hp, 0, 0)),
                  pl.BlockSpec((1, PAIR_W), lambda bi, hp: (0, hp))],
        out_specs=[seq, pl.BlockSpec((None, 2, HEAD_DIM, HEAD_DIM), lambda bi, hp: (bi, hp, 0, 0))],
        out_shape=[jax.ShapeDtypeStruct((b, l, HEAD_W), F32),
                   jax.ShapeDtypeStruct((b, N_HEADS, HEAD_DIM, HEAD_DIM), F32)],
        compiler_params=_params("parallel", "parallel"),
        name="ret_prompt",
    )(rq, rk, rv, lg_tab, gn_w)


def _ret_sample_kernel(q_ref, k_ref, v_ref, s0_ref, lg_ref, gn_ref, o_ref, s_ref, *, length):
    pad = RET_CHUNK - length
    zrow_f = jnp.zeros((pad, PAIR_W), F32)
    zblk = jnp.zeros((HEAD_DIM, HEAD_DIM), F32)
    for hp in range(N_PAIRS):
        lanes = slice(hp * PAIR_W, (hp + 1) * PAIR_W)
        tables = _ret_tables(lg_ref.at[hp], length)
        q = jnp.concatenate([q_ref[:, lanes].astype(F32), zrow_f], axis=0).astype(BF16)
        k = jnp.concatenate([k_ref[:, lanes], zrow_f], axis=0)
        v = jnp.concatenate([v_ref[:, lanes].astype(F32), zrow_f], axis=0).astype(BF16)
        state = jnp.concatenate(
            [jnp.concatenate([s0_ref[2 * hp], zblk], axis=1),
             jnp.concatenate([zblk, s0_ref[2 * hp + 1]], axis=1)], axis=0)
        o, state = _ret_chunk(q, k, v, state, tables)
        o_ref[:, lanes] = _group_norm(o[:length], gn_ref[:, lanes])
        s_ref[2 * hp] = state[:HEAD_DIM, :HEAD_DIM]
        s_ref[2 * hp + 1] = state[HEAD_DIM:, HEAD_DIM:]


def _ret_sample(rq, rk, rv, state, lg_tab, gn_w, *, length):
    rows = rq.shape[0]
    b = rows // length
    tok = pl.BlockSpec((length, HEAD_W), lambda bi: (bi, 0))
    st = pl.BlockSpec((None, N_HEADS, HEAD_DIM, HEAD_DIM), lambda bi: (bi, 0, 0, 0))
    return pl.pallas_call(
        functools.partial(_ret_sample_kernel, length=length),
        grid=(b,),
        in_specs=[tok, tok, tok, st, _const_spec(lg_tab.shape), _const_spec(gn_w.shape)],
        out_specs=[tok, st],
        out_shape=[jax.ShapeDtypeStruct((rows, HEAD_W), F32),
                   jax.ShapeDtypeStruct(state.shape, F32)],
        compiler_params=_params("parallel"),
        name="ret_sample",
    )(rq, rk, rv, state, lg_tab, gn_w)


def _ple(h1, p, proj, pnw, pgw):
    e = _rms(_mm(p.astype(BF16), proj), pnw)
    return h1 + jax.nn.sigmoid(_mm(h1.astype(BF16), pgw)) * e


def _silu(g):
    return g * jax.nn.sigmoid(g)


def _even_out_kernel(h_ref, fo_ref, ro_ref, g_ref, wo_ref, p_ref, proj_ref, pnw_ref, pgw_ref, o_ref):
    sg = _silu(g_ref[...])
    cat_f = (fo_ref[...] * sg[:, :HEAD_W]).astype(BF16)
    cat_r = (ro_ref[...] * sg[:, HEAD_W:]).astype(BF16)
    mix = _mm(cat_f, wo_ref[:HEAD_W, :]) + _mm(cat_r, wo_ref[HEAD_W:, :])
    o_ref[...] = _ple(h_ref[...] + mix, p_ref[...], proj_ref[...], pnw_ref[...], pgw_ref[...])


def _even_out(h, fo, ro, g, wo, p, proj, pnw, pgw, *, tm):
    rows, d = h.shape
    row = lambda w: pl.BlockSpec((tm, w), lambda i: (i, 0))
    return pl.pallas_call(
        _even_out_kernel,
        grid=(rows // tm,),
        in_specs=[row(d), row(HEAD_W), row(HEAD_W), row(2 * HEAD_W), _const_spec(wo.shape),
                  row(p.shape[1]), _const_spec(proj.shape), _const_spec(pnw.shape), _const_spec(pgw.shape)],
        out_specs=row(d),
        out_shape=jax.ShapeDtypeStruct((rows, d), F32),
        compiler_params=_params("parallel"),
        name="even_out",
    )(h, fo, ro, g, wo, p, proj, pnw, pgw)


def _odd_kernel(h_ref, nw_ref, win_ref, cw_ref, cb_ref, gw_ref, grb_ref, gib_ref, lam_ref, wout_ref,
                p_ref, proj_ref, pnw_ref, pgw_ref, h0_ref, cbuf_ref, fnw_ref,
                o_ref, lruh_ref, lruc_ref,
                ext_scr, carry_scr, a_scr, b_scr, sg_scr, *, tl, final_norm):
    w = h_ref.shape[-1]
    halo = 8

    @pl.when(pl.program_id(1) == 0)
    def _():
        carry_scr[...] = jnp.broadcast_to(h0_ref[...], carry_scr.shape)
        ext_scr[0:halo, :] = jnp.zeros((halo, w), F32)
        ext_scr[halo - (CONV_W - 1):halo, :] = cbuf_ref[...]

    x = h_ref[...]
    z = _mm(_rms(x, nw_ref[...]).astype(BF16), win_ref[...])
    xb = z[:, :w]
    sg_scr[...] = _silu(z[:, w:])
    ext_scr[halo:halo + tl, :] = xb
    xc = xb * cw_ref[CONV_W - 1:CONV_W, :] + cb_ref[...]
    for kk in range(1, CONV_W):
        xc = xc + ext_scr[halo - kk:halo - kk + tl, :] * cw_ref[CONV_W - 1 - kk:CONV_W - kk, :]
    lruc_ref[...] = ext_scr[halo + tl - (CONV_W - 1):halo + tl, :]
    ext_scr[0:halo, :] = xb[tl - halo:, :]

    sp = jax.nn.softplus(-lam_ref[...])
    for n in range(LRU_BLOCKS):
        lanes = slice(n * LRU_BW, (n + 1) * LRU_BW)
        xn = xc[:, lanes]
        zz = _mm(xn.astype(BF16), gw_ref[n])
        r = jax.nn.sigmoid(zz[:, :LRU_BW] + grb_ref[:, lanes])
        gi = jax.nn.sigmoid(zz[:, LRU_BW:] + gib_ref[:, lanes])
        log_a = (-LRU_C * r) * sp[:, lanes]
        a = jnp.exp(log_a)
        one_minus = -jnp.tanh(log_a) * (a * a + 1.0)
        a_scr[:, lanes] = a
        b_scr[:, lanes] = (jnp.sqrt(one_minus) * gi) * xn

    sub = lax.broadcasted_iota(jnp.int32, (8, w), 0)

    def group(gidx, carry):
        rows = pl.ds(pl.multiple_of(gidx * 8, 8), 8)
        a = a_scr[rows, :]
        b = b_scr[rows, :]
        for s in (1, 2, 4):
            keep = sub >= s
            b = a * jnp.where(keep, pltpu.roll(b, s, 0), 0.0) + b
            a = a * jnp.where(keep, pltpu.roll(a, s, 0), 1.0)
        hs = a * carry + b
        b_scr[rows, :] = hs
        return jnp.broadcast_to(hs[7:8, :], (8, w))

    carry = lax.fori_loop(0, tl // 8, group, carry_scr[...])
    carry_scr[...] = carry
    lruh_ref[...] = carry[0:1, :]

    mix = _mm((b_scr[...] * sg_scr[...]).astype(BF16), wout_ref[...])
    h2 = _ple(x + mix, p_ref[...], proj_ref[...], pnw_ref[...], pgw_ref[...])
    if final_norm:
        h2 = _rms(h2, fnw_ref[...])
    o_ref[...] = h2


def _odd_layer(h, nw, win, cw, cb, gw, grb, gib, lam, wout, p, proj, pnw, pgw, h0, cbuf, fnw,
               *, tl, final_norm):
    b, l, d = h.shape
    w = wout.shape[0]
    seq = lambda width: pl.BlockSpec((None, tl, width), lambda bi, li: (bi, li, 0))
    per_b = lambda r: pl.BlockSpec((None, r, w), lambda bi, li: (bi, 0, 0))
    consts = [nw, win, cw, cb, gw, grb, gib, lam, wout]
    tail = [proj, pnw, pgw]
    return pl.pallas_call(
        functools.partial(_odd_kernel, tl=tl, final_norm=final_norm),
        grid=(b, l // tl),
        in_specs=[seq(d)] + [_const_spec(c.shape) for c in consts] + [seq(p.shape[-1])]
                 + [_const_spec(c.shape) for c in tail] + [per_b(1), per_b(CONV_W - 1), _const_spec(fnw.shape)],
        out_specs=[seq(d), per_b(1), per_b(CONV_W - 1)],
        out_shape=[jax.ShapeDtypeStruct((b, l, d), F32),
                   jax.ShapeDtypeStruct((b, 1, w), F32),
                   jax.ShapeDtypeStruct((b, CONV_W - 1, w), F32)],
        scratch_shapes=[pltpu.VMEM((tl + 8, w), F32), pltpu.VMEM((8, w), F32),
                        pltpu.VMEM((tl, w), F32), pltpu.VMEM((tl, w), F32), pltpu.VMEM((tl, w), F32)],
        compiler_params=_params("parallel", "arbitrary"),
        name="odd_layer",
    )(h, *consts, p, *tail, h0, cbuf, fnw)


def _rel_kernel(tbl_ref, *refs):
    pages = refs[:PAGES_PER_STEP]
    o_ref = refs[PAGES_PER_STEP]
    carry_scr = refs[PAGES_PER_STEP + 1]
    del tbl_ref

    @pl.when(pl.program_id(1) == 0)
    def _():
        carry_scr[...] = jnp.zeros_like(carry_scr)

    lane = lax.broadcasted_iota(jnp.int32, (N_HEADS, PAGE), 1)
    carry = carry_scr[...]
    for i in reversed(range(PAGES_PER_STEP)):
        x = pages[i][...]
        y = x
        s = 1
        while s < PAGE:
            y = y + jnp.where(lane + s < PAGE, pltpu.roll(y, PAGE - s, 1), 0.0)
            s *= 2
        o_ref[:, i * PAGE:(i + 1) * PAGE] = (y - x) + carry
        carry = carry + jnp.broadcast_to(y[:, 0:1], carry.shape)
    carry_scr[...] = carry


def _past_rel(logf_t, table, layer):
    b, n_pages = table.shape
    n_steps = n_pages // PAGES_PER_STEP

    def page_spec(i):
        return pl.BlockSpec(
            (None, None, N_HEADS, PAGE),
            lambda bi, c, tbl: (layer, tbl[bi, (n_steps - 1 - c) * PAGES_PER_STEP + i], 0, 0))

    return pl.pallas_call(
        _rel_kernel,
        grid_spec=pltpu.PrefetchScalarGridSpec(
            num_scalar_prefetch=1,
            grid=(b, n_steps),
            in_specs=[page_spec(i) for i in range(PAGES_PER_STEP)],
            out_specs=pl.BlockSpec((None, N_HEADS, PAGES_PER_STEP * PAGE),
                                   lambda bi, c, tbl: (bi, 0, n_steps - 1 - c)),
            scratch_shapes=[pltpu.VMEM((N_HEADS, PAGE), F32)]),
        out_shape=jax.ShapeDtypeStruct((b, N_HEADS, n_pages * PAGE), F32),
        compiler_params=_params("parallel", "arbitrary"),
        name="past_rel",
    )(table, *([logf_t] * PAGES_PER_STEP))


def _fox_sample_kernel(tbl_ref, q_ref, kn_ref, vn_ref, cnc_ref, cnr_ref, rel_ref, *refs, t_new):
    k_pages = refs[:PAGES_PER_STEP]
    v_pages = refs[PAGES_PER_STEP:2 * PAGES_PER_STEP]
    o_ref = refs[2 * PAGES_PER_STEP]
    qbd_scr, m_scr, l_scr, acc_scr = refs[2 * PAGES_PER_STEP + 1:]
    del tbl_ref
    c = pl.program_id(1)
    n_rows = N_HEADS * t_new
    cn_col = cnc_ref[...]

    @pl.when(c == 0)
    def _():
        q = q_ref[...].astype(F32)
        lane_head = lax.broadcasted_iota(jnp.int32, q.shape, 1) // HEAD_DIM
        zero = jnp.zeros_like(q)
        for h in range(N_HEADS):
            qbd_scr[h * t_new:(h + 1) * t_new, :] = jnp.where(lane_head == h, q, zero)
        m_scr[...] = jnp.full(m_scr.shape, NEG_INF, F32)
        l_scr[...] = jnp.zeros_like(l_scr)
        acc_scr[...] = jnp.zeros_like(acc_scr)

    qbd = qbd_scr[...].astype(BF16)

    def online(s, pv_fn):
        m = m_scr[...]
        m_new = jnp.maximum(m, jnp.max(s, axis=-1, keepdims=True))
        alpha = jnp.exp(m - m_new)
        p = jnp.exp(s - m_new)
        l_scr[...] = alpha * l_scr[...] + jnp.sum(p, axis=-1, keepdims=True)
        acc_scr[...] = alpha * acc_scr[...] + pv_fn(p.astype(BF16))
        m_scr[...] = m_new

    scores = []
    for i in range(PAGES_PER_STEP):
        rel = rel_ref[:, i * PAGE:(i + 1) * PAGE]
        rel_rows = jnp.concatenate(
            [jnp.broadcast_to(rel[h:h + 1, :], (t_new, PAGE)) for h in range(N_HEADS)], axis=0)
        s = _mm_nt(qbd, k_pages[i][...].astype(BF16))
        scores.append((s + cn_col) + rel_rows)

    def pv_past(p):
        out = _mm(p[:, :PAGE], v_pages[0][...].astype(BF16))
        for i in range(1, PAGES_PER_STEP):
            out = out + _mm(p[:, i * PAGE:(i + 1) * PAGE], v_pages[i][...].astype(BF16))
        return out

    online(jnp.concatenate(scores, axis=1), pv_past)

    @pl.when(c == pl.num_programs(1) - 1)
    def _():
        zpad = jnp.zeros((PAGE - t_new, HEAD_W), F32)
        kn = jnp.concatenate([kn_ref[...], zpad], axis=0).astype(BF16)
        vn = jnp.concatenate([vn_ref[...], zpad], axis=0).astype(BF16)
        s = (_mm_nt(qbd, kn) + cn_col) - cnr_ref[...]
        t_query = lax.broadcasted_iota(jnp.int32, s.shape, 0) % t_new
        t_key = lax.broadcasted_iota(jnp.int32, s.shape, 1)
        s = jnp.where(t_query >= t_key, s, NEG_INF)
        online(s, lambda p: _mm(p, vn))
        o = acc_scr[...] / l_scr[...]
        lane_head = lax.broadcasted_iota(jnp.int32, (t_new, HEAD_W), 1) // HEAD_DIM
        out = jnp.zeros((t_new, HEAD_W), F32)
        for h in range(N_HEADS):
            out = jnp.where(lane_head == h, o[h * t_new:(h + 1) * t_new, :], out)
        o_ref[...] = out


def _fox_sample(q, k_new, v_new, cn_col, cn_rows, rel, cache_k, cache_v, table, layer, *, t_new):
    rows = q.shape[0]
    b, n_pages = table.shape
    n_steps = n_pages // PAGES_PER_STEP
    n_rows = N_HEADS * t_new
    tok = pl.BlockSpec((t_new, HEAD_W), lambda bi, c, tbl: (bi, 0))

    def page_spec(i):
        return pl.BlockSpec((None, None, PAGE, HEAD_W),
                            lambda bi, c, tbl: (layer, tbl[bi, c * PAGES_PER_STEP + i], 0, 0))

    pages = [page_spec(i) for i in range(PAGES_PER_STEP)]
    return pl.pallas_call(
        functools.partial(_fox_sample_kernel, t_new=t_new),
        grid_spec=pltpu.PrefetchScalarGridSpec(
            num_scalar_prefetch=1,
            grid=(b, n_steps),
            in_specs=[tok, tok, tok,
                      pl.BlockSpec((None, n_rows, 1), lambda bi, c, tbl: (bi, 0, 0)),
                      pl.BlockSpec((None, n_rows, PAGE), lambda bi, c, tbl: (bi, 0, 0)),
                      pl.BlockSpec((None, N_HEADS, PAGES_PER_STEP * PAGE), lambda bi, c, tbl: (bi, 0, c))]
                     + pages + pages,
            out_specs=tok,
            scratch_shapes=[pltpu.VMEM((n_rows, HEAD_W), F32), pltpu.VMEM((n_rows, 1), F32),
                            pltpu.VMEM((n_rows, 1), F32), pltpu.VMEM((n_rows, HEAD_W), F32)]),
        out_shape=jax.ShapeDtypeStruct((rows, HEAD_W), F32),
        compiler_params=_params("parallel", "arbitrary"),
        name="fox_sample",
    )(table, q, k_new, v_new, cn_col, cn_rows, rel,
      *([cache_k] * PAGES_PER_STEP), *([cache_v] * PAGES_PER_STEP))


def _rope_tables(pos):
    inv = ROPE_BASE ** (-jnp.arange(HALF_DIM, dtype=F32) / HALF_DIM)
    ang = pos[:, None] * inv[None, :]
    cos = jnp.cos(ang)
    sin = jnp.sin(ang)
    cos_head = jnp.concatenate([cos, cos], axis=-1)
    sin_head = jnp.concatenate([-sin, sin], axis=-1)
    return jnp.tile(cos_head, (1, N_HEADS)), jnp.tile(sin_head, (1, N_HEADS))


def _log_gamma_table(log_gamma):
    pairs = log_gamma.reshape(N_PAIRS, 2)
    ones = jnp.ones((N_PAIRS, 1, PAIR_W), F32)
    first = pairs[:, 0][:, None, None] * ones
    second = pairs[:, 1][:, None, None] * ones
    mix = jnp.repeat(pairs, HEAD_DIM, axis=1)[:, None, :]
    return jnp.concatenate([first, second, mix, jnp.zeros((N_PAIRS, 5, PAIR_W), F32)], axis=1)


def kernel(x_prompt, x_sample, cache_fox_k, cache_fox_v, cache_fox_logf, page_table, state_ret, state_lru_h, state_lru_conv, p_prompt, p_sample, norm_w, w_in_even, b_forget, ret_gn_w, w_out_even, w_in_odd, conv_w, conv_b, gate_r_w, gate_r_b, gate_i_w, gate_i_b, lru_lambda, w_out_odd, ple_proj, ple_norm_w, ple_gate_w, final_norm_w):
    bp, lp, d = x_prompt.shape
    bs, ls, _ = x_sample.shape
    depth = norm_w.shape[0]
    n_pool = cache_fox_k.shape[1]
    n_pages = page_table.shape[1]
    past_len = n_pages * PAGE
    lru_w = w_out_odd.shape[1]
    rows_p, rows_s = bp * lp, bs * ls
    tm_p = 256

    log_gamma = jnp.log1p(-jnp.exp2(-5.0 - jnp.arange(N_HEADS, dtype=F32)))
    lg_tab = _log_gamma_table(log_gamma)
    cos_p, sin_p = _rope_tables(jnp.arange(lp, dtype=F32))
    cos_s, sin_s = _rope_tables(past_len + jnp.arange(ls, dtype=F32))
    cos_s, sin_s = jnp.tile(cos_s, (bs, 1)), jnp.tile(sin_s, (bs, 1))

    cache_k = cache_fox_k.reshape(cache_fox_k.shape[0], n_pool, PAGE, HEAD_W)
    cache_v = cache_fox_v.reshape(cache_fox_v.shape[0], n_pool, PAGE, HEAD_W)
    cache_lt = jnp.swapaxes(cache_fox_logf, 2, 3)

    row2 = lambda v: v.reshape(1, -1)
    hp = x_prompt.reshape(rows_p, d)
    hs = x_sample.reshape(rows_s, d)
    outs = {k: [] for k in ("fk_p", "fv_p", "fl_p", "rs_p", "lh_p", "lc_p",
                            "fk_s", "fv_s", "fl_s", "rs_s", "lh_s", "lc_s")}
    y_p = y_s = None

    for i in range(depth):
        j = i // 2
        proj = ple_proj[i].astype(BF16)
        pgw = ple_gate_w[i].astype(BF16)
        pnw = row2(ple_norm_w[i])
        nw = row2(norm_w[i])
        pp = p_prompt[i].reshape(rows_p, -1)
        ps = p_sample[i].reshape(rows_s, -1)
        if i % 2 == 0:
            w = w_in_even[j]
            c0 = 3 * HEAD_W
            wf = w[:, :c0].astype(BF16)
            wl = jnp.pad(w[:, c0:c0 + N_HEADS], ((0, 0), (0, PAGE - N_HEADS))).astype(BF16)
            wr = w[:, c0 + N_HEADS:2 * c0 + N_HEADS].astype(BF16)
            wg = w[:, 2 * c0 + N_HEADS:].astype(BF16)
            bf = jnp.pad(b_forget[j], (0, PAGE - N_HEADS)).reshape(1, PAGE)
            wo = w_out_even[j].astype(BF16)
            gn = row2(ret_gn_w[j])

            fq, fk, fv, fkb, fvb, lf, lft, rq, rk, rv, g = _even_in(
                hp, nw, wf, wl, wr, wg, bf, cos_p, sin_p, tm=tm_p, pos_blocks=lp // tm_p)
            cum = _seg_cumsum(lft, seg=lp, blk=lp).reshape(N_PAIRS, 2, bp, lp)
            cum_rows = cum.transpose(2, 0, 1, 3)
            cum_cols = cum.transpose(2, 0, 3, 1)
            b3 = lambda a: a.reshape(bp, lp, HEAD_W)
            fo = _fox_prompt(b3(fq), b3(fkb), b3(fvb), cum_rows, cum_cols, tq=256)
            ro, s_p = _ret_prompt(b3(rq), b3(rk), b3(rv), lg_tab, gn)
            hp = _even_out(hp, fo.reshape(rows_p, HEAD_W), ro.reshape(rows_p, HEAD_W), g, wo,
                           pp, proj, pnw, pgw, tm=tm_p)
            outs["fk_p"].append(fk.reshape(bp, lp, N_HEADS, HEAD_DIM))
            outs["fv_p"].append(fv.reshape(bp, lp, N_HEADS, HEAD_DIM))
            outs["fl_p"].append(lf.reshape(bp, lp, N_HEADS))
            outs["rs_p"].append(s_p)

            fq, fk, fv, fkb, fvb, lf, lft, rq, rk, rv, g = _even_in(
                hs, nw, wf, wl, wr, wg, bf, cos_s, sin_s, tm=rows_s, pos_blocks=1)
            cn = _seg_cumsum(lft, seg=ls, blk=rows_s).reshape(N_HEADS, bs, ls).transpose(1, 0, 2)
            cn_col = cn.reshape(bs, N_HEADS * ls, 1)
            cn_rows = jnp.pad(jnp.repeat(cn, ls, axis=1), ((0, 0), (0, 0), (0, PAGE - ls)))
            rel = _past_rel(cache_lt, page_table, j)
            fo = _fox_sample(fq, fk, fv, cn_col, cn_rows, rel, cache_k, cache_v, page_table, j, t_new=ls)
            ro, s_s = _ret_sample(rq, rk, rv, state_ret[j], lg_tab, gn, length=ls)
            hs = _even_out(hs, fo, ro, g, wo, ps, proj, pnw, pgw, tm=rows_s)
            outs["fk_s"].append(fk.reshape(bs, ls, N_HEADS, HEAD_DIM))
            outs["fv_s"].append(fv.reshape(bs, ls, N_HEADS, HEAD_DIM))
            outs["fl_s"].append(lf.reshape(bs, ls, N_HEADS))
            outs["rs_s"].append(s_s)
        else:
            final = i == depth - 1
            gw = jnp.concatenate([gate_r_w[j], gate_i_w[j]], axis=-1).astype(BF16)
            args = (nw, w_in_odd[j].astype(BF16), conv_w[j], row2(conv_b[j]), gw, row2(gate_r_b[j]),
                    row2(gate_i_b[j]), row2(lru_lambda[j]), w_out_odd[j].astype(BF16))
            tail = (proj, pnw, pgw)
            fnw = row2(final_norm_w)
            o, lh, lc = _odd_layer(hp.reshape(bp, lp, d), *args, pp.reshape(bp, lp, -1), *tail,
                                   jnp.zeros((bp, 1, lru_w), F32), jnp.zeros((bp, CONV_W - 1, lru_w), F32),
                                   fnw, tl=256, final_norm=final)
            hp = o.reshape(rows_p, d)
            outs["lh_p"].append(lh.reshape(bp, lru_w))
            outs["lc_p"].append(lc)
            o, lh, lc = _odd_layer(hs.reshape(bs, ls, d), *args, ps.reshape(bs, ls, -1), *tail,
                                   state_lru_h[j].reshape(bs, 1, lru_w), state_lru_conv[j],
                                   fnw, tl=ls, final_norm=final)
            hs = o.reshape(rows_s, d)
            outs["lh_s"].append(lh.reshape(bs, lru_w))
            outs["lc_s"].append(lc)
            if final:
                y_p, y_s = hp.reshape(bp, lp, d), hs.reshape(bs, ls, d)

    st = lambda k: jnp.stack(outs[k])
    return (y_p, y_s, st("fk_p"), st("fv_p"), st("fl_p"), st("rs_p"), st("lh_p"), st("lc_p"),
            st("fk_s"), st("fv_s"), st("fl_s"), st("rs_s"), st("lh_s"), st("lc_s"))
```

```python
import functools
import math

import jax
import jax.numpy as jnp
from jax import lax
from jax.experimental import pallas as pl
from jax.experimental.pallas import tpu as pltpu

F32 = jnp.float32
BF16 = jnp.bfloat16

HEAD_DIM = 64
HALF_DIM = HEAD_DIM // 2
N_HEADS = 8
HEAD_W = N_HEADS * HEAD_DIM
PAIR_W = 2 * HEAD_DIM
N_PAIRS = N_HEADS // 2
PAGE = 128
PAGES_PER_STEP = 16
RET_CHUNK = 128
ROPE_BASE = 10000.0
LRU_BLOCKS = 8
LRU_BW = 128
CONV_W = 4
LRU_C = 8.0
NORM_EPS = 1e-6
NEG_INF = -1e30
QK_SCALE = HEAD_DIM ** -0.5
VMEM_LIMIT = 56 * 1024 * 1024


def _params(*sem):
    return pltpu.CompilerParams(dimension_semantics=sem, vmem_limit_bytes=VMEM_LIMIT)


def _mm(a, b):
    return jnp.dot(a, b, preferred_element_type=F32)


def _mm_nt(a, b):
    return lax.dot_general(a, b, (((1,), (1,)), ((), ())), preferred_element_type=F32)


def _rms(x, w):
    return (x * lax.rsqrt(jnp.mean(x * x, axis=-1, keepdims=True) + NORM_EPS)) * w


def _const_spec(shape):
    nd = len(shape)
    return pl.BlockSpec(shape, lambda *_: (0,) * nd)


def _swap_halves(x):
    n = x.shape[-1]
    lane = lax.broadcasted_iota(jnp.int32, x.shape, x.ndim - 1)
    return jnp.where(lane % HEAD_DIM < HALF_DIM,
                     pltpu.roll(x, n - HALF_DIM, x.ndim - 1),
                     pltpu.roll(x, HALF_DIM, x.ndim - 1))


def _retention_gate_proj(ub, wr_ref, wg_ref, cos_ref, sin_ref, rq_ref, rk_ref, rv_ref, g_ref):
    zr = _mm(ub, wr_ref[...])
    cos = cos_ref[...]
    sin = sin_ref[...]
    rq = zr[:, :HEAD_W]
    rk = zr[:, HEAD_W:2 * HEAD_W]
    rq_ref[...] = (rq * cos + _swap_halves(rq) * sin).astype(BF16)
    rk_ref[...] = (rk * cos + _swap_halves(rk) * sin) * QK_SCALE
    rv_ref[...] = zr[:, 2 * HEAD_W:].astype(BF16)
    g_ref[...] = _mm(ub, wg_ref[...])


def _even_in_prompt_kernel(h_ref, nw_ref, wqkvt_ref, wk_ref, wlt_ref, wr_ref, wg_ref, bfc_ref, cos_ref, sin_ref,
                           fqt_ref, fkt_ref, fvt_ref, fkb_ref, fvtb_ref, lft_ref,
                           rq_ref, rk_ref, rv_ref, g_ref):
    ub = _rms(h_ref[...], nw_ref[...]).astype(BF16)
    zt = _mm_nt(wqkvt_ref[...], ub)
    fqt_ref[...] = (zt[:HEAD_W] * QK_SCALE).astype(BF16)
    fkt_ref[...] = zt[HEAD_W:2 * HEAD_W]
    fvt = zt[2 * HEAD_W:]
    fvt_ref[...] = fvt
    fvtb_ref[...] = fvt.astype(BF16)
    fkb_ref[...] = _mm(ub, wk_ref[...]).astype(BF16)
    lft_ref[...] = jax.nn.log_sigmoid(_mm_nt(wlt_ref[...], ub) + bfc_ref[...])[:N_HEADS]
    _retention_gate_proj(ub, wr_ref, wg_ref, cos_ref, sin_ref, rq_ref, rk_ref, rv_ref, g_ref)


def _even_in_prompt(h, nw, wqkvt, wk, wlt, wr, wg, bfc, cos, sin, *, tm):
    b, l, d = h.shape
    tok = lambda w: pl.BlockSpec((None, tm, w), lambda bi, i: (bi, i, 0))
    feat = lambda r: pl.BlockSpec((None, r, tm), lambda bi, i: (bi, 0, i))
    pos = pl.BlockSpec((tm, HEAD_W), lambda bi, i: (i, 0))
    sd = jax.ShapeDtypeStruct
    consts = [nw, wqkvt, wk, wlt, wr, wg, bfc]
    return pl.pallas_call(
        _even_in_prompt_kernel,
        grid=(b, l // tm),
        in_specs=[tok(d)] + [_const_spec(c.shape) for c in consts] + [pos, pos],
        out_specs=[feat(HEAD_W), feat(HEAD_W), feat(HEAD_W), tok(HEAD_W), feat(HEAD_W), feat(N_HEADS),
                   tok(HEAD_W), tok(HEAD_W), tok(HEAD_W), tok(2 * HEAD_W)],
        out_shape=[sd((b, HEAD_W, l), BF16), sd((b, HEAD_W, l), F32), sd((b, HEAD_W, l), F32),
                   sd((b, l, HEAD_W), BF16), sd((b, HEAD_W, l), BF16), sd((b, N_HEADS, l), F32),
                   sd((b, l, HEAD_W), BF16), sd((b, l, HEAD_W), F32), sd((b, l, HEAD_W), BF16),
                   sd((b, l, 2 * HEAD_W), F32)],
        compiler_params=_params("parallel", "parallel"),
        name="even_in_prompt",
    )(h, *consts, cos, sin)


def _even_in_sample_kernel(h_ref, nw_ref, wf_ref, wl_ref, wr_ref, wg_ref, bf_ref, cos_ref, sin_ref,
                           fq_ref, fk_ref, fv_ref, lf_ref, lft_ref, rq_ref, rk_ref, rv_ref, g_ref):
    ub = _rms(h_ref[...], nw_ref[...]).astype(BF16)
    zf = _mm(ub, wf_ref[...])
    fq_ref[...] = (zf[:, :HEAD_W] * QK_SCALE).astype(BF16)
    fk_ref[...] = zf[:, HEAD_W:2 * HEAD_W]
    fv_ref[...] = zf[:, 2 * HEAD_W:]
    logf = jax.nn.log_sigmoid(_mm(ub, wl_ref[...]) + bf_ref[...])
    lf_ref[...] = logf[:, :N_HEADS]
    lft_ref[...] = logf.T[:N_HEADS]
    _retention_gate_proj(ub, wr_ref, wg_ref, cos_ref, sin_ref, rq_ref, rk_ref, rv_ref, g_ref)


def _even_in_sample(h, nw, wf, wl, wr, wg, bf, cos, sin):
    rows = h.shape[0]
    sd = jax.ShapeDtypeStruct
    ins = [h, nw, wf, wl, wr, wg, bf, cos, sin]
    outs = [sd((rows, HEAD_W), BF16), sd((rows, HEAD_W), F32), sd((rows, HEAD_W), F32),
            sd((rows, N_HEADS), F32), sd((N_HEADS, rows), F32),
            sd((rows, HEAD_W), BF16), sd((rows, HEAD_W), F32), sd((rows, HEAD_W), BF16),
            sd((rows, 2 * HEAD_W), F32)]
    return pl.pallas_call(
        _even_in_sample_kernel,
        grid=(1,),
        in_specs=[_const_spec(a.shape) for a in ins],
        out_specs=[_const_spec(o.shape) for o in outs],
        out_shape=outs,
        compiler_params=_params("arbitrary"),
        name="even_in_sample",
    )(*ins)


def _seg_cumsum_kernel(x_ref, o_ref, *, seg):
    x = x_ref[...]
    pos = lax.broadcasted_iota(jnp.int32, x.shape, 1) % seg
    s = 1
    while s < seg:
        x = x + jnp.where(pos >= s, pltpu.roll(x, s, 1), 0.0)
        s *= 2
    o_ref[...] = x


def _seg_cumsum(x, *, seg):
    r, n = x.shape
    return pl.pallas_call(
        functools.partial(_seg_cumsum_kernel, seg=seg),
        grid=(r // 8,),
        in_specs=[pl.BlockSpec((8, n), lambda i: (i, 0))],
        out_specs=pl.BlockSpec((8, n), lambda i: (i, 0)),
        out_shape=jax.ShapeDtypeStruct((r, n), F32),
        compiler_params=_params("parallel"),
        name="seg_cumsum",
    )(x)


def _pair_masks(shape):
    lane = lax.broadcasted_iota(jnp.int32, shape, len(shape) - 1)
    return lane < HEAD_DIM


def _fox_prompt_kernel(qt_ref, k_ref, vt_ref, cr_ref, cc_ref, o_ref, ckey_scr, acc_scr, *, tq, tk):
    qi = pl.program_id(2)
    seq = k_ref.shape[0]

    @pl.when(qi == 0)
    def _():
        cc = cc_ref[...]
        for h in range(2):
            ckey_scr[h] = jnp.broadcast_to(cc[:, h:h + 1], (seq, tq))

    qt = qt_ref[...]
    row = lax.broadcasted_iota(jnp.int32, qt.shape, 0)
    zero = jnp.zeros_like(qt)
    w_heads = (jnp.where(row < HEAD_DIM, qt, zero), jnp.where(row < HEAD_DIM, zero, qt))
    cr = cr_ref[...]
    c_query = (cr[0:1, :], cr[1:2, :])
    key_l = lax.broadcasted_iota(jnp.int32, (tk, tq), 0)
    qry_l = lax.broadcasted_iota(jnp.int32, (tk, tq), 1)
    acc_scr[...] = jnp.zeros_like(acc_scr)

    def step(start, carry, mask_offset):
        kc = k_ref[pl.ds(start, tk), :]
        new = []
        for h in range(2):
            m, l = carry[h]
            s = (_mm(kc, w_heads[h]) + c_query[h]) - ckey_scr[h, pl.ds(start, tk), :]
            if mask_offset is not None:
                s = jnp.where(key_l + mask_offset <= qry_l, s, NEG_INF)
            m_new = jnp.maximum(m, jnp.max(s, axis=0, keepdims=True))
            alpha = jnp.exp(m - m_new)
            p = jnp.exp(s - m_new)
            l = alpha * l + jnp.sum(p, axis=0, keepdims=True)
            rows = slice(h * HEAD_DIM, (h + 1) * HEAD_DIM)
            pv = _mm(vt_ref[rows, pl.ds(start, tk)], p.astype(BF16))
            acc_scr[rows, :] = alpha * acc_scr[rows, :] + pv
            new.append((m_new, l))
        return tuple(new)

    init_h = (jnp.full((1, tq), NEG_INF, F32), jnp.zeros((1, tq), F32))
    n_sub = tq // tk

    def unmasked(jj, carry):
        for g in range(n_sub):
            carry = step(pl.multiple_of((jj * n_sub + g) * tk, tk), carry, None)
        return carry

    carry = lax.fori_loop(0, qi, unmasked, (init_h, init_h))
    base = qi * tq
    for d in range(n_sub):
        carry = step(pl.multiple_of(base + d * tk, tk), carry, d * tk)
    (_, l0), (_, l1) = carry
    o_ref[...] = (acc_scr[...] / jnp.where(row < HEAD_DIM, l0, l1)).T


def _fox_prompt(qt, kb, vtb, cum_rows, cum_cols, *, tq, tk):
    b, l, _ = kb.shape
    return pl.pallas_call(
        functools.partial(_fox_prompt_kernel, tq=tq, tk=tk),
        grid=(b, N_PAIRS, l // tq),
        in_specs=[pl.BlockSpec((None, PAIR_W, tq), lambda bi, hp, qi: (bi, hp, qi)),
                  pl.BlockSpec((None, l, PAIR_W), lambda bi, hp, qi: (bi, 0, hp)),
                  pl.BlockSpec((None, PAIR_W, l), lambda bi, hp, qi: (bi, hp, 0)),
                  pl.BlockSpec((None, None, 2, tq), lambda bi, hp, qi: (bi, hp, 0, qi)),
                  pl.BlockSpec((None, None, l, 2), lambda bi, hp, qi: (bi, hp, 0, 0))],
        out_specs=pl.BlockSpec((None, tq, PAIR_W), lambda bi, hp, qi: (bi, qi, hp)),
        out_shape=jax.ShapeDtypeStruct((b, l, HEAD_W), F32),
        scratch_shapes=[pltpu.VMEM((2, l, tq), F32), pltpu.VMEM((PAIR_W, tq), F32)],
        compiler_params=_params("parallel", "parallel", "arbitrary"),
        name="fox_prompt",
    )(qt, kb, vtb, cum_rows, cum_cols)


def _ret_tables(lg_ref, length):
    n = RET_CHUNK
    lg_a = lg_ref[0:1, :]
    lg_b = lg_ref[1:2, :]
    lg_m = lg_ref[2:3, :]
    row = lax.broadcasted_iota(jnp.int32, (n, n), 0)
    col = lax.broadcasted_iota(jnp.int32, (n, n), 1)
    diff = row - col
    lower = diff >= 0
    dpos = jnp.where(lower, diff, 0).astype(F32)
    decay = (jnp.where(lower, jnp.exp(dpos * lg_a), 0.0), jnp.where(lower, jnp.exp(dpos * lg_b), 0.0))
    rowf = row.astype(F32)
    q_dec = jnp.exp((rowf + 1.0) * lg_m)
    k_dec = jnp.exp((length - 1.0 - rowf) * lg_m)
    s_dec = jnp.exp(float(length) * jnp.where(row < HEAD_DIM, lg_a, lg_b))
    same_head = (row < HEAD_DIM) == (col < HEAD_DIM)
    return decay, q_dec, k_dec, s_dec, same_head


def _ret_chunk(q, k, v, state, tables):
    decay, q_dec, k_dec, s_dec, same_head = tables
    first = _pair_masks(q.shape)
    zero = jnp.zeros_like(q)
    kb = k.astype(BF16)
    o_heads = []
    for h, qh in enumerate((jnp.where(first, q, zero), jnp.where(first, zero, q))):
        scores = _mm_nt(qh, kb) * decay[h]
        o_heads.append(_mm(scores.astype(BF16), v))
    o = jnp.where(first, o_heads[0], o_heads[1])
    o = o + _mm(q, state.astype(BF16)) * q_dec
    kd_t = (k * k_dec).T.astype(BF16)
    state = state * s_dec + jnp.where(same_head, _mm(kd_t, v), 0.0)
    return o, state


def _group_norm(o, gn_w):
    first = _pair_masks(o.shape)
    inv = 1.0 / HEAD_DIM
    s0 = jnp.sum(jnp.where(first, o, 0.0), axis=-1, keepdims=True)
    s1 = jnp.sum(jnp.where(first, 0.0, o), axis=-1, keepdims=True)
    d = o - jnp.where(first, s0, s1) * inv
    dd = d * d
    v0 = jnp.sum(jnp.where(first, dd, 0.0), axis=-1, keepdims=True)
    v1 = jnp.sum(jnp.where(first, 0.0, dd), axis=-1, keepdims=True)
    var = jnp.where(first, v0, v1) * inv
    return (d * lax.rsqrt(var + NORM_EPS)) * gn_w


def _ret_prompt_kernel(q_ref, k_ref, v_ref, lg_ref, gn_ref, o_ref, s_ref, *, n_chunks):
    tables = _ret_tables(lg_ref, RET_CHUNK)
    gn_w = gn_ref[...]

    def body(c, state):
        rows = pl.ds(pl.multiple_of(c * RET_CHUNK, RET_CHUNK), RET_CHUNK)
        o, state = _ret_chunk(q_ref[rows, :], k_ref[rows, :], v_ref[rows, :], state, tables)
        o_ref[rows, :] = _group_norm(o, gn_w)
        return state

    state = lax.fori_loop(0, n_chunks, body, jnp.zeros((PAIR_W, PAIR_W), F32), unroll=8)
    s_ref[0] = state[:HEAD_DIM, :HEAD_DIM]
    s_ref[1] = state[HEAD_DIM:, HEAD_DIM:]


def _ret_prompt(rq, rk, rv, lg_tab, gn_w):
    b, l, _ = rq.shape
    seq = pl.BlockSpec((None, l, PAIR_W), lambda bi, hp: (bi, 0, hp))
    return pl.pallas_call(
        functools.partial(_ret_prompt_kernel, n_chunks=l // RET_CHUNK),
        grid=(b, N_PAIRS),
        in_specs=[seq, seq, seq,
                  pl.BlockSpec((None, 8, PAIR_W), lambda bi, hp: (hp, 0, 0)),
                  pl.BlockSpec((1, PAIR_W), lambda bi, hp: (0, hp))],
        out_specs=[seq, pl.BlockSpec((None, 2, HEAD_DIM, HEAD_DIM), lambda bi, hp: (bi, hp, 0, 0))],
        out_shape=[jax.ShapeDtypeStruct((b, l, HEAD_W), F32),
                   jax.ShapeDtypeStruct((b, N_HEADS, HEAD_DIM, HEAD_DIM), F32)],
        compiler_params=_params("parallel", "parallel"),
        name="ret_prompt",
    )(rq, rk, rv, lg_tab, gn_w)


def _ret_sample_kernel(q_ref, k_ref, v_ref, s0_ref, lg_ref, gn_ref, o_ref, s_ref, *, length):
    pad = RET_CHUNK - length
    zrow_f = jnp.zeros((pad, PAIR_W), F32)
    zblk = jnp.zeros((HEAD_DIM, HEAD_DIM), F32)
    for hp in range(N_PAIRS):
        lanes = slice(hp * PAIR_W, (hp + 1) * PAIR_W)
        tables = _ret_tables(lg_ref.at[hp], length)
        q = jnp.concatenate([q_ref[:, lanes].astype(F32), zrow_f], axis=0).astype(BF16)
        k = jnp.concatenate([k_ref[:, lanes], zrow_f], axis=0)
        v = jnp.concatenate([v_ref[:, lanes].astype(F32), zrow_f], axis=0).astype(BF16)
        state = jnp.concatenate(
            [jnp.concatenate([s0_ref[2 * hp], zblk], axis=1),
             jnp.concatenate([zblk, s0_ref[2 * hp + 1]], axis=1)], axis=0)
        o, state = _ret_chunk(q, k, v, state, tables)
        o_ref[:, lanes] = _group_norm(o[:length], gn_ref[:, lanes])
        s_ref[2 * hp] = state[:HEAD_DIM, :HEAD_DIM]
        s_ref[2 * hp + 1] = state[HEAD_DIM:, HEAD_DIM:]


def _ret_sample(rq, rk, rv, state, lg_tab, gn_w, *, length):
    rows = rq.shape[0]
    b = rows // length
    tok = pl.BlockSpec((length, HEAD_W), lambda bi: (bi, 0))
    st = pl.BlockSpec((None, N_HEADS, HEAD_DIM, HEAD_DIM), lambda bi: (bi, 0, 0, 0))
    return pl.pallas_call(
        functools.partial(_ret_sample_kernel, length=length),
        grid=(b,),
        in_specs=[tok, tok, tok, st, _const_spec(lg_tab.shape), _const_spec(gn_w.shape)],
        out_specs=[tok, st],
        out_shape=[jax.ShapeDtypeStruct((rows, HEAD_W), F32),
                   jax.ShapeDtypeStruct(state.shape, F32)],
        compiler_params=_params("parallel"),
        name="ret_sample",
    )(rq, rk, rv, state, lg_tab, gn_w)


def _ple(h1, p, proj, pnw, pgw):
    e = _rms(_mm(p.astype(BF16), proj), pnw)
    return h1 + jax.nn.sigmoid(_mm(h1.astype(BF16), pgw)) * e


def _silu(g):
    return g * jax.nn.sigmoid(g)


def _even_out_kernel(h_ref, fo_ref, ro_ref, g_ref, wo_ref, p_ref, proj_ref, pnw_ref, pgw_ref, o_ref):
    sg = _silu(g_ref[...])
    cat_f = (fo_ref[...] * sg[:, :HEAD_W]).astype(BF16)
    cat_r = (ro_ref[...] * sg[:, HEAD_W:]).astype(BF16)
    mix = _mm(cat_f, wo_ref[:HEAD_W, :]) + _mm(cat_r, wo_ref[HEAD_W:, :])
    o_ref[...] = _ple(h_ref[...] + mix, p_ref[...], proj_ref[...], pnw_ref[...], pgw_ref[...])


def _even_out(h, fo, ro, g, wo, p, proj, pnw, pgw, *, tm):
    rows, d = h.shape
    row = lambda w: pl.BlockSpec((tm, w), lambda i: (i, 0))
    return pl.pallas_call(
        _even_out_kernel,
        grid=(rows // tm,),
        in_specs=[row(d), row(HEAD_W), row(HEAD_W), row(2 * HEAD_W), _const_spec(wo.shape),
                  row(p.shape[1]), _const_spec(proj.shape), _const_spec(pnw.shape), _const_spec(pgw.shape)],
        out_specs=row(d),
        out_shape=jax.ShapeDtypeStruct((rows, d), F32),
        compiler_params=_params("parallel"),
        name="even_out",
    )(h, fo, ro, g, wo, p, proj, pnw, pgw)


def _odd_kernel(h_ref, nw_ref, win_ref, cw_ref, cb_ref, gw_ref, grb_ref, gib_ref, lam_ref, wout_ref,
                p_ref, proj_ref, pnw_ref, pgw_ref, h0_ref, cbuf_ref, fnw_ref,
                o_ref, lruh_ref, lruc_ref,
                ext_scr, carry_scr, a_scr, b_scr, sg_scr, *, tl, final_norm):
    w = h_ref.shape[-1]
    halo = 8

    @pl.when(pl.program_id(1) == 0)
    def _():
        carry_scr[...] = jnp.broadcast_to(h0_ref[...], carry_scr.shape)
        ext_scr[0:halo, :] = jnp.zeros((halo, w), F32)
        ext_scr[halo - (CONV_W - 1):halo, :] = cbuf_ref[...]

    x = h_ref[...]
    z = _mm(_rms(x, nw_ref[...]).astype(BF16), win_ref[...])
    xb = z[:, :w]
    sg_scr[...] = _silu(z[:, w:])
    ext_scr[halo:halo + tl, :] = xb
    xc = xb * cw_ref[CONV_W - 1:CONV_W, :] + cb_ref[...]
    for kk in range(1, CONV_W):
        xc = xc + ext_scr[halo - kk:halo - kk + tl, :] * cw_ref[CONV_W - 1 - kk:CONV_W - kk, :]
    lruc_ref[...] = ext_scr[halo + tl - (CONV_W - 1):halo + tl, :]
    ext_scr[0:halo, :] = xb[tl - halo:, :]

    sp = jax.nn.softplus(-lam_ref[...])
    for n in range(LRU_BLOCKS):
        lanes = slice(n * LRU_BW, (n + 1) * LRU_BW)
        xn = xc[:, lanes]
        zz = _mm(xn.astype(BF16), gw_ref[n])
        r = jax.nn.sigmoid(zz[:, :LRU_BW] + grb_ref[:, lanes])
        gi = jax.nn.sigmoid(zz[:, LRU_BW:] + gib_ref[:, lanes])
        log_a = (-LRU_C * r) * sp[:, lanes]
        a = jnp.exp(log_a)
        one_minus = -jnp.tanh(log_a) * (a * a + 1.0)
        a_scr[:, lanes] = a
        b_scr[:, lanes] = (jnp.sqrt(one_minus) * gi) * xn

    sub = lax.broadcasted_iota(jnp.int32, (8, w), 0)

    def group(gidx, carry):
        rows = pl.ds(pl.multiple_of(gidx * 8, 8), 8)
        a = a_scr[rows, :]
        b = b_scr[rows, :]
        for s in (1, 2, 4):
            keep = sub >= s
            b = a * jnp.where(keep, pltpu.roll(b, s, 0), 0.0) + b
            a = a * jnp.where(keep, pltpu.roll(a, s, 0), 1.0)
        hs = a * carry + b
        b_scr[rows, :] = hs
        return jnp.broadcast_to(hs[7:8, :], (8, w))

    carry = lax.fori_loop(0, tl // 8, group, carry_scr[...], unroll=min(4, tl // 8))
    carry_scr[...] = carry
    lruh_ref[...] = carry[0:1, :]

    mix = _mm((b_scr[...] * sg_scr[...]).astype(BF16), wout_ref[...])
    h2 = _ple(x + mix, p_ref[...], proj_ref[...], pnw_ref[...], pgw_ref[...])
    if final_norm:
        h2 = _rms(h2, fnw_ref[...])
    o_ref[...] = h2


def _odd_layer(h, nw, win, cw, cb, gw, grb, gib, lam, wout, p, proj, pnw, pgw, h0, cbuf, fnw,
               *, tl, final_norm):
    b, l, d = h.shape
    w = wout.shape[0]
    seq = lambda width: pl.BlockSpec((None, tl, width), lambda bi, li: (bi, li, 0))
    per_b = lambda r: pl.BlockSpec((None, r, w), lambda bi, li: (bi, 0, 0))
    consts = [nw, win, cw, cb, gw, grb, gib, lam, wout]
    tail = [proj, pnw, pgw]
    return pl.pallas_call(
        functools.partial(_odd_kernel, tl=tl, final_norm=final_norm),
        grid=(b, l // tl),
        in_specs=[seq(d)] + [_const_spec(c.shape) for c in consts] + [seq(p.shape[-1])]
                 + [_const_spec(c.shape) for c in tail] + [per_b(1), per_b(CONV_W - 1), _const_spec(fnw.shape)],
        out_specs=[seq(d), per_b(1), per_b(CONV_W - 1)],
        out_shape=[jax.ShapeDtypeStruct((b, l, d), F32),
                   jax.ShapeDtypeStruct((b, 1, w), F32),
                   jax.ShapeDtypeStruct((b, CONV_W - 1, w), F32)],
        scratch_shapes=[pltpu.VMEM((tl + 8, w), F32), pltpu.VMEM((8, w), F32),
                        pltpu.VMEM((tl, w), F32), pltpu.VMEM((tl, w), F32), pltpu.VMEM((tl, w), F32)],
        compiler_params=_params("parallel", "arbitrary"),
        name="odd_layer",
    )(h, *consts, p, *tail, h0, cbuf, fnw)


def _fox_sample_kernel(tbl_ref, q_ref, kn_ref, vn_ref, cnc_ref, cnr_ref, tri_ref, *refs, t_new):
    k_pages = refs[:PAGES_PER_STEP]
    v_pages = refs[PAGES_PER_STEP:2 * PAGES_PER_STEP]
    f_pages = refs[2 * PAGES_PER_STEP:3 * PAGES_PER_STEP]
    o_ref = refs[3 * PAGES_PER_STEP]
    qbd_scr, m_scr, l_scr, acc_scr, rel_scr = refs[3 * PAGES_PER_STEP + 1:]
    del tbl_ref
    c = pl.program_id(1)
    cn_col = cnc_ref[...]

    @pl.when(c == 0)
    def _():
        rel_scr[...] = jnp.zeros_like(rel_scr)
        q = q_ref[...].astype(F32)
        lane_head = lax.broadcasted_iota(jnp.int32, q.shape, 1) // HEAD_DIM
        zero = jnp.zeros_like(q)
        for h in range(N_HEADS):
            qbd_scr[h * t_new:(h + 1) * t_new, :] = jnp.where(lane_head == h, q, zero)
        m_scr[...] = jnp.full(m_scr.shape, NEG_INF, F32)
        l_scr[...] = jnp.zeros_like(l_scr)
        acc_scr[...] = jnp.zeros_like(acc_scr)

    qbd = qbd_scr[...].astype(BF16)

    def online(s, pv_fn):
        m = m_scr[...]
        m_new = jnp.maximum(m, jnp.max(s, axis=-1, keepdims=True))
        alpha = jnp.exp(m - m_new)
        p = jnp.exp(s - m_new)
        l_scr[...] = alpha * l_scr[...] + jnp.sum(p, axis=-1, keepdims=True)
        acc_scr[...] = alpha * acc_scr[...] + pv_fn(p.astype(BF16))
        m_scr[...] = m_new

    xs = jnp.concatenate([f_pages[i][...] for i in range(PAGES_PER_STEP)], axis=0)
    hi = xs.astype(BF16)
    rest = xs - hi.astype(F32)
    mid = rest.astype(BF16)
    lo = (rest - mid.astype(F32)).astype(BF16)
    tri = tri_ref[...]
    sums = (_mm(hi, tri) + _mm(mid, tri)) + _mm(lo, tri)
    after = rel_scr[...]
    bias = [None] * PAGES_PER_STEP
    for i in reversed(range(PAGES_PER_STEP)):
        rows = slice(i * N_HEADS, (i + 1) * N_HEADS)
        rel = sums[rows, :PAGE] + after
        after = after + sums[rows, PAGE:]
        bias[i] = jnp.concatenate(
            [jnp.broadcast_to(rel[h:h + 1, :], (t_new, PAGE)) for h in range(N_HEADS)], axis=0)
    rel_scr[...] = after

    kt = jnp.concatenate([k_pages[i][...].astype(BF16) for i in range(PAGES_PER_STEP)], axis=1)
    vt = jnp.concatenate([v_pages[i][...].astype(BF16) for i in range(PAGES_PER_STEP)], axis=1)
    s = (_mm(qbd, kt) + cn_col) + jnp.concatenate(bias, axis=1)
    online(s, lambda p: _mm_nt(p, vt))

    @pl.when(c == pl.num_programs(1) - 1)
    def _():
        zpad = jnp.zeros((PAGE - t_new, HEAD_W), F32)
        kn = jnp.concatenate([kn_ref[...], zpad], axis=0).astype(BF16)
        vn = jnp.concatenate([vn_ref[...], zpad], axis=0).astype(BF16)
        s = (_mm_nt(qbd, kn) + cn_col) - cnr_ref[...]
        t_query = lax.broadcasted_iota(jnp.int32, s.shape, 0) % t_new
        t_key = lax.broadcasted_iota(jnp.int32, s.shape, 1)
        s = jnp.where(t_query >= t_key, s, NEG_INF)
        online(s, lambda p: _mm(p, vn))
        o = acc_scr[...] / l_scr[...]
        lane_head = lax.broadcasted_iota(jnp.int32, (t_new, HEAD_W), 1) // HEAD_DIM
        out = jnp.zeros((t_new, HEAD_W), F32)
        for h in range(N_HEADS):
            out = jnp.where(lane_head == h, o[h * t_new:(h + 1) * t_new, :], out)
        o_ref[...] = out


def _fox_sample(q, k_new, v_new, cn_col, cn_rows, cache_kt, cache_vt, cache_ft, table, layer, *, t_new):
    rows = q.shape[0]
    b, n_pages = table.shape
    n_steps = n_pages // PAGES_PER_STEP
    n_rows = N_HEADS * t_new
    tok = pl.BlockSpec((t_new, HEAD_W), lambda bi, c, tbl: (bi, 0))

    def page_spec(i, height):
        return pl.BlockSpec(
            (None, None, height, PAGE),
            lambda bi, c, tbl: (layer, tbl[bi, (n_steps - 1 - c) * PAGES_PER_STEP + i], 0, 0))

    kv_pages = [page_spec(i, HEAD_W) for i in range(PAGES_PER_STEP)]
    f_pages = [page_spec(i, N_HEADS) for i in range(PAGES_PER_STEP)]
    pos = jnp.arange(PAGE)
    later = (pos[:, None] > pos[None, :]).astype(BF16)
    tri = jnp.concatenate([later, jnp.ones((PAGE, PAGE), BF16)], axis=1)
    return pl.pallas_call(
        functools.partial(_fox_sample_kernel, t_new=t_new),
        grid_spec=pltpu.PrefetchScalarGridSpec(
            num_scalar_prefetch=1,
            grid=(b, n_steps),
            in_specs=[tok, tok, tok,
                      pl.BlockSpec((None, n_rows, 1), lambda bi, c, tbl: (bi, 0, 0)),
                      pl.BlockSpec((None, n_rows, PAGE), lambda bi, c, tbl: (bi, 0, 0)),
                      pl.BlockSpec(tri.shape, lambda bi, c, tbl: (0, 0))]
                     + kv_pages + kv_pages + f_pages,
            out_specs=tok,
            scratch_shapes=[pltpu.VMEM((n_rows, HEAD_W), F32), pltpu.VMEM((n_rows, 1), F32),
                            pltpu.VMEM((n_rows, 1), F32), pltpu.VMEM((n_rows, HEAD_W), F32),
                            pltpu.VMEM((N_HEADS, PAGE), F32)]),
        out_shape=jax.ShapeDtypeStruct((rows, HEAD_W), F32),
        compiler_params=_params("parallel", "arbitrary"),
        name="fox_sample",
    )(table, q, k_new, v_new, cn_col, cn_rows, tri,
      *([cache_kt] * PAGES_PER_STEP), *([cache_vt] * PAGES_PER_STEP), *([cache_ft] * PAGES_PER_STEP))


def _rope_tables(pos):
    inv = ROPE_BASE ** (-jnp.arange(HALF_DIM, dtype=F32) / HALF_DIM)
    ang = pos[:, None] * inv[None, :]
    cos = jnp.cos(ang)
    sin = jnp.sin(ang)
    cos_head = jnp.concatenate([cos, cos], axis=-1)
    sin_head = jnp.concatenate([-sin, sin], axis=-1)
    return jnp.tile(cos_head, (1, N_HEADS)), jnp.tile(sin_head, (1, N_HEADS))


def _log_gamma_table(log_gamma):
    pairs = log_gamma.reshape(N_PAIRS, 2)
    ones = jnp.ones((N_PAIRS, 1, PAIR_W), F32)
    first = pairs[:, 0][:, None, None] * ones
    second = pairs[:, 1][:, None, None] * ones
    mix = jnp.repeat(pairs, HEAD_DIM, axis=1)[:, None, :]
    return jnp.concatenate([first, second, mix, jnp.zeros((N_PAIRS, 5, PAIR_W), F32)], axis=1)


def kernel(x_prompt, x_sample, cache_fox_k, cache_fox_v, cache_fox_logf, page_table, state_ret, state_lru_h, state_lru_conv, p_prompt, p_sample, norm_w, w_in_even, b_forget, ret_gn_w, w_out_even, w_in_odd, conv_w, conv_b, gate_r_w, gate_r_b, gate_i_w, gate_i_b, lru_lambda, w_out_odd, ple_proj, ple_norm_w, ple_gate_w, final_norm_w):
    bp, lp, d = x_prompt.shape
    bs, ls, _ = x_sample.shape
    depth = norm_w.shape[0]
    n_pool = cache_fox_k.shape[1]
    n_pages = page_table.shape[1]
    past_len = n_pages * PAGE
    lru_w = w_out_odd.shape[1]
    rows_p, rows_s = bp * lp, bs * ls
    tm_p = 256

    log_gamma = jnp.log1p(-jnp.exp2(-5.0 - jnp.arange(N_HEADS, dtype=F32)))
    lg_tab = _log_gamma_table(log_gamma)
    cos_p, sin_p = _rope_tables(jnp.arange(lp, dtype=F32))
    cos_s, sin_s = _rope_tables(past_len + jnp.arange(ls, dtype=F32))
    cos_s, sin_s = jnp.tile(cos_s, (bs, 1)), jnp.tile(sin_s, (bs, 1))

    cache_kt = cache_fox_k.transpose(0, 1, 3, 4, 2).reshape(-1, n_pool, HEAD_W, PAGE)
    cache_vt = cache_fox_v.transpose(0, 1, 3, 4, 2).reshape(-1, n_pool, HEAD_W, PAGE)
    cache_ft = jnp.swapaxes(cache_fox_logf, 2, 3)

    row2 = lambda v: v.reshape(1, -1)
    hp = x_prompt.reshape(rows_p, d)
    hs = x_sample.reshape(rows_s, d)
    outs = {k: [] for k in ("fk_p", "fv_p", "fl_p", "rs_p", "lh_p", "lc_p",
                            "fk_s", "fv_s", "fl_s", "rs_s", "lh_s", "lc_s")}
    y_p = y_s = None

    for i in range(depth):
        j = i // 2
        proj = ple_proj[i].astype(BF16)
        pgw = ple_gate_w[i].astype(BF16)
        pnw = row2(ple_norm_w[i])
        nw = row2(norm_w[i])
        pp = p_prompt[i].reshape(rows_p, -1)
        ps = p_sample[i].reshape(rows_s, -1)
        if i % 2 == 0:
            w = w_in_even[j]
            c0 = 3 * HEAD_W
            wf = w[:, :c0].astype(BF16)
            wl = jnp.pad(w[:, c0:c0 + N_HEADS], ((0, 0), (0, PAGE - N_HEADS))).astype(BF16)
            wr = w[:, c0 + N_HEADS:2 * c0 + N_HEADS].astype(BF16)
            wg = w[:, 2 * c0 + N_HEADS:].astype(BF16)
            bf = jnp.pad(b_forget[j], (0, PAGE - N_HEADS)).reshape(1, PAGE)
            wo = w_out_even[j].astype(BF16)
            gn = row2(ret_gn_w[j])
            wt = jnp.swapaxes(w, 0, 1)
            wqkvt = wt[:c0].astype(BF16)
            wlt = jnp.pad(wt[c0:c0 + N_HEADS], ((0, N_HEADS), (0, 0))).astype(BF16)
            bfc = jnp.pad(b_forget[j], (0, N_HEADS)).reshape(2 * N_HEADS, 1)
            wk = w[:, HEAD_W:2 * HEAD_W].astype(BF16)

            fqt, fkt, fvt, fkb, fvtb, lft, rq, rk, rv, g = _even_in_prompt(
                hp.reshape(bp, lp, d), nw, wqkvt, wk, wlt, wr, wg, bfc, cos_p, sin_p, tm=tm_p)
            cum_rows = _seg_cumsum(lft.reshape(bp * N_HEADS, lp), seg=lp).reshape(bp, N_PAIRS, 2, lp)
            cum_cols = jnp.swapaxes(cum_rows, 2, 3)
            fo = _fox_prompt(fqt, fkb, fvtb, cum_rows, cum_cols, tq=512, tk=512)
            ro, s_p = _ret_prompt(rq, rk, rv, lg_tab, gn)
            hp = _even_out(hp, fo.reshape(rows_p, HEAD_W), ro.reshape(rows_p, HEAD_W),
                           g.reshape(rows_p, 2 * HEAD_W), wo, pp, proj, pnw, pgw, tm=tm_p)
            to_blhd = lambda t: t.reshape(bp, N_HEADS, HEAD_DIM, lp).transpose(0, 3, 1, 2)
            outs["fk_p"].append(to_blhd(fkt))
            outs["fv_p"].append(to_blhd(fvt))
            outs["fl_p"].append(jnp.swapaxes(lft, 1, 2))
            outs["rs_p"].append(s_p)

            fq, fk, fv, lf, lft, rq, rk, rv, g = _even_in_sample(hs, nw, wf, wl, wr, wg, bf, cos_s, sin_s)
            cn = _seg_cumsum(lft, seg=ls).reshape(N_HEADS, bs, ls).transpose(1, 0, 2)
            cn_col = cn.reshape(bs, N_HEADS * ls, 1)
            cn_rows = jnp.pad(jnp.repeat(cn, ls, axis=1), ((0, 0), (0, 0), (0, PAGE - ls)))
            fo = _fox_sample(fq, fk, fv, cn_col, cn_rows, cache_kt, cache_vt, cache_ft, page_table, j, t_new=ls)
            ro, s_s = _ret_sample(rq, rk, rv, state_ret[j], lg_tab, gn, length=ls)
            hs = _even_out(hs, fo, ro, g, wo, ps, proj, pnw, pgw, tm=rows_s)
            outs["fk_s"].append(fk.reshape(bs, ls, N_HEADS, HEAD_DIM))
            outs["fv_s"].append(fv.reshape(bs, ls, N_HEADS, HEAD_DIM))
            outs["fl_s"].append(lf.reshape(bs, ls, N_HEADS))
            outs["rs_s"].append(s_s)
        else:
            final = i == depth - 1
            gw = jnp.concatenate([gate_r_w[j], gate_i_w[j]], axis=-1).astype(BF16)
            args = (nw, w_in_odd[j].astype(BF16), conv_w[j], row2(conv_b[j]), gw, row2(gate_r_b[j]),
                    row2(gate_i_b[j]), row2(lru_lambda[j]), w_out_odd[j].astype(BF16))
            tail = (proj, pnw, pgw)
            fnw = row2(final_norm_w)
            o, lh, lc = _odd_layer(hp.reshape(bp, lp, d), *args, pp.reshape(bp, lp, -1), *tail,
                                   jnp.zeros((bp, 1, lru_w), F32), jnp.zeros((bp, CONV_W - 1, lru_w), F32),
                                   fnw, tl=256, final_norm=final)
            hp = o.reshape(rows_p, d)
            outs["lh_p"].append(lh.reshape(bp, lru_w))
            outs["lc_p"].append(lc)
            o, lh, lc = _odd_layer(hs.reshape(bs, ls, d), *args, ps.reshape(bs, ls, -1), *tail,
                                   state_lru_h[j].reshape(bs, 1, lru_w), state_lru_conv[j],
                                   fnw, tl=ls, final_norm=final)
            hs = o.reshape(rows_s, d)
            outs["lh_s"].append(lh.reshape(bs, lru_w))
            outs["lc_s"].append(lc)
            if final:
                y_p, y_s = hp.reshape(bp, lp, d), hs.reshape(bs, ls, d)

    st = lambda k: jnp.stack(outs[k])
    return (y_p, y_s, st("fk_p"), st("fv_p"), st("fl_p"), st("rs_p"), st("lh_p"), st("lc_p"),
            st("fk_s"), st("fv_s"), st("fl_s"), st("rs_s"), st("lh_s"), st("lc_s"))
```

```python
import functools
import math

import jax
import jax.numpy as jnp
from jax import lax
from jax.experimental import pallas as pl
from jax.experimental.pallas import tpu as pltpu

F32 = jnp.float32
BF16 = jnp.bfloat16

HEAD_DIM = 64
HALF_DIM = HEAD_DIM // 2
N_HEADS = 8
HEAD_W = N_HEADS * HEAD_DIM
PAIR_W = 2 * HEAD_DIM
N_PAIRS = N_HEADS // 2
PAGE = 128
PAGES_PER_STEP = 16
RET_CHUNK = 128
ROPE_BASE = 10000.0
LRU_BLOCKS = 8
LRU_BW = 128
CONV_W = 4
SEQ_PER_BLOCK = 8
LRU_C = 8.0
NORM_EPS = 1e-6
NEG_INF = -1e30
QK_SCALE = HEAD_DIM ** -0.5
VMEM_LIMIT = 56 * 1024 * 1024


def _params(*sem):
    return pltpu.CompilerParams(dimension_semantics=sem, vmem_limit_bytes=VMEM_LIMIT)


def _mm(a, b):
    return jnp.dot(a, b, preferred_element_type=F32)


def _mm_nt(a, b):
    return lax.dot_general(a, b, (((1,), (1,)), ((), ())), preferred_element_type=F32)


def _rms(x, w):
    return (x * lax.rsqrt(jnp.mean(x * x, axis=-1, keepdims=True) + NORM_EPS)) * w


def _const_spec(shape):
    nd = len(shape)
    return pl.BlockSpec(shape, lambda *_: (0,) * nd)


def _swap_halves(x):
    n = x.shape[-1]
    lane = lax.broadcasted_iota(jnp.int32, x.shape, x.ndim - 1)
    return jnp.where(lane % HEAD_DIM < HALF_DIM,
                     pltpu.roll(x, n - HALF_DIM, x.ndim - 1),
                     pltpu.roll(x, HALF_DIM, x.ndim - 1))


def _retention_gate_proj(ub, wr_ref, wg_ref, cos_ref, sin_ref, rq_ref, rk_ref, rv_ref, g_ref):
    zr = _mm(ub, wr_ref[...])
    cos = cos_ref[...]
    sin = sin_ref[...]
    rq = zr[:, :HEAD_W]
    rk = zr[:, HEAD_W:2 * HEAD_W]
    rq_ref[...] = (rq * cos + _swap_halves(rq) * sin).astype(BF16)
    rk_ref[...] = (rk * cos + _swap_halves(rk) * sin) * QK_SCALE
    rv_ref[...] = zr[:, 2 * HEAD_W:].astype(BF16)
    g_ref[...] = _mm(ub, wg_ref[...])


def _even_in_prompt_kernel(h_ref, nw_ref, wqkvt_ref, wk_ref, wlt_ref, wr_ref, wg_ref, bfc_ref, cos_ref, sin_ref,
                           fqt_ref, fkt_ref, fvt_ref, fkb_ref, fvtb_ref, lft_ref,
                           rq_ref, rk_ref, rv_ref, g_ref):
    ub = _rms(h_ref[...], nw_ref[...]).astype(BF16)
    zt = _mm_nt(wqkvt_ref[...], ub)
    fqt_ref[...] = (zt[:HEAD_W] * QK_SCALE).astype(BF16)
    fkt_ref[...] = zt[HEAD_W:2 * HEAD_W]
    fvt = zt[2 * HEAD_W:]
    fvt_ref[...] = fvt
    fvtb_ref[...] = fvt.astype(BF16)
    fkb_ref[...] = _mm(ub, wk_ref[...]).astype(BF16)
    lft_ref[...] = jax.nn.log_sigmoid(_mm_nt(wlt_ref[...], ub) + bfc_ref[...])[:N_HEADS]
    _retention_gate_proj(ub, wr_ref, wg_ref, cos_ref, sin_ref, rq_ref, rk_ref, rv_ref, g_ref)


def _even_in_prompt(h, nw, wqkvt, wk, wlt, wr, wg, bfc, cos, sin, *, tm):
    b, l, d = h.shape
    tok = lambda w: pl.BlockSpec((None, tm, w), lambda bi, i: (bi, i, 0))
    feat = lambda r: pl.BlockSpec((None, r, tm), lambda bi, i: (bi, 0, i))
    pos = pl.BlockSpec((tm, HEAD_W), lambda bi, i: (i, 0))
    sd = jax.ShapeDtypeStruct
    consts = [nw, wqkvt, wk, wlt, wr, wg, bfc]
    return pl.pallas_call(
        _even_in_prompt_kernel,
        grid=(b, l // tm),
        in_specs=[tok(d)] + [_const_spec(c.shape) for c in consts] + [pos, pos],
        out_specs=[feat(HEAD_W), feat(HEAD_W), feat(HEAD_W), tok(HEAD_W), feat(HEAD_W), feat(N_HEADS),
                   tok(HEAD_W), tok(HEAD_W), tok(HEAD_W), tok(2 * HEAD_W)],
        out_shape=[sd((b, HEAD_W, l), BF16), sd((b, HEAD_W, l), F32), sd((b, HEAD_W, l), F32),
                   sd((b, l, HEAD_W), BF16), sd((b, HEAD_W, l), BF16), sd((b, N_HEADS, l), F32),
                   sd((b, l, HEAD_W), BF16), sd((b, l, HEAD_W), F32), sd((b, l, HEAD_W), BF16),
                   sd((b, l, 2 * HEAD_W), F32)],
        compiler_params=_params("parallel", "parallel"),
        name="even_in_prompt",
    )(h, *consts, cos, sin)


def _even_in_sample_kernel(h_ref, nw_ref, wf_ref, wl_ref, wr_ref, wg_ref, bf_ref, cos_ref, sin_ref,
                           fq_ref, fk_ref, fv_ref, lf_ref, lft_ref, rq_ref, rk_ref, rv_ref, g_ref):
    ub = _rms(h_ref[...], nw_ref[...]).astype(BF16)
    zf = _mm(ub, wf_ref[...])
    fq_ref[...] = (zf[:, :HEAD_W] * QK_SCALE).astype(BF16)
    fk_ref[...] = zf[:, HEAD_W:2 * HEAD_W]
    fv_ref[...] = zf[:, 2 * HEAD_W:]
    logf = jax.nn.log_sigmoid(_mm(ub, wl_ref[...]) + bf_ref[...])
    lf_ref[...] = logf[:, :N_HEADS]
    lft_ref[...] = logf.T[:N_HEADS]
    _retention_gate_proj(ub, wr_ref, wg_ref, cos_ref, sin_ref, rq_ref, rk_ref, rv_ref, g_ref)


def _even_in_sample(h, nw, wf, wl, wr, wg, bf, cos, sin):
    rows = h.shape[0]
    sd = jax.ShapeDtypeStruct
    ins = [h, nw, wf, wl, wr, wg, bf, cos, sin]
    outs = [sd((rows, HEAD_W), BF16), sd((rows, HEAD_W), F32), sd((rows, HEAD_W), F32),
            sd((rows, N_HEADS), F32), sd((N_HEADS, rows), F32),
            sd((rows, HEAD_W), BF16), sd((rows, HEAD_W), F32), sd((rows, HEAD_W), BF16),
            sd((rows, 2 * HEAD_W), F32)]
    return pl.pallas_call(
        _even_in_sample_kernel,
        grid=(1,),
        in_specs=[_const_spec(a.shape) for a in ins],
        out_specs=[_const_spec(o.shape) for o in outs],
        out_shape=outs,
        compiler_params=_params("arbitrary"),
        name="even_in_sample",
    )(*ins)


def _seg_cumsum_kernel(x_ref, o_ref, *, seg):
    x = x_ref[...]
    pos = lax.broadcasted_iota(jnp.int32, x.shape, 1) % seg
    s = 1
    while s < seg:
        x = x + jnp.where(pos >= s, pltpu.roll(x, s, 1), 0.0)
        s *= 2
    o_ref[...] = x


def _seg_cumsum(x, *, seg):
    r, n = x.shape
    return pl.pallas_call(
        functools.partial(_seg_cumsum_kernel, seg=seg),
        grid=(r // 8,),
        in_specs=[pl.BlockSpec((8, n), lambda i: (i, 0))],
        out_specs=pl.BlockSpec((8, n), lambda i: (i, 0)),
        out_shape=jax.ShapeDtypeStruct((r, n), F32),
        compiler_params=_params("parallel"),
        name="seg_cumsum",
    )(x)


def _pair_masks(shape):
    lane = lax.broadcasted_iota(jnp.int32, shape, len(shape) - 1)
    return lane < HEAD_DIM


def _split3(x):
    hi = x.astype(BF16).astype(F32)
    rest = x - hi
    mid = rest.astype(BF16).astype(F32)
    return hi, mid, rest - mid


AUG_ROWS = 8
SUM_ROWS = 16


def _fox_prompt_kernel(qt_ref, k_ref, vt_ref, cr_ref, cc_ref, o_ref, kaug_scr, *, tq, tk):
    qi = pl.program_id(2)
    seq = k_ref.shape[0]

    @pl.when(qi == 0)
    def _():
        cc = cc_ref[...]
        lane = lax.broadcasted_iota(jnp.int32, (seq, PAIR_W), 1)
        aug = jnp.zeros((seq, PAIR_W), F32)
        for h in range(2):
            pieces = _split3(-cc[:, h:h + 1])
            for i in range(3):
                aug = jnp.where(lane == h * AUG_ROWS + i, pieces[i], aug)
                aug = jnp.where(lane == h * AUG_ROWS + 3 + i, 1.0, aug)
        kaug_scr[:, :PAIR_W] = k_ref[...]
        kaug_scr[:, PAIR_W:] = aug.astype(BF16)

    qt = qt_ref[...]
    row = lax.broadcasted_iota(jnp.int32, qt.shape, 0)
    zero = jnp.zeros_like(qt)
    cr = cr_ref[...]
    w_heads = []
    for h in range(2):
        head_rows = (row < HEAD_DIM) if h == 0 else (row >= HEAD_DIM)
        pieces = _split3(cr[h:h + 1, :])
        bias = jnp.zeros(qt.shape, F32)
        for i in range(3):
            bias = jnp.where(row == h * AUG_ROWS + i, 1.0, bias)
            bias = jnp.where(row == h * AUG_ROWS + 3 + i, pieces[i], bias)
        w_heads.append(jnp.concatenate([jnp.where(head_rows, qt, zero), bias.astype(BF16)], axis=0))
    key_l = lax.broadcasted_iota(jnp.int32, (tk, tq), 0)
    qry_l = lax.broadcasted_iota(jnp.int32, (tk, tq), 1)
    ones = jnp.ones((SUM_ROWS, tk), BF16)

    def step(start, carry, mask_offset):
        kc = kaug_scr[pl.ds(start, tk), :]
        vts = [jnp.concatenate([vt_ref[h * HEAD_DIM:(h + 1) * HEAD_DIM, pl.ds(start, tk)], ones], axis=0)
               for h in range(2)]
        scores = [_mm(kc, w_heads[h]) for h in range(2)]
        new = []
        for h in range(2):
            m, acc = carry[h]
            s = scores[h]
            if mask_offset is not None:
                s = jnp.where(key_l + mask_offset <= qry_l, s, NEG_INF)
            m_new = jnp.maximum(m, jnp.max(s, axis=0, keepdims=True))
            alpha = jnp.exp(m - m_new)
            p = jnp.exp(s - m_new).astype(BF16)
            new.append((m_new, alpha * acc + _mm(vts[h], p)))
        return tuple(new)

    init = (jnp.full((1, tq), NEG_INF, F32), jnp.zeros((HEAD_DIM + SUM_ROWS, tq), F32))
    n_sub = tq // tk

    def unmasked(jj, carry):
        for g in range(n_sub):
            carry = step(pl.multiple_of((jj * n_sub + g) * tk, tk), carry, None)
        return carry

    carry = lax.fori_loop(0, qi, unmasked, (init, init))
    base = qi * tq
    for d in range(n_sub):
        carry = step(pl.multiple_of(base + d * tk, tk), carry, d * tk)
    out = [acc[:HEAD_DIM, :] / acc[HEAD_DIM:HEAD_DIM + 1, :] for _, acc in carry]
    o_ref[...] = jnp.concatenate(out, axis=0).T


def _fox_prompt(qt, kb, vtb, cum_rows, cum_cols, *, tq, tk):
    b, l, _ = kb.shape
    return pl.pallas_call(
        functools.partial(_fox_prompt_kernel, tq=tq, tk=tk),
        grid=(b, N_PAIRS, l // tq),
        in_specs=[pl.BlockSpec((None, PAIR_W, tq), lambda bi, hp, qi: (bi, hp, qi)),
                  pl.BlockSpec((None, l, PAIR_W), lambda bi, hp, qi: (bi, 0, hp)),
                  pl.BlockSpec((None, PAIR_W, l), lambda bi, hp, qi: (bi, hp, 0)),
                  pl.BlockSpec((None, None, 2, tq), lambda bi, hp, qi: (bi, hp, 0, qi)),
                  pl.BlockSpec((None, None, l, 2), lambda bi, hp, qi: (bi, hp, 0, 0))],
        out_specs=pl.BlockSpec((None, tq, PAIR_W), lambda bi, hp, qi: (bi, qi, hp)),
        out_shape=jax.ShapeDtypeStruct((b, l, HEAD_W), F32),
        scratch_shapes=[pltpu.VMEM((l, 2 * PAIR_W), BF16)],
        compiler_params=_params("parallel", "parallel", "arbitrary"),
        name="fox_prompt",
    )(qt, kb, vtb, cum_rows, cum_cols)


def _ret_tables(lg_ref, length):
    n = RET_CHUNK
    lg_a = lg_ref[0:1, :]
    lg_b = lg_ref[1:2, :]
    lg_m = lg_ref[2:3, :]
    row = lax.broadcasted_iota(jnp.int32, (n, n), 0)
    col = lax.broadcasted_iota(jnp.int32, (n, n), 1)
    diff = row - col
    lower = diff >= 0
    dpos = jnp.where(lower, diff, 0).astype(F32)
    decay = (jnp.where(lower, jnp.exp(dpos * lg_a), 0.0), jnp.where(lower, jnp.exp(dpos * lg_b), 0.0))
    rowf = row.astype(F32)
    q_dec = jnp.exp((rowf + 1.0) * lg_m)
    k_dec = jnp.exp((length - 1.0 - rowf) * lg_m)
    s_dec = jnp.exp(float(length) * jnp.where(row < HEAD_DIM, lg_a, lg_b))
    same_head = (row < HEAD_DIM) == (col < HEAD_DIM)
    return decay, q_dec, k_dec, s_dec, same_head


def _ret_chunk(q, k, v, state, tables):
    decay, q_dec, k_dec, s_dec, same_head = tables
    first = _pair_masks(q.shape)
    zero = jnp.zeros_like(q)
    kb = k.astype(BF16)
    o_heads = []
    for h, qh in enumerate((jnp.where(first, q, zero), jnp.where(first, zero, q))):
        scores = _mm_nt(qh, kb) * decay[h]
        o_heads.append(_mm(scores.astype(BF16), v))
    o = jnp.where(first, o_heads[0], o_heads[1])
    o = o + _mm(q, state.astype(BF16)) * q_dec
    kd_t = (k * k_dec).T.astype(BF16)
    state = state * s_dec + jnp.where(same_head, _mm(kd_t, v), 0.0)
    return o, state


def _group_norm(o, gn_w):
    first = _pair_masks(o.shape)
    inv = 1.0 / HEAD_DIM
    s0 = jnp.sum(jnp.where(first, o, 0.0), axis=-1, keepdims=True)
    s1 = jnp.sum(jnp.where(first, 0.0, o), axis=-1, keepdims=True)
    d = o - jnp.where(first, s0, s1) * inv
    dd = d * d
    v0 = jnp.sum(jnp.where(first, dd, 0.0), axis=-1, keepdims=True)
    v1 = jnp.sum(jnp.where(first, 0.0, dd), axis=-1, keepdims=True)
    var = jnp.where(first, v0, v1) * inv
    return (d * lax.rsqrt(var + NORM_EPS)) * gn_w


def _ret_prompt_kernel(q_ref, k_ref, v_ref, lg_ref, gn_ref, o_ref, s_ref, *, n_chunks):
    tables = _ret_tables(lg_ref, RET_CHUNK)
    gn_w = gn_ref[...]

    def body(c, state):
        rows = pl.ds(pl.multiple_of(c * RET_CHUNK, RET_CHUNK), RET_CHUNK)
        o, state = _ret_chunk(q_ref[rows, :], k_ref[rows, :], v_ref[rows, :], state, tables)
        o_ref[rows, :] = _group_norm(o, gn_w)
        return state

    state = lax.fori_loop(0, n_chunks, body, jnp.zeros((PAIR_W, PAIR_W), F32), unroll=8)
    s_ref[0] = state[:HEAD_DIM, :HEAD_DIM]
    s_ref[1] = state[HEAD_DIM:, HEAD_DIM:]


def _ret_prompt(rq, rk, rv, lg_tab, gn_w):
    b, l, _ = rq.shape
    seq = pl.BlockSpec((None, l, PAIR_W), lambda bi, hp: (bi, 0, hp))
    return pl.pallas_call(
        functools.partial(_ret_prompt_kernel, n_chunks=l // RET_CHUNK),
        grid=(b, N_PAIRS),
        in_specs=[seq, seq, seq,
                  pl.BlockSpec((None, 8, PAIR_W), lambda bi, hp: (hp, 0, 0)),
                  pl.BlockSpec((1, PAIR_W), lambda bi, hp: (0, hp))],
        out_specs=[seq, pl.BlockSpec((None, 2, HEAD_DIM, HEAD_DIM), lambda bi, hp: (bi, hp, 0, 0))],
        out_shape=[jax.ShapeDtypeStruct((b, l, HEAD_W), F32),
                   jax.ShapeDtypeStruct((b, N_HEADS, HEAD_DIM, HEAD_DIM), F32)],
        compiler_params=_params("parallel", "parallel"),
        name="ret_prompt",
    )(rq, rk, rv, lg_tab, gn_w)


def _ret_sample_kernel(q_ref, k_ref, v_ref, s0_ref, lg_ref, gn_ref, o_ref, s_ref, *, length):
    pad = RET_CHUNK - length
    zrow_f = jnp.zeros((pad, PAIR_W), F32)
    zblk = jnp.zeros((HEAD_DIM, HEAD_DIM), F32)
    for hp in range(N_PAIRS):
        lanes = slice(hp * PAIR_W, (hp + 1) * PAIR_W)
        tables = _ret_tables(lg_ref.at[hp], length)
        q = jnp.concatenate([q_ref[:, lanes].astype(F32), zrow_f], axis=0).astype(BF16)
        k = jnp.concatenate([k_ref[:, lanes], zrow_f], axis=0)
        v = jnp.concatenate([v_ref[:, lanes].astype(F32), zrow_f], axis=0).astype(BF16)
        state = jnp.concatenate(
            [jnp.concatenate([s0_ref[2 * hp], zblk], axis=1),
             jnp.concatenate([zblk, s0_ref[2 * hp + 1]], axis=1)], axis=0)
        o, state = _ret_chunk(q, k, v, state, tables)
        o_ref[:, lanes] = _group_norm(o[:length], gn_ref[:, lanes])
        s_ref[2 * hp] = state[:HEAD_DIM, :HEAD_DIM]
        s_ref[2 * hp + 1] = state[HEAD_DIM:, HEAD_DIM:]


def _ret_sample(rq, rk, rv, state, lg_tab, gn_w, *, length):
    rows = rq.shape[0]
    b = rows // length
    tok = pl.BlockSpec((length, HEAD_W), lambda bi: (bi, 0))
    st = pl.BlockSpec((None, N_HEADS, HEAD_DIM, HEAD_DIM), lambda bi: (bi, 0, 0, 0))
    return pl.pallas_call(
        functools.partial(_ret_sample_kernel, length=length),
        grid=(b,),
        in_specs=[tok, tok, tok, st, _const_spec(lg_tab.shape), _const_spec(gn_w.shape)],
        out_specs=[tok, st],
        out_shape=[jax.ShapeDtypeStruct((rows, HEAD_W), F32),
                   jax.ShapeDtypeStruct(state.shape, F32)],
        compiler_params=_params("parallel"),
        name="ret_sample",
    )(rq, rk, rv, state, lg_tab, gn_w)


def _ple(h1, p, proj, pnw, pgw):
    e = _rms(_mm(p.astype(BF16), proj), pnw)
    return h1 + jax.nn.sigmoid(_mm(h1.astype(BF16), pgw)) * e


def _silu(g):
    return g * jax.nn.sigmoid(g)


def _even_out_kernel(h_ref, fo_ref, ro_ref, g_ref, wo_ref, p_ref, proj_ref, pnw_ref, pgw_ref, o_ref):
    sg = _silu(g_ref[...])
    cat_f = (fo_ref[...] * sg[:, :HEAD_W]).astype(BF16)
    cat_r = (ro_ref[...] * sg[:, HEAD_W:]).astype(BF16)
    mix = _mm(cat_f, wo_ref[:HEAD_W, :]) + _mm(cat_r, wo_ref[HEAD_W:, :])
    o_ref[...] = _ple(h_ref[...] + mix, p_ref[...], proj_ref[...], pnw_ref[...], pgw_ref[...])


def _even_out(h, fo, ro, g, wo, p, proj, pnw, pgw, *, layer, tm):
    rows, d = h.shape
    row = lambda w: pl.BlockSpec((tm, w), lambda i: (i, 0))
    return pl.pallas_call(
        _even_out_kernel,
        grid=(rows // tm,),
        in_specs=[row(d), row(HEAD_W), row(HEAD_W), row(2 * HEAD_W), _const_spec(wo.shape),
                  pl.BlockSpec((None, tm, p.shape[-1]), lambda i: (layer, i, 0)),
                  _const_spec(proj.shape), _const_spec(pnw.shape), _const_spec(pgw.shape)],
        out_specs=row(d),
        out_shape=jax.ShapeDtypeStruct((rows, d), F32),
        compiler_params=_params("parallel"),
        name="even_out",
    )(h, fo, ro, g, wo, p, proj, pnw, pgw)


def _odd_kernel(h_ref, nw_ref, win_ref, cw_ref, cb_ref, gw_ref, grb_ref, gib_ref, lam_ref, wout_ref,
                p_ref, proj_ref, pnw_ref, pgw_ref, h0_ref, cbuf_ref, fnw_ref,
                o_ref, lruh_ref, lruc_ref,
                ext_scr, carry_scr, y_scr, sg_scr, *, tl, final_norm):
    nb, _, d = h_ref.shape
    w = wout_ref.shape[0]
    rows = nb * tl
    halo = (CONV_W - 1) * nb

    @pl.when(pl.program_id(1) == 0)
    def _():
        carry_scr[...] = h0_ref[...]
        ext_scr[0:halo, :] = cbuf_ref[...].reshape(halo, w)

    x = jnp.swapaxes(h_ref[...], 0, 1).reshape(rows, d)
    z = _mm(_rms(x, nw_ref[...]).astype(BF16), win_ref[...])
    xb = z[:, :w]
    sg_scr[...] = _silu(z[:, w:])
    ext_scr[halo:halo + rows, :] = xb
    xc = xb * cw_ref[CONV_W - 1:CONV_W, :] + cb_ref[...]
    for kk in range(1, CONV_W):
        xc = xc + ext_scr[halo - kk * nb:halo - kk * nb + rows, :] * cw_ref[CONV_W - 1 - kk:CONV_W - kk, :]
    last_steps = ext_scr[rows:rows + halo, :]
    lruc_ref[...] = last_steps.reshape(CONV_W - 1, nb, w)
    ext_scr[0:halo, :] = last_steps

    sp = jax.nn.softplus(-lam_ref[...])
    for n in range(LRU_BLOCKS):
        lanes = slice(n * LRU_BW, (n + 1) * LRU_BW)
        xn = xc[:, lanes]
        zz = _mm(xn.astype(BF16), gw_ref[n])
        r = jax.nn.sigmoid(zz[:, :LRU_BW] + grb_ref[:, lanes])
        gi = jax.nn.sigmoid(zz[:, LRU_BW:] + gib_ref[:, lanes])
        log_a = (-LRU_C * r) * sp[:, lanes]
        a = jnp.exp(log_a)
        one_minus = -jnp.tanh(log_a) * (a * a + 1.0)
        b = (jnp.sqrt(one_minus) * gi) * xn
        hs = carry_scr[:, lanes]
        for t in range(tl):
            step = slice(t * nb, (t + 1) * nb)
            hs = a[step] * hs + b[step]
            y_scr[step, lanes] = hs
        carry_scr[:, lanes] = hs
        lruh_ref[:, lanes] = hs

    mix = _mm((y_scr[...] * sg_scr[...]).astype(BF16), wout_ref[...])
    p = jnp.swapaxes(p_ref[...], 0, 1).reshape(rows, -1)
    h2 = _ple(x + mix, p, proj_ref[...], pnw_ref[...], pgw_ref[...])
    if final_norm:
        h2 = _rms(h2, fnw_ref[...])
    o_ref[...] = jnp.swapaxes(h2.reshape(tl, nb, d), 0, 1)


def _odd_layer(h, nw, win, cw, cb, gw, grb, gib, lam, wout, p, proj, pnw, pgw, h0, cbuf, fnw,
               *, layer, nb, tl, final_norm):
    b, l, d = h.shape
    w = wout.shape[0]
    seq = lambda width: pl.BlockSpec((nb, tl, width), lambda bi, li: (bi, li, 0))
    state = pl.BlockSpec((nb, w), lambda bi, li: (bi, 0))
    conv = pl.BlockSpec((CONV_W - 1, nb, w), lambda bi, li: (0, bi, 0))
    consts = [nw, win, cw, cb, gw, grb, gib, lam, wout]
    tail = [proj, pnw, pgw]
    rows = nb * tl
    return pl.pallas_call(
        functools.partial(_odd_kernel, tl=tl, final_norm=final_norm),
        grid=(b // nb, l // tl),
        in_specs=[seq(d)] + [_const_spec(c.shape) for c in consts]
                 + [pl.BlockSpec((None, nb, tl, p.shape[-1]), lambda bi, li: (layer, bi, li, 0))]
                 + [_const_spec(c.shape) for c in tail] + [state, conv, _const_spec(fnw.shape)],
        out_specs=[seq(d), state, conv],
        out_shape=[jax.ShapeDtypeStruct((b, l, d), F32),
                   jax.ShapeDtypeStruct((b, w), F32),
                   jax.ShapeDtypeStruct((CONV_W - 1, b, w), F32)],
        scratch_shapes=[pltpu.VMEM(((CONV_W - 1) * nb + rows, w), F32), pltpu.VMEM((nb, w), F32),
                        pltpu.VMEM((rows, w), F32), pltpu.VMEM((rows, w), F32)],
        compiler_params=_params("parallel", "arbitrary"),
        name="odd_layer",
    )(h, *consts, p, *tail, h0, cbuf, fnw)


def _fox_sample_kernel(tbl_ref, q_ref, kn_ref, vn_ref, cnc_ref, cnr_ref, tri_ref, *refs, t_new):
    k_pages = refs[:PAGES_PER_STEP]
    v_pages = refs[PAGES_PER_STEP:2 * PAGES_PER_STEP]
    f_pages = refs[2 * PAGES_PER_STEP:3 * PAGES_PER_STEP]
    o_ref = refs[3 * PAGES_PER_STEP]
    qbd_scr, m_scr, l_scr, acc_scr, rel_scr = refs[3 * PAGES_PER_STEP + 1:]
    del tbl_ref
    c = pl.program_id(1)
    cn_col = cnc_ref[...]

    @pl.when(c == 0)
    def _():
        rel_scr[...] = jnp.zeros_like(rel_scr)
        q = q_ref[...].astype(F32)
        lane_head = lax.broadcasted_iota(jnp.int32, q.shape, 1) // HEAD_DIM
        zero = jnp.zeros_like(q)
        for h in range(N_HEADS):
            qbd_scr[h * t_new:(h + 1) * t_new, :] = jnp.where(lane_head == h, q, zero)
        m_scr[...] = jnp.full(m_scr.shape, NEG_INF, F32)
        l_scr[...] = jnp.zeros_like(l_scr)
        acc_scr[...] = jnp.zeros_like(acc_scr)

    qbd = qbd_scr[...].astype(BF16)

    def online(s, pv_fn):
        m = m_scr[...]
        m_new = jnp.maximum(m, jnp.max(s, axis=-1, keepdims=True))
        alpha = jnp.exp(m - m_new)
        p = jnp.exp(s - m_new)
        l_scr[...] = alpha * l_scr[...] + jnp.sum(p, axis=-1, keepdims=True)
        acc_scr[...] = alpha * acc_scr[...] + pv_fn(p.astype(BF16))
        m_scr[...] = m_new

    xs = jnp.concatenate([f_pages[i][...] for i in range(PAGES_PER_STEP)], axis=0)
    hi = xs.astype(BF16)
    rest = xs - hi.astype(F32)
    mid = rest.astype(BF16)
    lo = (rest - mid.astype(F32)).astype(BF16)
    tri = tri_ref[...]
    sums = (_mm(hi, tri) + _mm(mid, tri)) + _mm(lo, tri)
    after = rel_scr[...]
    bias = [None] * PAGES_PER_STEP
    for i in reversed(range(PAGES_PER_STEP)):
        rows = slice(i * N_HEADS, (i + 1) * N_HEADS)
        rel = sums[rows, :PAGE] + after
        after = after + sums[rows, PAGE:]
        bias[i] = jnp.concatenate(
            [jnp.broadcast_to(rel[h:h + 1, :], (t_new, PAGE)) for h in range(N_HEADS)], axis=0)
    rel_scr[...] = after

    kt = jnp.concatenate([k_pages[i][...].astype(BF16) for i in range(PAGES_PER_STEP)], axis=1)
    vt = jnp.concatenate([v_pages[i][...].astype(BF16) for i in range(PAGES_PER_STEP)], axis=1)
    s = (_mm(qbd, kt) + cn_col) + jnp.concatenate(bias, axis=1)
    online(s, lambda p: _mm_nt(p, vt))

    @pl.when(c == pl.num_programs(1) - 1)
    def _():
        zpad = jnp.zeros((PAGE - t_new, HEAD_W), F32)
        kn = jnp.concatenate([kn_ref[...], zpad], axis=0).astype(BF16)
        vn = jnp.concatenate([vn_ref[...], zpad], axis=0).astype(BF16)
        s = (_mm_nt(qbd, kn) + cn_col) - cnr_ref[...]
        t_query = lax.broadcasted_iota(jnp.int32, s.shape, 0) % t_new
        t_key = lax.broadcasted_iota(jnp.int32, s.shape, 1)
        s = jnp.where(t_query >= t_key, s, NEG_INF)
        online(s, lambda p: _mm(p, vn))
        o = acc_scr[...] / l_scr[...]
        lane_head = lax.broadcasted_iota(jnp.int32, (t_new, HEAD_W), 1) // HEAD_DIM
        out = jnp.zeros((t_new, HEAD_W), F32)
        for h in range(N_HEADS):
            out = jnp.where(lane_head == h, o[h * t_new:(h + 1) * t_new, :], out)
        o_ref[...] = out


def _fox_sample(q, k_new, v_new, cn_col, cn_rows, cache_kt, cache_vt, cache_ft, table, layer, *, t_new):
    rows = q.shape[0]
    b, n_pages = table.shape
    n_steps = n_pages // PAGES_PER_STEP
    n_rows = N_HEADS * t_new
    tok = pl.BlockSpec((t_new, HEAD_W), lambda bi, c, tbl: (bi, 0))

    def page_spec(i, height):
        return pl.BlockSpec(
            (None, None, height, PAGE),
            lambda bi, c, tbl: (layer, tbl[bi, (n_steps - 1 - c) * PAGES_PER_STEP + i], 0, 0))

    kv_pages = [page_spec(i, HEAD_W) for i in range(PAGES_PER_STEP)]
    f_pages = [page_spec(i, N_HEADS) for i in range(PAGES_PER_STEP)]
    pos = jnp.arange(PAGE)
    later = (pos[:, None] > pos[None, :]).astype(BF16)
    tri = jnp.concatenate([later, jnp.ones((PAGE, PAGE), BF16)], axis=1)
    return pl.pallas_call(
        functools.partial(_fox_sample_kernel, t_new=t_new),
        grid_spec=pltpu.PrefetchScalarGridSpec(
            num_scalar_prefetch=1,
            grid=(b, n_steps),
            in_specs=[tok, tok, tok,
                      pl.BlockSpec((None, n_rows, 1), lambda bi, c, tbl: (bi, 0, 0)),
                      pl.BlockSpec((None, n_rows, PAGE), lambda bi, c, tbl: (bi, 0, 0)),
                      pl.BlockSpec(tri.shape, lambda bi, c, tbl: (0, 0))]
                     + kv_pages + kv_pages + f_pages,
            out_specs=tok,
            scratch_shapes=[pltpu.VMEM((n_rows, HEAD_W), F32), pltpu.VMEM((n_rows, 1), F32),
                            pltpu.VMEM((n_rows, 1), F32), pltpu.VMEM((n_rows, HEAD_W), F32),
                            pltpu.VMEM((N_HEADS, PAGE), F32)]),
        out_shape=jax.ShapeDtypeStruct((rows, HEAD_W), F32),
        compiler_params=_params("parallel", "arbitrary"),
        name="fox_sample",
    )(table, q, k_new, v_new, cn_col, cn_rows, tri,
      *([cache_kt] * PAGES_PER_STEP), *([cache_vt] * PAGES_PER_STEP), *([cache_ft] * PAGES_PER_STEP))


def _rope_tables(pos):
    inv = ROPE_BASE ** (-jnp.arange(HALF_DIM, dtype=F32) / HALF_DIM)
    ang = pos[:, None] * inv[None, :]
    cos = jnp.cos(ang)
    sin = jnp.sin(ang)
    cos_head = jnp.concatenate([cos, cos], axis=-1)
    sin_head = jnp.concatenate([-sin, sin], axis=-1)
    return jnp.tile(cos_head, (1, N_HEADS)), jnp.tile(sin_head, (1, N_HEADS))


def _log_gamma_table(log_gamma):
    pairs = log_gamma.reshape(N_PAIRS, 2)
    ones = jnp.ones((N_PAIRS, 1, PAIR_W), F32)
    first = pairs[:, 0][:, None, None] * ones
    second = pairs[:, 1][:, None, None] * ones
    mix = jnp.repeat(pairs, HEAD_DIM, axis=1)[:, None, :]
    return jnp.concatenate([first, second, mix, jnp.zeros((N_PAIRS, 5, PAIR_W), F32)], axis=1)


def kernel(x_prompt, x_sample, cache_fox_k, cache_fox_v, cache_fox_logf, page_table, state_ret, state_lru_h, state_lru_conv, p_prompt, p_sample, norm_w, w_in_even, b_forget, ret_gn_w, w_out_even, w_in_odd, conv_w, conv_b, gate_r_w, gate_r_b, gate_i_w, gate_i_b, lru_lambda, w_out_odd, ple_proj, ple_norm_w, ple_gate_w, final_norm_w):
    bp, lp, d = x_prompt.shape
    bs, ls, _ = x_sample.shape
    depth = norm_w.shape[0]
    n_pool = cache_fox_k.shape[1]
    n_pages = page_table.shape[1]
    past_len = n_pages * PAGE
    lru_w = w_out_odd.shape[1]
    rows_p, rows_s = bp * lp, bs * ls
    tm_p = 256

    log_gamma = jnp.log1p(-jnp.exp2(-5.0 - jnp.arange(N_HEADS, dtype=F32)))
    lg_tab = _log_gamma_table(log_gamma)
    cos_p, sin_p = _rope_tables(jnp.arange(lp, dtype=F32))
    cos_s, sin_s = _rope_tables(past_len + jnp.arange(ls, dtype=F32))
    cos_s, sin_s = jnp.tile(cos_s, (bs, 1)), jnp.tile(sin_s, (bs, 1))

    cache_kt = cache_fox_k.transpose(0, 1, 3, 4, 2).reshape(-1, n_pool, HEAD_W, PAGE)
    cache_vt = cache_fox_v.transpose(0, 1, 3, 4, 2).reshape(-1, n_pool, HEAD_W, PAGE)
    cache_ft = jnp.swapaxes(cache_fox_logf, 2, 3)

    row2 = lambda v: v.reshape(1, -1)
    hp = x_prompt.reshape(rows_p, d)
    hs = x_sample.reshape(rows_s, d)
    outs = {k: [] for k in ("fk_p", "fv_p", "fl_p", "rs_p", "lh_p", "lc_p",
                            "fk_s", "fv_s", "fl_s", "rs_s", "lh_s", "lc_s")}
    y_p = y_s = None

    for i in range(depth):
        j = i // 2
        proj = ple_proj[i].astype(BF16)
        pgw = ple_gate_w[i].astype(BF16)
        pnw = row2(ple_norm_w[i])
        nw = row2(norm_w[i])
        if i % 2 == 0:
            w = w_in_even[j]
            c0 = 3 * HEAD_W
            wf = w[:, :c0].astype(BF16)
            wl = jnp.pad(w[:, c0:c0 + N_HEADS], ((0, 0), (0, PAGE - N_HEADS))).astype(BF16)
            wr = w[:, c0 + N_HEADS:2 * c0 + N_HEADS].astype(BF16)
            wg = w[:, 2 * c0 + N_HEADS:].astype(BF16)
            bf = jnp.pad(b_forget[j], (0, PAGE - N_HEADS)).reshape(1, PAGE)
            wo = w_out_even[j].astype(BF16)
            gn = row2(ret_gn_w[j])
            wt = jnp.swapaxes(w, 0, 1)
            wqkvt = wt[:c0].astype(BF16)
            wlt = jnp.pad(wt[c0:c0 + N_HEADS], ((0, N_HEADS), (0, 0))).astype(BF16)
            bfc = jnp.pad(b_forget[j], (0, N_HEADS)).reshape(2 * N_HEADS, 1)
            wk = w[:, HEAD_W:2 * HEAD_W].astype(BF16)

            fqt, fkt, fvt, fkb, fvtb, lft, rq, rk, rv, g = _even_in_prompt(
                hp.reshape(bp, lp, d), nw, wqkvt, wk, wlt, wr, wg, bfc, cos_p, sin_p, tm=tm_p)
            cum_rows = _seg_cumsum(lft.reshape(bp * N_HEADS, lp), seg=lp).reshape(bp, N_PAIRS, 2, lp)
            cum_cols = jnp.swapaxes(cum_rows, 2, 3)
            fo = _fox_prompt(fqt, fkb, fvtb, cum_rows, cum_cols, tq=512, tk=512)
            ro, s_p = _ret_prompt(rq, rk, rv, lg_tab, gn)
            hp = _even_out(hp, fo.reshape(rows_p, HEAD_W), ro.reshape(rows_p, HEAD_W),
                           g.reshape(rows_p, 2 * HEAD_W), wo, p_prompt.reshape(depth, rows_p, -1),
                           proj, pnw, pgw, layer=i, tm=tm_p)
            to_blhd = lambda t: t.reshape(bp, N_HEADS, HEAD_DIM, lp).transpose(0, 3, 1, 2)
            outs["fk_p"].append(to_blhd(fkt))
            outs["fv_p"].append(to_blhd(fvt))
            outs["fl_p"].append(jnp.swapaxes(lft, 1, 2))
            outs["rs_p"].append(s_p)

            fq, fk, fv, lf, lft, rq, rk, rv, g = _even_in_sample(hs, nw, wf, wl, wr, wg, bf, cos_s, sin_s)
            cn = _seg_cumsum(lft, seg=ls).reshape(N_HEADS, bs, ls).transpose(1, 0, 2)
            cn_col = cn.reshape(bs, N_HEADS * ls, 1)
            cn_rows = jnp.pad(jnp.repeat(cn, ls, axis=1), ((0, 0), (0, 0), (0, PAGE - ls)))
            fo = _fox_sample(fq, fk, fv, cn_col, cn_rows, cache_kt, cache_vt, cache_ft, page_table, j, t_new=ls)
            ro, s_s = _ret_sample(rq, rk, rv, state_ret[j], lg_tab, gn, length=ls)
            hs = _even_out(hs, fo, ro, g, wo, p_sample.reshape(depth, rows_s, -1),
                           proj, pnw, pgw, layer=i, tm=rows_s)
            outs["fk_s"].append(fk.reshape(bs, ls, N_HEADS, HEAD_DIM))
            outs["fv_s"].append(fv.reshape(bs, ls, N_HEADS, HEAD_DIM))
            outs["fl_s"].append(lf.reshape(bs, ls, N_HEADS))
            outs["rs_s"].append(s_s)
        else:
            final = i == depth - 1
            gw = jnp.concatenate([gate_r_w[j], gate_i_w[j]], axis=-1).astype(BF16)
            args = (nw, w_in_odd[j].astype(BF16), conv_w[j], row2(conv_b[j]), gw, row2(gate_r_b[j]),
                    row2(gate_i_b[j]), row2(lru_lambda[j]), w_out_odd[j].astype(BF16))
            tail = (proj, pnw, pgw)
            fnw = row2(final_norm_w)
            o, lh, lc = _odd_layer(hp.reshape(bp, lp, d), *args, p_prompt, *tail,
                                   jnp.zeros((bp, lru_w), F32), jnp.zeros((CONV_W - 1, bp, lru_w), F32),
                                   fnw, layer=i, nb=SEQ_PER_BLOCK, tl=32, final_norm=final)
            hp = o.reshape(rows_p, d)
            outs["lh_p"].append(lh)
            outs["lc_p"].append(jnp.swapaxes(lc, 0, 1))
            o, lh, lc = _odd_layer(hs.reshape(bs, ls, d), *args, p_sample, *tail,
                                   state_lru_h[j], jnp.swapaxes(state_lru_conv[j], 0, 1),
                                   fnw, layer=i, nb=SEQ_PER_BLOCK, tl=ls, final_norm=final)
            hs = o.reshape(rows_s, d)
            outs["lh_s"].append(lh)
            outs["lc_s"].append(jnp.swapaxes(lc, 0, 1))
            if final:
                y_p, y_s = hp.reshape(bp, lp, d), hs.reshape(bs, ls, d)

    st = lambda k: jnp.stack(outs[k])
    return (y_p, y_s, st("fk_p"), st("fv_p"), st("fl_p"), st("rs_p"), st("lh_p"), st("lc_p"),
            st("fk_s"), st("fv_s"), st("fl_s"), st("rs_s"), st("lh_s"), st("lc_s"))
```

```python
import functools
import math

import jax
import jax.numpy as jnp
from jax import lax
from jax.experimental import pallas as pl
from jax.experimental.pallas import tpu as pltpu

F32 = jnp.float32
BF16 = jnp.bfloat16

HEAD_DIM = 64
HALF_DIM = HEAD_DIM // 2
N_HEADS = 8
HEAD_W = N_HEADS * HEAD_DIM
PAIR_W = 2 * HEAD_DIM
N_PAIRS = N_HEADS // 2
PAGE = 128
PAGES_PER_STEP = 32
RET_CHUNK = 128
RET_GROUP = 8
ROPE_BASE = 10000.0
LRU_BLOCKS = 8
LRU_BW = 128
CONV_W = 4
SEQ_PER_BLOCK = 8
LRU_C = 8.0
NORM_EPS = 1e-6
NEG_INF = -1e30
QK_SCALE = HEAD_DIM ** -0.5
VMEM_LIMIT = 56 * 1024 * 1024


def _params(*sem):
    return pltpu.CompilerParams(dimension_semantics=sem, vmem_limit_bytes=VMEM_LIMIT)


def _mm(a, b):
    return jnp.dot(a, b, preferred_element_type=F32)


def _mm_nt(a, b):
    return lax.dot_general(a, b, (((1,), (1,)), ((), ())), preferred_element_type=F32)


def _rms(x, w):
    return (x * lax.rsqrt(jnp.mean(x * x, axis=-1, keepdims=True) + NORM_EPS)) * w


def _const_spec(shape):
    nd = len(shape)
    return pl.BlockSpec(shape, lambda *_: (0,) * nd)


def _swap_halves(x):
    n = x.shape[-1]
    lane = lax.broadcasted_iota(jnp.int32, x.shape, x.ndim - 1)
    return jnp.where(lane % HEAD_DIM < HALF_DIM,
                     pltpu.roll(x, n - HALF_DIM, x.ndim - 1),
                     pltpu.roll(x, HALF_DIM, x.ndim - 1))


def _retention_gate_proj(ub, wr_ref, wg_ref, cos_ref, sin_ref, rq_ref, rk_ref, rv_ref, g_ref):
    zr = _mm(ub, wr_ref[...])
    cos = cos_ref[...]
    sin = sin_ref[...]
    rq = zr[:, :HEAD_W]
    rk = zr[:, HEAD_W:2 * HEAD_W]
    rq_ref[...] = (rq * cos + _swap_halves(rq) * sin).astype(BF16)
    rk_ref[...] = (rk * cos + _swap_halves(rk) * sin) * QK_SCALE
    rv_ref[...] = zr[:, 2 * HEAD_W:].astype(BF16)
    g_ref[...] = _mm(ub, wg_ref[...])


N_EVEN_IN_PROMPT_INPUTS = 9


def _even_in_prompt_kernel(*refs):
    h_ref, nw_ref, wqkvt_ref, wlt_ref, wr_ref, wg_ref, bfc_ref, cos_ref, sin_ref = refs[:N_EVEN_IN_PROMPT_INPUTS]
    (fqt_ref, fkt_ref, fvt_ref, fkb_ref, fvtb_ref, lft_ref, rq_ref, rk_ref, rv_ref, g_ref) = refs[-10:]
    ub = _rms(h_ref[...], nw_ref[...]).astype(BF16)
    zt = _mm_nt(wqkvt_ref[...], ub)
    fqt_ref[...] = (zt[:HEAD_W] * QK_SCALE).astype(BF16)
    fkt = zt[HEAD_W:2 * HEAD_W]
    fkt_ref[...] = fkt
    fkb_ref[...] = fkt.T.astype(BF16)
    fvt = zt[2 * HEAD_W:]
    fvt_ref[...] = fvt
    fvtb_ref[...] = fvt.astype(BF16)
    lft_ref[...] = jax.nn.log_sigmoid(_mm_nt(wlt_ref[...], ub) + bfc_ref[...])[:N_HEADS]
    _retention_gate_proj(ub, wr_ref, wg_ref, cos_ref, sin_ref, rq_ref, rk_ref, rv_ref, g_ref)


def _even_in_prompt(h, nw, wqkvt, wlt, wr, wg, bfc, cos, sin, stacked_kv, *, layer, n_layers, tm):
    b, l, d = h.shape
    tok = lambda w: pl.BlockSpec((None, tm, w), lambda bi, i: (bi, i, 0))
    feat = lambda r: pl.BlockSpec((None, r, tm), lambda bi, i: (bi, 0, i))
    stacked = pl.BlockSpec((None, None, HEAD_W, tm), lambda bi, i: (layer, bi, 0, i))
    pos = pl.BlockSpec((tm, HEAD_W), lambda bi, i: (i, 0))
    sd = jax.ShapeDtypeStruct
    consts = [nw, wqkvt, wlt, wr, wg, bfc]
    ins = [h, *consts, cos, sin]
    in_specs = [tok(d)] + [_const_spec(c.shape) for c in consts] + [pos, pos]
    assert len(ins) == N_EVEN_IN_PROMPT_INPUTS
    aliases = {}
    if stacked_kv is not None:
        aliases = {len(ins): 1, len(ins) + 1: 2}
        ins += list(stacked_kv)
        in_specs += [pl.BlockSpec(memory_space=pl.ANY)] * 2
    return pl.pallas_call(
        _even_in_prompt_kernel,
        grid=(b, l // tm),
        in_specs=in_specs,
        out_specs=[feat(HEAD_W), stacked, stacked, tok(HEAD_W), feat(HEAD_W), feat(N_HEADS),
                   tok(HEAD_W), tok(HEAD_W), tok(HEAD_W), tok(2 * HEAD_W)],
        out_shape=[sd((b, HEAD_W, l), BF16), sd((n_layers, b, HEAD_W, l), F32), sd((n_layers, b, HEAD_W, l), F32),
                   sd((b, l, HEAD_W), BF16), sd((b, HEAD_W, l), BF16), sd((b, N_HEADS, l), F32),
                   sd((b, l, HEAD_W), BF16), sd((b, l, HEAD_W), F32), sd((b, l, HEAD_W), BF16),
                   sd((b, l, 2 * HEAD_W), F32)],
        input_output_aliases=aliases,
        compiler_params=_params("parallel", "parallel"),
        name="even_in_prompt",
    )(*ins)


def _even_in_sample_kernel(h_ref, nw_ref, wf_ref, wl_ref, wr_ref, wg_ref, bf_ref, cos_ref, sin_ref,
                           fq_ref, fk_ref, fv_ref, lf_ref, lft_ref, rq_ref, rk_ref, rv_ref, g_ref):
    ub = _rms(h_ref[...], nw_ref[...]).astype(BF16)
    zf = _mm(ub, wf_ref[...])
    fq_ref[...] = (zf[:, :HEAD_W] * QK_SCALE).astype(BF16)
    fk_ref[...] = zf[:, HEAD_W:2 * HEAD_W]
    fv_ref[...] = zf[:, 2 * HEAD_W:]
    logf = jax.nn.log_sigmoid(_mm(ub, wl_ref[...]) + bf_ref[...])
    lf_ref[...] = logf[:, :N_HEADS]
    lft_ref[...] = logf.T[:N_HEADS]
    _retention_gate_proj(ub, wr_ref, wg_ref, cos_ref, sin_ref, rq_ref, rk_ref, rv_ref, g_ref)


def _even_in_sample(h, nw, wf, wl, wr, wg, bf, cos, sin):
    rows = h.shape[0]
    sd = jax.ShapeDtypeStruct
    ins = [h, nw, wf, wl, wr, wg, bf, cos, sin]
    outs = [sd((rows, HEAD_W), BF16), sd((rows, HEAD_W), F32), sd((rows, HEAD_W), F32),
            sd((rows, N_HEADS), F32), sd((N_HEADS, rows), F32),
            sd((rows, HEAD_W), BF16), sd((rows, HEAD_W), F32), sd((rows, HEAD_W), BF16),
            sd((rows, 2 * HEAD_W), F32)]
    return pl.pallas_call(
        _even_in_sample_kernel,
        grid=(1,),
        in_specs=[_const_spec(a.shape) for a in ins],
        out_specs=[_const_spec(o.shape) for o in outs],
        out_shape=outs,
        compiler_params=_params("arbitrary"),
        name="even_in_sample",
    )(*ins)


def _seg_cumsum_kernel(x_ref, o_ref, *, seg):
    x = x_ref[...]
    pos = lax.broadcasted_iota(jnp.int32, x.shape, 1) % seg
    s = 1
    while s < seg:
        x = x + jnp.where(pos >= s, pltpu.roll(x, s, 1), 0.0)
        s *= 2
    o_ref[...] = x


def _seg_cumsum(x, *, seg):
    r, n = x.shape
    return pl.pallas_call(
        functools.partial(_seg_cumsum_kernel, seg=seg),
        grid=(r // 8,),
        in_specs=[pl.BlockSpec((8, n), lambda i: (i, 0))],
        out_specs=pl.BlockSpec((8, n), lambda i: (i, 0)),
        out_shape=jax.ShapeDtypeStruct((r, n), F32),
        compiler_params=_params("parallel"),
        name="seg_cumsum",
    )(x)


def _pair_masks(shape):
    lane = lax.broadcasted_iota(jnp.int32, shape, len(shape) - 1)
    return lane < HEAD_DIM


def _split3(x):
    hi = x.astype(BF16).astype(F32)
    rest = x - hi
    mid = rest.astype(BF16).astype(F32)
    return hi, mid, rest - mid


AUG_ROWS = 8
SUM_ROWS = 16


def _fox_prompt_kernel(qt_ref, k_ref, vt_ref, cr_ref, cc_ref, o_ref, kaug_scr, *, tq, tk):
    qi = pl.program_id(2)
    seq = k_ref.shape[0]

    @pl.when(qi == 0)
    def _():
        cc = cc_ref[...]
        lane = lax.broadcasted_iota(jnp.int32, (seq, PAIR_W), 1)
        aug = jnp.zeros((seq, PAIR_W), F32)
        for h in range(2):
            pieces = _split3(-cc[:, h:h + 1])
            for i in range(3):
                aug = jnp.where(lane == h * AUG_ROWS + i, pieces[i], aug)
                aug = jnp.where(lane == h * AUG_ROWS + 3 + i, 1.0, aug)
        kaug_scr[:, :PAIR_W] = k_ref[...]
        kaug_scr[:, PAIR_W:] = aug.astype(BF16)

    qt = qt_ref[...]
    row = lax.broadcasted_iota(jnp.int32, qt.shape, 0)
    zero = jnp.zeros_like(qt)
    cr = cr_ref[...]
    w_heads = []
    for h in range(2):
        head_rows = (row < HEAD_DIM) if h == 0 else (row >= HEAD_DIM)
        pieces = _split3(cr[h:h + 1, :])
        bias = jnp.zeros(qt.shape, F32)
        for i in range(3):
            bias = jnp.where(row == h * AUG_ROWS + i, 1.0, bias)
            bias = jnp.where(row == h * AUG_ROWS + 3 + i, pieces[i], bias)
        w_heads.append(jnp.concatenate([jnp.where(head_rows, qt, zero), bias.astype(BF16)], axis=0))
    key_l = lax.broadcasted_iota(jnp.int32, (tk, tq), 0)
    qry_l = lax.broadcasted_iota(jnp.int32, (tk, tq), 1)
    ones = jnp.ones((SUM_ROWS, tk), BF16)

    def scores_of(j):
        kc = kaug_scr[j * tk:(j + 1) * tk, :]
        return tuple(_mm(kc, w_heads[h]) for h in range(2))

    def values_of(j, p_heads):
        start = j * tk
        return [_mm(jnp.concatenate([vt_ref[h * HEAD_DIM:(h + 1) * HEAD_DIM, pl.ds(start, tk)], ones], axis=0),
                    p_heads[h]) for h in range(2)]

    def softmax(s, m, diagonal):
        if diagonal:
            s = jnp.where(key_l <= qry_l, s, NEG_INF)
        m_new = jnp.maximum(m, jnp.max(s, axis=0, keepdims=True))
        return m_new, jnp.exp(m - m_new), jnp.exp(s - m_new).astype(BF16)

    def run(n_blocks):
        m_heads = [jnp.full((1, tq), NEG_INF, F32)] * 2
        acc_heads = [jnp.zeros((HEAD_DIM + SUM_ROWS, tq), F32)] * 2
        s_cur = scores_of(0)
        for j in range(n_blocks):
            s_next = scores_of(j + 1) if j + 1 < n_blocks else None
            p_cur, alphas = [], []
            for h in range(2):
                m_heads[h], alpha, p = softmax(s_cur[h], m_heads[h], j + 1 == n_blocks)
                alphas.append(alpha)
                p_cur.append(p)
            pv = values_of(j, p_cur)
            acc_heads = [alphas[h] * acc_heads[h] + pv[h] for h in range(2)]
            s_cur = s_next
        out = [acc[:HEAD_DIM, :] / acc[HEAD_DIM:HEAD_DIM + 1, :] for acc in acc_heads]
        o_ref[...] = jnp.concatenate(out, axis=0).T

    for q_block in range(seq // tq):
        pl.when(qi == q_block)(functools.partial(run, q_block + 1))


def _fox_prompt(qt, kb, vtb, cum_rows, cum_cols, *, tq, tk):
    b, l, _ = kb.shape
    return pl.pallas_call(
        functools.partial(_fox_prompt_kernel, tq=tq, tk=tk),
        grid=(b, N_PAIRS, l // tq),
        in_specs=[pl.BlockSpec((None, PAIR_W, tq), lambda bi, hp, qi: (bi, hp, qi)),
                  pl.BlockSpec((None, l, PAIR_W), lambda bi, hp, qi: (bi, 0, hp)),
                  pl.BlockSpec((None, PAIR_W, l), lambda bi, hp, qi: (bi, hp, 0)),
                  pl.BlockSpec((None, None, 2, tq), lambda bi, hp, qi: (bi, hp, 0, qi)),
                  pl.BlockSpec((None, None, l, 2), lambda bi, hp, qi: (bi, hp, 0, 0))],
        out_specs=pl.BlockSpec((None, tq, PAIR_W), lambda bi, hp, qi: (bi, qi, hp)),
        out_shape=jax.ShapeDtypeStruct((b, l, HEAD_W), F32),
        scratch_shapes=[pltpu.VMEM((l, 2 * PAIR_W), BF16)],
        compiler_params=_params("parallel", "parallel", "arbitrary"),
        name="fox_prompt",
    )(qt, kb, vtb, cum_rows, cum_cols)


def _ret_tables(lg_ref, length):
    n = RET_CHUNK
    lg_a = lg_ref[0:1, :]
    lg_b = lg_ref[1:2, :]
    lg_m = lg_ref[2:3, :]
    row = lax.broadcasted_iota(jnp.int32, (n, n), 0)
    col = lax.broadcasted_iota(jnp.int32, (n, n), 1)
    diff = row - col
    lower = diff >= 0
    dpos = jnp.where(lower, diff, 0).astype(F32)
    decay = (jnp.where(lower, jnp.exp(dpos * lg_a), 0.0), jnp.where(lower, jnp.exp(dpos * lg_b), 0.0))
    rowf = row.astype(F32)
    q_dec = jnp.exp((rowf + 1.0) * lg_m)
    k_dec = jnp.exp((length - 1.0 - rowf) * lg_m)
    s_dec = jnp.exp(float(length) * jnp.where(row < HEAD_DIM, lg_a, lg_b))
    same_head = (row < HEAD_DIM) == (col < HEAD_DIM)
    return decay, q_dec, k_dec, s_dec, same_head


def _ret_chunk(q, k, v, state, tables):
    decay, q_dec, k_dec, s_dec, same_head = tables
    first = _pair_masks(q.shape)
    zero = jnp.zeros_like(q)
    kb = k.astype(BF16)
    o_heads = []
    for h, qh in enumerate((jnp.where(first, q, zero), jnp.where(first, zero, q))):
        scores = _mm_nt(qh, kb) * decay[h]
        o_heads.append(_mm(scores.astype(BF16), v))
    o = jnp.where(first, o_heads[0], o_heads[1])
    o = o + _mm(q, state.astype(BF16)) * q_dec
    kd_t = (k * k_dec).T.astype(BF16)
    state = state * s_dec + jnp.where(same_head, _mm(kd_t, v), 0.0)
    return o, state


def _group_norm(o, gn_w):
    first = _pair_masks(o.shape)
    inv = 1.0 / HEAD_DIM
    s0 = jnp.sum(jnp.where(first, o, 0.0), axis=-1, keepdims=True)
    s1 = jnp.sum(jnp.where(first, 0.0, o), axis=-1, keepdims=True)
    d = o - jnp.where(first, s0, s1) * inv
    dd = d * d
    v0 = jnp.sum(jnp.where(first, dd, 0.0), axis=-1, keepdims=True)
    v1 = jnp.sum(jnp.where(first, 0.0, dd), axis=-1, keepdims=True)
    var = jnp.where(first, v0, v1) * inv
    return (d * lax.rsqrt(var + NORM_EPS)) * gn_w


def _ret_prompt_kernel(q_ref, k_ref, v_ref, lg_ref, gn_ref, o_ref, s_ref, *, n_chunks):
    tables = _ret_tables(lg_ref, RET_CHUNK)
    gn_w = gn_ref[...]

    decay, q_dec, k_dec, s_dec, same_head = tables
    group = RET_GROUP

    def body(cc, state):
        rows = [pl.ds(pl.multiple_of((cc * group + g) * RET_CHUNK, RET_CHUNK), RET_CHUNK)
                for g in range(group)]
        qs = [q_ref[r, :] for r in rows]
        ks = [k_ref[r, :] for r in rows]
        vs = [v_ref[r, :] for r in rows]
        first = _pair_masks(qs[0].shape)
        zero = jnp.zeros_like(qs[0])
        kbs = [k.astype(BF16) for k in ks]
        scores = [[_mm_nt(jnp.where(first, q, zero) if h == 0 else jnp.where(first, zero, q), kb)
                   for h in range(2)] for q, kb in zip(qs, kbs)]
        incs = [jnp.where(same_head, _mm((k * k_dec).T.astype(BF16), v), 0.0) for k, v in zip(ks, vs)]
        intra = [[_mm((sc[h] * decay[h]).astype(BF16), v) for h in range(2)] for sc, v in zip(scores, vs)]
        states = []
        for g in range(group):
            states.append(state)
            state = state * s_dec + incs[g]
        cross = [_mm(q, s.astype(BF16)) * q_dec for q, s in zip(qs, states)]
        for g in range(group):
            o = jnp.where(first, intra[g][0], intra[g][1]) + cross[g]
            o_ref[rows[g], :] = _group_norm(o, gn_w)
        return state

    state = lax.fori_loop(0, n_chunks // group, body, jnp.zeros((PAIR_W, PAIR_W), F32))
    s_ref[0] = state[:HEAD_DIM, :HEAD_DIM]
    s_ref[1] = state[HEAD_DIM:, HEAD_DIM:]


def _ret_prompt(rq, rk, rv, lg_tab, gn_w):
    b, l, _ = rq.shape
    seq = pl.BlockSpec((None, l, PAIR_W), lambda bi, hp: (bi, 0, hp))
    return pl.pallas_call(
        functools.partial(_ret_prompt_kernel, n_chunks=l // RET_CHUNK),
        grid=(b, N_PAIRS),
        in_specs=[seq, seq, seq,
                  pl.BlockSpec((None, 8, PAIR_W), lambda bi, hp: (hp, 0, 0)),
                  pl.BlockSpec((1, PAIR_W), lambda bi, hp: (0, hp))],
        out_specs=[seq, pl.BlockSpec((None, 2, HEAD_DIM, HEAD_DIM), lambda bi, hp: (bi, hp, 0, 0))],
        out_shape=[jax.ShapeDtypeStruct((b, l, HEAD_W), F32),
                   jax.ShapeDtypeStruct((b, N_HEADS, HEAD_DIM, HEAD_DIM), F32)],
        compiler_params=_params("parallel", "parallel"),
        name="ret_prompt",
    )(rq, rk, rv, lg_tab, gn_w)


def _ret_sample_kernel(q_ref, k_ref, v_ref, s0_ref, lg_ref, gn_ref, o_ref, s_ref, *, length):
    pad = RET_CHUNK - length
    zrow_f = jnp.zeros((pad, PAIR_W), F32)
    zblk = jnp.zeros((HEAD_DIM, HEAD_DIM), F32)
    for hp in range(N_PAIRS):
        lanes = slice(hp * PAIR_W, (hp + 1) * PAIR_W)
        tables = _ret_tables(lg_ref.at[hp], length)
        q = jnp.concatenate([q_ref[:, lanes].astype(F32), zrow_f], axis=0).astype(BF16)
        k = jnp.concatenate([k_ref[:, lanes], zrow_f], axis=0)
        v = jnp.concatenate([v_ref[:, lanes].astype(F32), zrow_f], axis=0).astype(BF16)
        state = jnp.concatenate(
            [jnp.concatenate([s0_ref[2 * hp], zblk], axis=1),
             jnp.concatenate([zblk, s0_ref[2 * hp + 1]], axis=1)], axis=0)
        o, state = _ret_chunk(q, k, v, state, tables)
        o_ref[:, lanes] = _group_norm(o[:length], gn_ref[:, lanes])
        s_ref[2 * hp] = state[:HEAD_DIM, :HEAD_DIM]
        s_ref[2 * hp + 1] = state[HEAD_DIM:, HEAD_DIM:]


def _ret_sample(rq, rk, rv, state, lg_tab, gn_w, *, length):
    rows = rq.shape[0]
    b = rows // length
    tok = pl.BlockSpec((length, HEAD_W), lambda bi: (bi, 0))
    st = pl.BlockSpec((None, N_HEADS, HEAD_DIM, HEAD_DIM), lambda bi: (bi, 0, 0, 0))
    return pl.pallas_call(
        functools.partial(_ret_sample_kernel, length=length),
        grid=(b,),
        in_specs=[tok, tok, tok, st, _const_spec(lg_tab.shape), _const_spec(gn_w.shape)],
        out_specs=[tok, st],
        out_shape=[jax.ShapeDtypeStruct((rows, HEAD_W), F32),
                   jax.ShapeDtypeStruct(state.shape, F32)],
        compiler_params=_params("parallel"),
        name="ret_sample",
    )(rq, rk, rv, state, lg_tab, gn_w)


def _ple(h1, p, proj, pnw, pgw):
    e = _rms(_mm(p.astype(BF16), proj), pnw)
    return h1 + jax.nn.sigmoid(_mm(h1.astype(BF16), pgw)) * e


def _silu(g):
    return g * jax.nn.sigmoid(g)


def _even_out_kernel(h_ref, fo_ref, ro_ref, g_ref, wo_ref, p_ref, proj_ref, pnw_ref, pgw_ref, o_ref):
    sg = _silu(g_ref[...])
    cat_f = (fo_ref[...] * sg[:, :HEAD_W]).astype(BF16)
    cat_r = (ro_ref[...] * sg[:, HEAD_W:]).astype(BF16)
    mix = _mm(cat_f, wo_ref[:HEAD_W, :]) + _mm(cat_r, wo_ref[HEAD_W:, :])
    o_ref[...] = _ple(h_ref[...] + mix, p_ref[...], proj_ref[...], pnw_ref[...], pgw_ref[...])


def _even_out(h, fo, ro, g, wo, p, proj, pnw, pgw, *, layer, tm):
    rows, d = h.shape
    row = lambda w: pl.BlockSpec((tm, w), lambda i: (i, 0))
    return pl.pallas_call(
        _even_out_kernel,
        grid=(rows // tm,),
        in_specs=[row(d), row(HEAD_W), row(HEAD_W), row(2 * HEAD_W), _const_spec(wo.shape),
                  pl.BlockSpec((None, tm, p.shape[-1]), lambda i: (layer, i, 0)),
                  _const_spec(proj.shape), _const_spec(pnw.shape), _const_spec(pgw.shape)],
        out_specs=row(d),
        out_shape=jax.ShapeDtypeStruct((rows, d), F32),
        compiler_params=_params("parallel"),
        name="even_out",
    )(h, fo, ro, g, wo, p, proj, pnw, pgw)


def _odd_kernel(h_ref, nw_ref, win_ref, cw_ref, cb_ref, gw_ref, grb_ref, gib_ref, lam_ref, wout_ref,
                p_ref, proj_ref, pnw_ref, pgw_ref, h0_ref, cbuf_ref, fnw_ref,
                o_ref, lruh_ref, lruc_ref,
                ext_scr, carry_scr, y_scr, sg_scr, *, tl, final_norm):
    nb, _, d = h_ref.shape
    w = wout_ref.shape[0]
    rows = nb * tl
    halo = (CONV_W - 1) * nb

    @pl.when(pl.program_id(1) == 0)
    def _():
        carry_scr[...] = h0_ref[...]
        ext_scr[0:halo, :] = cbuf_ref[...].reshape(halo, w)

    x = jnp.swapaxes(h_ref[...], 0, 1).reshape(rows, d)
    z = _mm(_rms(x, nw_ref[...]).astype(BF16), win_ref[...])
    xb = z[:, :w]
    sg_scr[...] = _silu(z[:, w:])
    ext_scr[halo:halo + rows, :] = xb
    xc = xb * cw_ref[CONV_W - 1:CONV_W, :] + cb_ref[...]
    for kk in range(1, CONV_W):
        xc = xc + ext_scr[halo - kk * nb:halo - kk * nb + rows, :] * cw_ref[CONV_W - 1 - kk:CONV_W - kk, :]
    last_steps = ext_scr[rows:rows + halo, :]
    lruc_ref[...] = last_steps.reshape(CONV_W - 1, nb, w)
    ext_scr[0:halo, :] = last_steps

    sp = jax.nn.softplus(-lam_ref[...])
    for n in range(LRU_BLOCKS):
        lanes = slice(n * LRU_BW, (n + 1) * LRU_BW)
        xn = xc[:, lanes]
        zz = _mm(xn.astype(BF16), gw_ref[n])
        r = jax.nn.sigmoid(zz[:, :LRU_BW] + grb_ref[:, lanes])
        gi = jax.nn.sigmoid(zz[:, LRU_BW:] + gib_ref[:, lanes])
        log_a = (-LRU_C * r) * sp[:, lanes]
        a = jnp.exp(log_a)
        one_minus = -jnp.tanh(log_a) * (a * a + 1.0)
        b = (jnp.sqrt(one_minus) * gi) * xn
        hs = carry_scr[:, lanes]
        for t in range(tl):
            step = slice(t * nb, (t + 1) * nb)
            hs = a[step] * hs + b[step]
            y_scr[step, lanes] = hs
        carry_scr[:, lanes] = hs
        lruh_ref[:, lanes] = hs

    mix = _mm((y_scr[...] * sg_scr[...]).astype(BF16), wout_ref[...])
    p = jnp.swapaxes(p_ref[...], 0, 1).reshape(rows, -1)
    h2 = _ple(x + mix, p, proj_ref[...], pnw_ref[...], pgw_ref[...])
    if final_norm:
        h2 = _rms(h2, fnw_ref[...])
    o_ref[...] = jnp.swapaxes(h2.reshape(tl, nb, d), 0, 1)


def _odd_layer(h, nw, win, cw, cb, gw, grb, gib, lam, wout, p, proj, pnw, pgw, h0, cbuf, fnw,
               *, layer, nb, tl, final_norm):
    b, l, d = h.shape
    w = wout.shape[0]
    seq = lambda width: pl.BlockSpec((nb, tl, width), lambda bi, li: (bi, li, 0))
    state = pl.BlockSpec((nb, w), lambda bi, li: (bi, 0))
    conv = pl.BlockSpec((CONV_W - 1, nb, w), lambda bi, li: (0, bi, 0))
    consts = [nw, win, cw, cb, gw, grb, gib, lam, wout]
    tail = [proj, pnw, pgw]
    rows = nb * tl
    return pl.pallas_call(
        functools.partial(_odd_kernel, tl=tl, final_norm=final_norm),
        grid=(b // nb, l // tl),
        in_specs=[seq(d)] + [_const_spec(c.shape) for c in consts]
                 + [pl.BlockSpec((None, nb, tl, p.shape[-1]), lambda bi, li: (layer, bi, li, 0))]
                 + [_const_spec(c.shape) for c in tail] + [state, conv, _const_spec(fnw.shape)],
        out_specs=[seq(d), state, conv],
        out_shape=[jax.ShapeDtypeStruct((b, l, d), F32),
                   jax.ShapeDtypeStruct((b, w), F32),
                   jax.ShapeDtypeStruct((CONV_W - 1, b, w), F32)],
        scratch_shapes=[pltpu.VMEM(((CONV_W - 1) * nb + rows, w), F32), pltpu.VMEM((nb, w), F32),
                        pltpu.VMEM((rows, w), F32), pltpu.VMEM((rows, w), F32)],
        compiler_params=_params("parallel", "arbitrary"),
        name="odd_layer",
    )(h, *consts, p, *tail, h0, cbuf, fnw)


def _fox_sample_kernel(tbl_ref, q_ref, kn_ref, vn_ref, cnc_ref, cnr_ref, tri_ref, *refs, t_new):
    k_pages = refs[:PAGES_PER_STEP]
    v_pages = refs[PAGES_PER_STEP:2 * PAGES_PER_STEP]
    f_pages = refs[2 * PAGES_PER_STEP:3 * PAGES_PER_STEP]
    o_ref = refs[3 * PAGES_PER_STEP]
    qbd_scr, m_scr, l_scr, acc_scr, rel_scr = refs[3 * PAGES_PER_STEP + 1:]
    del tbl_ref
    c = pl.program_id(1)
    cn_col = cnc_ref[...]

    @pl.when(c == 0)
    def _():
        rel_scr[...] = jnp.zeros_like(rel_scr)
        q = q_ref[...].astype(F32)
        lane_head = lax.broadcasted_iota(jnp.int32, q.shape, 1) // HEAD_DIM
        zero = jnp.zeros_like(q)
        for h in range(N_HEADS):
            qbd_scr[h * t_new:(h + 1) * t_new, :] = jnp.where(lane_head == h, q, zero)
        m_scr[...] = jnp.full(m_scr.shape, NEG_INF, F32)
        l_scr[...] = jnp.zeros_like(l_scr)
        acc_scr[...] = jnp.zeros_like(acc_scr)

    qbd = qbd_scr[...].astype(BF16)

    def online(s, pv_fn):
        m = m_scr[...]
        m_new = jnp.maximum(m, jnp.max(s, axis=-1, keepdims=True))
        alpha = jnp.exp(m - m_new)
        p = jnp.exp(s - m_new)
        l_scr[...] = alpha * l_scr[...] + jnp.sum(p, axis=-1, keepdims=True)
        acc_scr[...] = alpha * acc_scr[...] + pv_fn(p.astype(BF16))
        m_scr[...] = m_new

    xs = jnp.concatenate([f_pages[i][...] for i in range(PAGES_PER_STEP)], axis=0)
    hi = xs.astype(BF16)
    rest = xs - hi.astype(F32)
    mid = rest.astype(BF16)
    lo = (rest - mid.astype(F32)).astype(BF16)
    tri = tri_ref[...]
    sums = (_mm(hi, tri) + _mm(mid, tri)) + _mm(lo, tri)
    after = rel_scr[...]
    bias = [None] * PAGES_PER_STEP
    for i in reversed(range(PAGES_PER_STEP)):
        rows = slice(i * N_HEADS, (i + 1) * N_HEADS)
        rel = sums[rows, :PAGE] + after
        after = after + sums[rows, PAGE:]
        bias[i] = jnp.concatenate(
            [jnp.broadcast_to(rel[h:h + 1, :], (t_new, PAGE)) for h in range(N_HEADS)], axis=0)
    rel_scr[...] = after

    kt = jnp.concatenate([k_pages[i][...].astype(BF16) for i in range(PAGES_PER_STEP)], axis=1)
    vt = jnp.concatenate([v_pages[i][...].astype(BF16) for i in range(PAGES_PER_STEP)], axis=1)
    s = (_mm(qbd, kt) + cn_col) + jnp.concatenate(bias, axis=1)
    online(s, lambda p: _mm_nt(p, vt))

    @pl.when(c == pl.num_programs(1) - 1)
    def _():
        zpad = jnp.zeros((PAGE - t_new, HEAD_W), F32)
        kn = jnp.concatenate([kn_ref[...], zpad], axis=0).astype(BF16)
        vn = jnp.concatenate([vn_ref[...], zpad], axis=0).astype(BF16)
        s = (_mm_nt(qbd, kn) + cn_col) - cnr_ref[...]
        t_query = lax.broadcasted_iota(jnp.int32, s.shape, 0) % t_new
        t_key = lax.broadcasted_iota(jnp.int32, s.shape, 1)
        s = jnp.where(t_query >= t_key, s, NEG_INF)
        online(s, lambda p: _mm(p, vn))
        o = acc_scr[...] / l_scr[...]
        lane_head = lax.broadcasted_iota(jnp.int32, (t_new, HEAD_W), 1) // HEAD_DIM
        out = jnp.zeros((t_new, HEAD_W), F32)
        for h in range(N_HEADS):
            out = jnp.where(lane_head == h, o[h * t_new:(h + 1) * t_new, :], out)
        o_ref[...] = out


def _fox_sample(q, k_new, v_new, cn_col, cn_rows, cache_kt, cache_vt, cache_ft, table, layer, *, t_new):
    rows = q.shape[0]
    b, n_pages = table.shape
    n_steps = n_pages // PAGES_PER_STEP
    n_rows = N_HEADS * t_new
    tok = pl.BlockSpec((t_new, HEAD_W), lambda bi, c, tbl: (bi, 0))

    def page_spec(i, height):
        return pl.BlockSpec(
            (None, None, height, PAGE),
            lambda bi, c, tbl: (layer, tbl[bi, (n_steps - 1 - c) * PAGES_PER_STEP + i], 0, 0))

    kv_pages = [page_spec(i, HEAD_W) for i in range(PAGES_PER_STEP)]
    f_pages = [page_spec(i, N_HEADS) for i in range(PAGES_PER_STEP)]
    pos = jnp.arange(PAGE)
    later = (pos[:, None] > pos[None, :]).astype(BF16)
    tri = jnp.concatenate([later, jnp.ones((PAGE, PAGE), BF16)], axis=1)
    return pl.pallas_call(
        functools.partial(_fox_sample_kernel, t_new=t_new),
        grid_spec=pltpu.PrefetchScalarGridSpec(
            num_scalar_prefetch=1,
            grid=(b, n_steps),
            in_specs=[tok, tok, tok,
                      pl.BlockSpec((None, n_rows, 1), lambda bi, c, tbl: (bi, 0, 0)),
                      pl.BlockSpec((None, n_rows, PAGE), lambda bi, c, tbl: (bi, 0, 0)),
                      pl.BlockSpec(tri.shape, lambda bi, c, tbl: (0, 0))]
                     + kv_pages + kv_pages + f_pages,
            out_specs=tok,
            scratch_shapes=[pltpu.VMEM((n_rows, HEAD_W), F32), pltpu.VMEM((n_rows, 1), F32),
                            pltpu.VMEM((n_rows, 1), F32), pltpu.VMEM((n_rows, HEAD_W), F32),
                            pltpu.VMEM((N_HEADS, PAGE), F32)]),
        out_shape=jax.ShapeDtypeStruct((rows, HEAD_W), F32),
        compiler_params=_params("parallel", "arbitrary"),
        name="fox_sample",
    )(table, q, k_new, v_new, cn_col, cn_rows, tri,
      *([cache_kt] * PAGES_PER_STEP), *([cache_vt] * PAGES_PER_STEP), *([cache_ft] * PAGES_PER_STEP))


def _rope_tables(pos):
    inv = ROPE_BASE ** (-jnp.arange(HALF_DIM, dtype=F32) / HALF_DIM)
    ang = pos[:, None] * inv[None, :]
    cos = jnp.cos(ang)
    sin = jnp.sin(ang)
    cos_head = jnp.concatenate([cos, cos], axis=-1)
    sin_head = jnp.concatenate([-sin, sin], axis=-1)
    return jnp.tile(cos_head, (1, N_HEADS)), jnp.tile(sin_head, (1, N_HEADS))


def _log_gamma_table(log_gamma):
    pairs = log_gamma.reshape(N_PAIRS, 2)
    ones = jnp.ones((N_PAIRS, 1, PAIR_W), F32)
    first = pairs[:, 0][:, None, None] * ones
    second = pairs[:, 1][:, None, None] * ones
    mix = jnp.repeat(pairs, HEAD_DIM, axis=1)[:, None, :]
    return jnp.concatenate([first, second, mix, jnp.zeros((N_PAIRS, 5, PAIR_W), F32)], axis=1)


def kernel(x_prompt, x_sample, cache_fox_k, cache_fox_v, cache_fox_logf, page_table, state_ret, state_lru_h, state_lru_conv, p_prompt, p_sample, norm_w, w_in_even, b_forget, ret_gn_w, w_out_even, w_in_odd, conv_w, conv_b, gate_r_w, gate_r_b, gate_i_w, gate_i_b, lru_lambda, w_out_odd, ple_proj, ple_norm_w, ple_gate_w, final_norm_w):
    bp, lp, d = x_prompt.shape
    bs, ls, _ = x_sample.shape
    depth = norm_w.shape[0]
    n_pool = cache_fox_k.shape[1]
    n_pages = page_table.shape[1]
    past_len = n_pages * PAGE
    lru_w = w_out_odd.shape[1]
    rows_p, rows_s = bp * lp, bs * ls
    tm_p = 256

    log_gamma = jnp.log1p(-jnp.exp2(-5.0 - jnp.arange(N_HEADS, dtype=F32)))
    lg_tab = _log_gamma_table(log_gamma)
    cos_p, sin_p = _rope_tables(jnp.arange(lp, dtype=F32))
    cos_s, sin_s = _rope_tables(past_len + jnp.arange(ls, dtype=F32))
    cos_s, sin_s = jnp.tile(cos_s, (bs, 1)), jnp.tile(sin_s, (bs, 1))

    cache_kt = cache_fox_k.transpose(0, 1, 3, 4, 2).reshape(-1, n_pool, HEAD_W, PAGE)
    cache_vt = cache_fox_v.transpose(0, 1, 3, 4, 2).reshape(-1, n_pool, HEAD_W, PAGE)
    cache_ft = jnp.swapaxes(cache_fox_logf, 2, 3)

    row2 = lambda v: v.reshape(1, -1)
    hp = x_prompt.reshape(rows_p, d)
    hs = x_sample.reshape(rows_s, d)
    outs = {k: [] for k in ("fl_p", "rs_p", "lh_p", "lc_p", "fk_s", "fv_s", "fl_s", "rs_s", "lh_s", "lc_s")}
    y_p = y_s = None
    n_even = w_in_even.shape[0]
    stacked_kv = None

    for i in range(depth):
        j = i // 2
        proj = ple_proj[i].astype(BF16)
        pgw = ple_gate_w[i].astype(BF16)
        pnw = row2(ple_norm_w[i])
        nw = row2(norm_w[i])
        if i % 2 == 0:
            w = w_in_even[j]
            c0 = 3 * HEAD_W
            wf = w[:, :c0].astype(BF16)
            wl = jnp.pad(w[:, c0:c0 + N_HEADS], ((0, 0), (0, PAGE - N_HEADS))).astype(BF16)
            wr = w[:, c0 + N_HEADS:2 * c0 + N_HEADS].astype(BF16)
            wg = w[:, 2 * c0 + N_HEADS:].astype(BF16)
            bf = jnp.pad(b_forget[j], (0, PAGE - N_HEADS)).reshape(1, PAGE)
            wo = w_out_even[j].astype(BF16)
            gn = row2(ret_gn_w[j])
            wt = jnp.swapaxes(w, 0, 1)
            wqkvt = wt[:c0].astype(BF16)
            wlt = jnp.pad(wt[c0:c0 + N_HEADS], ((0, N_HEADS), (0, 0))).astype(BF16)
            bfc = jnp.pad(b_forget[j], (0, N_HEADS)).reshape(2 * N_HEADS, 1)

            fqt, fkt_all, fvt_all, fkb, fvtb, lft, rq, rk, rv, g = _even_in_prompt(
                hp.reshape(bp, lp, d), nw, wqkvt, wlt, wr, wg, bfc, cos_p, sin_p, stacked_kv,
                layer=j, n_layers=n_even, tm=tm_p)
            stacked_kv = (fkt_all, fvt_all)
            cum_rows = _seg_cumsum(lft.reshape(bp * N_HEADS, lp), seg=lp).reshape(bp, N_PAIRS, 2, lp)
            cum_cols = jnp.swapaxes(cum_rows, 2, 3)
            fo = _fox_prompt(fqt, fkb, fvtb, cum_rows, cum_cols, tq=512, tk=512)
            ro, s_p = _ret_prompt(rq, rk, rv, lg_tab, gn)
            hp = _even_out(hp, fo.reshape(rows_p, HEAD_W), ro.reshape(rows_p, HEAD_W),
                           g.reshape(rows_p, 2 * HEAD_W), wo, p_prompt.reshape(depth, rows_p, -1),
                           proj, pnw, pgw, layer=i, tm=tm_p)
            outs["fl_p"].append(jnp.swapaxes(lft, 1, 2))
            outs["rs_p"].append(s_p)

            fq, fk, fv, lf, lft, rq, rk, rv, g = _even_in_sample(hs, nw, wf, wl, wr, wg, bf, cos_s, sin_s)
            cn = _seg_cumsum(lft, seg=ls).reshape(N_HEADS, bs, ls).transpose(1, 0, 2)
            cn_col = cn.reshape(bs, N_HEADS * ls, 1)
            cn_rows = jnp.pad(jnp.repeat(cn, ls, axis=1), ((0, 0), (0, 0), (0, PAGE - ls)))
            fo = _fox_sample(fq, fk, fv, cn_col, cn_rows, cache_kt, cache_vt, cache_ft, page_table, j, t_new=ls)
            ro, s_s = _ret_sample(rq, rk, rv, state_ret[j], lg_tab, gn, length=ls)
            hs = _even_out(hs, fo, ro, g, wo, p_sample.reshape(depth, rows_s, -1),
                           proj, pnw, pgw, layer=i, tm=rows_s)
            outs["fk_s"].append(fk.reshape(bs, ls, N_HEADS, HEAD_DIM))
            outs["fv_s"].append(fv.reshape(bs, ls, N_HEADS, HEAD_DIM))
            outs["fl_s"].append(lf.reshape(bs, ls, N_HEADS))
            outs["rs_s"].append(s_s)
        else:
            final = i == depth - 1
            gw = jnp.concatenate([gate_r_w[j], gate_i_w[j]], axis=-1).astype(BF16)
            args = (nw, w_in_odd[j].astype(BF16), conv_w[j], row2(conv_b[j]), gw, row2(gate_r_b[j]),
                    row2(gate_i_b[j]), row2(lru_lambda[j]), w_out_odd[j].astype(BF16))
            tail = (proj, pnw, pgw)
            fnw = row2(final_norm_w)
            o, lh, lc = _odd_layer(hp.reshape(bp, lp, d), *args, p_prompt, *tail,
                                   jnp.zeros((bp, lru_w), F32), jnp.zeros((CONV_W - 1, bp, lru_w), F32),
                                   fnw, layer=i, nb=SEQ_PER_BLOCK, tl=32, final_norm=final)
            hp = o.reshape(rows_p, d)
            outs["lh_p"].append(lh)
            outs["lc_p"].append(jnp.swapaxes(lc, 0, 1))
            o, lh, lc = _odd_layer(hs.reshape(bs, ls, d), *args, p_sample, *tail,
                                   state_lru_h[j], jnp.swapaxes(state_lru_conv[j], 0, 1),
                                   fnw, layer=i, nb=SEQ_PER_BLOCK, tl=ls, final_norm=final)
            hs = o.reshape(rows_s, d)
            outs["lh_s"].append(lh)
            outs["lc_s"].append(jnp.swapaxes(lc, 0, 1))
            if final:
                y_p, y_s = hp.reshape(bp, lp, d), hs.reshape(bs, ls, d)

    st = lambda k: jnp.stack(outs[k])
    to_blhd = lambda t: t.reshape(n_even, bp, N_HEADS, HEAD_DIM, lp).transpose(0, 1, 4, 2, 3)
    return (y_p, y_s, to_blhd(stacked_kv[0]), to_blhd(stacked_kv[1]), st("fl_p"), st("rs_p"), st("lh_p"), st("lc_p"),
            st("fk_s"), st("fv_s"), st("fl_s"), st("rs_s"), st("lh_s"), st("lc_s"))
```

```python
import functools
import math

import jax
import jax.numpy as jnp
from jax import lax
from jax.experimental import pallas as pl
from jax.experimental.pallas import tpu as pltpu

F32 = jnp.float32
BF16 = jnp.bfloat16

HEAD_DIM = 64
HALF_DIM = HEAD_DIM // 2
N_HEADS = 8
HEAD_W = N_HEADS * HEAD_DIM
PAIR_W = 2 * HEAD_DIM
N_PAIRS = N_HEADS // 2
PAGE = 128
PAGES_PER_STEP = 32
RET_CHUNK = 128
RET_GROUP = 8
ROPE_BASE = 10000.0
LRU_BLOCKS = 8
LRU_BW = 128
CONV_W = 4
SEQ_PER_BLOCK = 8
LRU_C = 8.0
NORM_EPS = 1e-6
NEG_INF = -1e30
QK_SCALE = HEAD_DIM ** -0.5
VMEM_LIMIT = 56 * 1024 * 1024


def _params(*sem):
    return pltpu.CompilerParams(dimension_semantics=sem, vmem_limit_bytes=VMEM_LIMIT)


def _mm(a, b):
    return jnp.dot(a, b, preferred_element_type=F32)


def _mm_nt(a, b):
    return lax.dot_general(a, b, (((1,), (1,)), ((), ())), preferred_element_type=F32)


def _rms(x, w):
    return (x * lax.rsqrt(jnp.mean(x * x, axis=-1, keepdims=True) + NORM_EPS)) * w


def _const_spec(shape):
    nd = len(shape)
    return pl.BlockSpec(shape, lambda *_: (0,) * nd)


def _swap_halves(x):
    n = x.shape[-1]
    lane = lax.broadcasted_iota(jnp.int32, x.shape, x.ndim - 1)
    return jnp.where(lane % HEAD_DIM < HALF_DIM,
                     pltpu.roll(x, n - HALF_DIM, x.ndim - 1),
                     pltpu.roll(x, HALF_DIM, x.ndim - 1))


def _retention_gate_proj(ub, wr_ref, wg_ref, cos_ref, sin_ref, rq_ref, rk_ref, rv_ref, g_ref):
    zr = _mm(ub, wr_ref[...])
    cos = cos_ref[...]
    sin = sin_ref[...]
    rq = zr[:, :HEAD_W]
    rk = zr[:, HEAD_W:2 * HEAD_W]
    rq_ref[...] = (rq * cos + _swap_halves(rq) * sin).astype(BF16)
    rk_ref[...] = (rk * cos + _swap_halves(rk) * sin) * QK_SCALE
    rv_ref[...] = zr[:, 2 * HEAD_W:].astype(BF16)
    g_ref[...] = _mm(ub, wg_ref[...])


N_EVEN_IN_PROMPT_INPUTS = 9


def _even_in_prompt_kernel(*refs):
    h_ref, nw_ref, wqkvt_ref, wlt_ref, wr_ref, wg_ref, bfc_ref, cos_ref, sin_ref = refs[:N_EVEN_IN_PROMPT_INPUTS]
    (fqt_ref, fkt_ref, fvt_ref, fkb_ref, fvtb_ref, lft_ref, rq_ref, rk_ref, rv_ref, g_ref) = refs[-10:]
    ub = _rms(h_ref[...], nw_ref[...]).astype(BF16)
    zt = _mm_nt(wqkvt_ref[...], ub)
    fqt_ref[...] = (zt[:HEAD_W] * QK_SCALE).astype(BF16)
    fkt = zt[HEAD_W:2 * HEAD_W]
    fkt_ref[...] = fkt
    fkb_ref[...] = fkt.T.astype(BF16)
    fvt = zt[2 * HEAD_W:]
    fvt_ref[...] = fvt
    fvtb_ref[...] = fvt.astype(BF16)
    lft_ref[...] = jax.nn.log_sigmoid(_mm_nt(wlt_ref[...], ub) + bfc_ref[...])[:N_HEADS]
    _retention_gate_proj(ub, wr_ref, wg_ref, cos_ref, sin_ref, rq_ref, rk_ref, rv_ref, g_ref)


def _even_in_prompt(h, nw, wqkvt, wlt, wr, wg, bfc, cos, sin, stacked_kv, *, layer, n_layers, tm):
    b, l, d = h.shape
    tok = lambda w: pl.BlockSpec((None, tm, w), lambda bi, i: (bi, i, 0))
    feat = lambda r: pl.BlockSpec((None, r, tm), lambda bi, i: (bi, 0, i))
    stacked = pl.BlockSpec((None, None, HEAD_W, tm), lambda bi, i: (layer, bi, 0, i))
    pos = pl.BlockSpec((tm, HEAD_W), lambda bi, i: (i, 0))
    sd = jax.ShapeDtypeStruct
    consts = [nw, wqkvt, wlt, wr, wg, bfc]
    ins = [h, *consts, cos, sin]
    in_specs = [tok(d)] + [_const_spec(c.shape) for c in consts] + [pos, pos]
    assert len(ins) == N_EVEN_IN_PROMPT_INPUTS
    aliases = {}
    if stacked_kv is not None:
        aliases = {len(ins): 1, len(ins) + 1: 2}
        ins += list(stacked_kv)
        in_specs += [pl.BlockSpec(memory_space=pl.ANY)] * 2
    return pl.pallas_call(
        _even_in_prompt_kernel,
        grid=(b, l // tm),
        in_specs=in_specs,
        out_specs=[feat(HEAD_W), stacked, stacked, tok(HEAD_W), feat(HEAD_W), feat(N_HEADS),
                   tok(HEAD_W), tok(HEAD_W), tok(HEAD_W), tok(2 * HEAD_W)],
        out_shape=[sd((b, HEAD_W, l), BF16), sd((n_layers, b, HEAD_W, l), F32), sd((n_layers, b, HEAD_W, l), F32),
                   sd((b, l, HEAD_W), BF16), sd((b, HEAD_W, l), BF16), sd((b, N_HEADS, l), F32),
                   sd((b, l, HEAD_W), BF16), sd((b, l, HEAD_W), F32), sd((b, l, HEAD_W), BF16),
                   sd((b, l, 2 * HEAD_W), F32)],
        input_output_aliases=aliases,
        compiler_params=_params("parallel", "parallel"),
        name="even_in_prompt",
    )(*ins)


def _even_in_sample_kernel(h_ref, nw_ref, wf_ref, wl_ref, wr_ref, wg_ref, bf_ref, cos_ref, sin_ref,
                           fq_ref, fk_ref, fv_ref, lf_ref, lft_ref, rq_ref, rk_ref, rv_ref, g_ref):
    ub = _rms(h_ref[...], nw_ref[...]).astype(BF16)
    zf = _mm(ub, wf_ref[...])
    fq_ref[...] = (zf[:, :HEAD_W] * QK_SCALE).astype(BF16)
    fk_ref[...] = zf[:, HEAD_W:2 * HEAD_W]
    fv_ref[...] = zf[:, 2 * HEAD_W:]
    logf = jax.nn.log_sigmoid(_mm(ub, wl_ref[...]) + bf_ref[...])
    lf_ref[...] = logf[:, :N_HEADS]
    lft_ref[...] = logf.T[:N_HEADS]
    _retention_gate_proj(ub, wr_ref, wg_ref, cos_ref, sin_ref, rq_ref, rk_ref, rv_ref, g_ref)


def _even_in_sample(h, nw, wf, wl, wr, wg, bf, cos, sin):
    rows = h.shape[0]
    sd = jax.ShapeDtypeStruct
    ins = [h, nw, wf, wl, wr, wg, bf, cos, sin]
    outs = [sd((rows, HEAD_W), BF16), sd((rows, HEAD_W), F32), sd((rows, HEAD_W), F32),
            sd((rows, N_HEADS), F32), sd((N_HEADS, rows), F32),
            sd((rows, HEAD_W), BF16), sd((rows, HEAD_W), F32), sd((rows, HEAD_W), BF16),
            sd((rows, 2 * HEAD_W), F32)]
    return pl.pallas_call(
        _even_in_sample_kernel,
        grid=(1,),
        in_specs=[_const_spec(a.shape) for a in ins],
        out_specs=[_const_spec(o.shape) for o in outs],
        out_shape=outs,
        compiler_params=_params("arbitrary"),
        name="even_in_sample",
    )(*ins)


def _seg_cumsum_kernel(x_ref, o_ref, *, seg):
    x = x_ref[...]
    pos = lax.broadcasted_iota(jnp.int32, x.shape, 1) % seg
    s = 1
    while s < seg:
        x = x + jnp.where(pos >= s, pltpu.roll(x, s, 1), 0.0)
        s *= 2
    o_ref[...] = x


def _seg_cumsum(x, *, seg):
    r, n = x.shape
    return pl.pallas_call(
        functools.partial(_seg_cumsum_kernel, seg=seg),
        grid=(r // 8,),
        in_specs=[pl.BlockSpec((8, n), lambda i: (i, 0))],
        out_specs=pl.BlockSpec((8, n), lambda i: (i, 0)),
        out_shape=jax.ShapeDtypeStruct((r, n), F32),
        compiler_params=_params("parallel"),
        name="seg_cumsum",
    )(x)


def _pair_masks(shape):
    lane = lax.broadcasted_iota(jnp.int32, shape, len(shape) - 1)
    return lane < HEAD_DIM


def _split3(x):
    hi = x.astype(BF16).astype(F32)
    rest = x - hi
    mid = rest.astype(BF16).astype(F32)
    return hi, mid, rest - mid


AUG_ROWS = 8
SUM_ROWS = 16


def _fox_prompt_kernel(qt_ref, k_ref, vt_ref, cr_ref, call_ref, o_ref, kaug_scr, *, tq, tk):
    qi = pl.program_id(2)
    seq = k_ref.shape[0]

    @pl.when(qi == 0)
    def _():
        c_all = call_ref[...]
        one = jnp.ones((1, seq), F32)
        zero_row = jnp.zeros((1, seq), F32)
        rows = []
        for h in range(2):
            rows += list(_split3(-c_all[h:h + 1, :])) + [one] * 3 + [zero_row] * (AUG_ROWS - 6)
        aug_t = jnp.concatenate(rows + [jnp.zeros((PAIR_W - 2 * AUG_ROWS, seq), F32)], axis=0)
        kaug_scr[:, :PAIR_W] = k_ref[...]
        kaug_scr[:, PAIR_W:] = aug_t.T.astype(BF16)

    qt = qt_ref[...]
    row = lax.broadcasted_iota(jnp.int32, qt.shape, 0)
    zero = jnp.zeros_like(qt)
    cr = cr_ref[...]
    w_heads = []
    for h in range(2):
        head_rows = (row < HEAD_DIM) if h == 0 else (row >= HEAD_DIM)
        pieces = _split3(cr[h:h + 1, :])
        bias = jnp.zeros(qt.shape, F32)
        for i in range(3):
            bias = jnp.where(row == h * AUG_ROWS + i, 1.0, bias)
            bias = jnp.where(row == h * AUG_ROWS + 3 + i, pieces[i], bias)
        w_heads.append(jnp.concatenate([jnp.where(head_rows, qt, zero), bias.astype(BF16)], axis=0))
    key_l = lax.broadcasted_iota(jnp.int32, (tk, tq), 0)
    qry_l = lax.broadcasted_iota(jnp.int32, (tk, tq), 1)
    ones = jnp.ones((SUM_ROWS, tk), BF16)

    def scores_of(j):
        kc = kaug_scr[j * tk:(j + 1) * tk, :]
        return tuple(_mm(kc, w_heads[h]) for h in range(2))

    def values_of(j, p_heads):
        start = j * tk
        return [_mm(jnp.concatenate([vt_ref[h * HEAD_DIM:(h + 1) * HEAD_DIM, pl.ds(start, tk)], ones], axis=0),
                    p_heads[h]) for h in range(2)]

    def softmax(s, m, diagonal):
        if diagonal:
            s = jnp.where(key_l <= qry_l, s, NEG_INF)
        m_new = jnp.maximum(m, jnp.max(s, axis=0, keepdims=True))
        return m_new, jnp.exp(m - m_new), jnp.exp(s - m_new).astype(BF16)

    def run(n_blocks):
        m_heads = [jnp.full((1, tq), NEG_INF, F32)] * 2
        acc_heads = [jnp.zeros((HEAD_DIM + SUM_ROWS, tq), F32)] * 2
        s_cur = scores_of(0)
        for j in range(n_blocks):
            s_next = scores_of(j + 1) if j + 1 < n_blocks else None
            p_cur, alphas = [], []
            for h in range(2):
                m_heads[h], alpha, p = softmax(s_cur[h], m_heads[h], j + 1 == n_blocks)
                alphas.append(alpha)
                p_cur.append(p)
            pv = values_of(j, p_cur)
            acc_heads = [alphas[h] * acc_heads[h] + pv[h] for h in range(2)]
            s_cur = s_next
        out = [acc[:HEAD_DIM, :] / acc[HEAD_DIM:HEAD_DIM + 1, :] for acc in acc_heads]
        o_ref[...] = jnp.concatenate(out, axis=0).T

    for q_block in range(seq // tq):
        pl.when(qi == q_block)(functools.partial(run, q_block + 1))


def _fox_prompt(qt, kb, vtb, cum_rows, *, tq, tk):
    b, l, _ = kb.shape
    assert tq == tk, "the diagonal key block must coincide with the query block"
    return pl.pallas_call(
        functools.partial(_fox_prompt_kernel, tq=tq, tk=tk),
        grid=(b, N_PAIRS, l // tq),
        in_specs=[pl.BlockSpec((None, PAIR_W, tq), lambda bi, hp, qi: (bi, hp, qi)),
                  pl.BlockSpec((None, l, PAIR_W), lambda bi, hp, qi: (bi, 0, hp)),
                  pl.BlockSpec((None, PAIR_W, l), lambda bi, hp, qi: (bi, hp, 0)),
                  pl.BlockSpec((None, None, 2, tq), lambda bi, hp, qi: (bi, hp, 0, qi)),
                  pl.BlockSpec((None, None, 2, l), lambda bi, hp, qi: (bi, hp, 0, 0))],
        out_specs=pl.BlockSpec((None, tq, PAIR_W), lambda bi, hp, qi: (bi, qi, hp)),
        out_shape=jax.ShapeDtypeStruct((b, l, HEAD_W), F32),
        scratch_shapes=[pltpu.VMEM((l, 2 * PAIR_W), BF16)],
        compiler_params=_params("parallel", "parallel", "arbitrary"),
        name="fox_prompt",
    )(qt, kb, vtb, cum_rows, cum_rows)


def _ret_tables(lg_ref, length):
    n = RET_CHUNK
    lg_a = lg_ref[0:1, :]
    lg_b = lg_ref[1:2, :]
    lg_m = lg_ref[2:3, :]
    row = lax.broadcasted_iota(jnp.int32, (n, n), 0)
    col = lax.broadcasted_iota(jnp.int32, (n, n), 1)
    diff = row - col
    lower = diff >= 0
    dpos = jnp.where(lower, diff, 0).astype(F32)
    decay = (jnp.where(lower, jnp.exp(dpos * lg_a), 0.0), jnp.where(lower, jnp.exp(dpos * lg_b), 0.0))
    rowf = row.astype(F32)
    q_dec = jnp.exp((rowf + 1.0) * lg_m)
    k_dec = jnp.exp((length - 1.0 - rowf) * lg_m)
    s_dec = jnp.exp(float(length) * jnp.where(row < HEAD_DIM, lg_a, lg_b))
    same_head = (row < HEAD_DIM) == (col < HEAD_DIM)
    return decay, q_dec, k_dec, s_dec, same_head


def _ret_chunk(q, k, v, state, tables):
    decay, q_dec, k_dec, s_dec, same_head = tables
    first = _pair_masks(q.shape)
    zero = jnp.zeros_like(q)
    kb = k.astype(BF16)
    o_heads = []
    for h, qh in enumerate((jnp.where(first, q, zero), jnp.where(first, zero, q))):
        scores = _mm_nt(qh, kb) * decay[h]
        o_heads.append(_mm(scores.astype(BF16), v))
    o = jnp.where(first, o_heads[0], o_heads[1])
    o = o + _mm(q, state.astype(BF16)) * q_dec
    kd_t = (k * k_dec).T.astype(BF16)
    state = state * s_dec + jnp.where(same_head, _mm(kd_t, v), 0.0)
    return o, state


def _group_norm(o, gn_w):
    first = _pair_masks(o.shape)
    inv = 1.0 / HEAD_DIM
    s0 = jnp.sum(jnp.where(first, o, 0.0), axis=-1, keepdims=True)
    s1 = jnp.sum(jnp.where(first, 0.0, o), axis=-1, keepdims=True)
    d = o - jnp.where(first, s0, s1) * inv
    dd = d * d
    v0 = jnp.sum(jnp.where(first, dd, 0.0), axis=-1, keepdims=True)
    v1 = jnp.sum(jnp.where(first, 0.0, dd), axis=-1, keepdims=True)
    var = jnp.where(first, v0, v1) * inv
    return (d * lax.rsqrt(var + NORM_EPS)) * gn_w


def _ret_prompt_kernel(q_ref, k_ref, v_ref, lg_ref, gn_ref, o_ref, s_ref, *, n_chunks):
    tables = _ret_tables(lg_ref, RET_CHUNK)
    gn_w = gn_ref[...]

    decay, q_dec, k_dec, s_dec, same_head = tables
    group = RET_GROUP

    def body(cc, state):
        rows = [pl.ds(pl.multiple_of((cc * group + g) * RET_CHUNK, RET_CHUNK), RET_CHUNK)
                for g in range(group)]
        qs = [q_ref[r, :] for r in rows]
        ks = [k_ref[r, :] for r in rows]
        vs = [v_ref[r, :] for r in rows]
        first = _pair_masks(qs[0].shape)
        zero = jnp.zeros_like(qs[0])
        kbs = [k.astype(BF16) for k in ks]
        scores = [[_mm_nt(jnp.where(first, q, zero) if h == 0 else jnp.where(first, zero, q), kb)
                   for h in range(2)] for q, kb in zip(qs, kbs)]
        incs = [jnp.where(same_head, _mm((k * k_dec).T.astype(BF16), v), 0.0) for k, v in zip(ks, vs)]
        intra = [[_mm((sc[h] * decay[h]).astype(BF16), v) for h in range(2)] for sc, v in zip(scores, vs)]
        states = []
        for g in range(group):
            states.append(state)
            state = state * s_dec + incs[g]
        cross = [_mm(q, s.astype(BF16)) * q_dec for q, s in zip(qs, states)]
        for g in range(group):
            o = jnp.where(first, intra[g][0], intra[g][1]) + cross[g]
            o_ref[rows[g], :] = _group_norm(o, gn_w)
        return state

    state = lax.fori_loop(0, n_chunks // group, body, jnp.zeros((PAIR_W, PAIR_W), F32))
    s_ref[0] = state[:HEAD_DIM, :HEAD_DIM]
    s_ref[1] = state[HEAD_DIM:, HEAD_DIM:]


def _ret_prompt(rq, rk, rv, lg_tab, gn_w):
    b, l, _ = rq.shape
    seq = pl.BlockSpec((None, l, PAIR_W), lambda bi, hp: (bi, 0, hp))
    return pl.pallas_call(
        functools.partial(_ret_prompt_kernel, n_chunks=l // RET_CHUNK),
        grid=(b, N_PAIRS),
        in_specs=[seq, seq, seq,
                  pl.BlockSpec((None, 8, PAIR_W), lambda bi, hp: (hp, 0, 0)),
                  pl.BlockSpec((1, PAIR_W), lambda bi, hp: (0, hp))],
        out_specs=[seq, pl.BlockSpec((None, 2, HEAD_DIM, HEAD_DIM), lambda bi, hp: (bi, hp, 0, 0))],
        out_shape=[jax.ShapeDtypeStruct((b, l, HEAD_W), F32),
                   jax.ShapeDtypeStruct((b, N_HEADS, HEAD_DIM, HEAD_DIM), F32)],
        compiler_params=_params("parallel", "parallel"),
        name="ret_prompt",
    )(rq, rk, rv, lg_tab, gn_w)


def _ret_sample_kernel(q_ref, k_ref, v_ref, s0_ref, lg_ref, gn_ref, o_ref, s_ref, *, length):
    pad = RET_CHUNK - length
    zrow_f = jnp.zeros((pad, PAIR_W), F32)
    zblk = jnp.zeros((HEAD_DIM, HEAD_DIM), F32)
    for hp in range(N_PAIRS):
        lanes = slice(hp * PAIR_W, (hp + 1) * PAIR_W)
        tables = _ret_tables(lg_ref.at[hp], length)
        q = jnp.concatenate([q_ref[:, lanes].astype(F32), zrow_f], axis=0).astype(BF16)
        k = jnp.concatenate([k_ref[:, lanes], zrow_f], axis=0)
        v = jnp.concatenate([v_ref[:, lanes].astype(F32), zrow_f], axis=0).astype(BF16)
        state = jnp.concatenate(
            [jnp.concatenate([s0_ref[2 * hp], zblk], axis=1),
             jnp.concatenate([zblk, s0_ref[2 * hp + 1]], axis=1)], axis=0)
        o, state = _ret_chunk(q, k, v, state, tables)
        o_ref[:, lanes] = _group_norm(o[:length], gn_ref[:, lanes])
        s_ref[2 * hp] = state[:HEAD_DIM, :HEAD_DIM]
        s_ref[2 * hp + 1] = state[HEAD_DIM:, HEAD_DIM:]


def _ret_sample(rq, rk, rv, state, lg_tab, gn_w, *, layer, length):
    rows = rq.shape[0]
    b = rows // length
    tok = pl.BlockSpec((length, HEAD_W), lambda bi: (bi, 0))
    st_in = pl.BlockSpec((None, None, N_HEADS, HEAD_DIM, HEAD_DIM), lambda bi: (layer, bi, 0, 0, 0))
    st = pl.BlockSpec((None, N_HEADS, HEAD_DIM, HEAD_DIM), lambda bi: (bi, 0, 0, 0))
    return pl.pallas_call(
        functools.partial(_ret_sample_kernel, length=length),
        grid=(b,),
        in_specs=[tok, tok, tok, st_in, _const_spec(lg_tab.shape), _const_spec(gn_w.shape)],
        out_specs=[tok, st],
        out_shape=[jax.ShapeDtypeStruct((rows, HEAD_W), F32),
                   jax.ShapeDtypeStruct(state.shape[1:], F32)],
        compiler_params=_params("parallel"),
        name="ret_sample",
    )(rq, rk, rv, state, lg_tab, gn_w)


def _ple(h1, p, proj, pnw, pgw):
    e = _rms(_mm(p.astype(BF16), proj), pnw)
    return h1 + jax.nn.sigmoid(_mm(h1.astype(BF16), pgw)) * e


def _silu(g):
    return g * jax.nn.sigmoid(g)


def _even_out_kernel(h_ref, fo_ref, ro_ref, g_ref, wo_ref, p_ref, proj_ref, pnw_ref, pgw_ref, o_ref):
    sg = _silu(g_ref[...])
    cat_f = (fo_ref[...] * sg[:, :HEAD_W]).astype(BF16)
    cat_r = (ro_ref[...] * sg[:, HEAD_W:]).astype(BF16)
    mix = _mm(cat_f, wo_ref[:HEAD_W, :]) + _mm(cat_r, wo_ref[HEAD_W:, :])
    o_ref[...] = _ple(h_ref[...] + mix, p_ref[...], proj_ref[...], pnw_ref[...], pgw_ref[...])


def _even_out(h, fo, ro, g, wo, p, proj, pnw, pgw, *, layer, tm):
    rows, d = h.shape
    row = lambda w: pl.BlockSpec((tm, w), lambda i: (i, 0))
    return pl.pallas_call(
        _even_out_kernel,
        grid=(rows // tm,),
        in_specs=[row(d), row(HEAD_W), row(HEAD_W), row(2 * HEAD_W), _const_spec(wo.shape),
                  pl.BlockSpec((None, tm, p.shape[-1]), lambda i: (layer, i, 0)),
                  _const_spec(proj.shape), _const_spec(pnw.shape), _const_spec(pgw.shape)],
        out_specs=row(d),
        out_shape=jax.ShapeDtypeStruct((rows, d), F32),
        compiler_params=_params("parallel"),
        name="even_out",
    )(h, fo, ro, g, wo, p, proj, pnw, pgw)


def _odd_kernel(h_ref, nw_ref, win_ref, cw_ref, cb_ref, gw_ref, grb_ref, gib_ref, lam_ref, wout_ref,
                p_ref, proj_ref, pnw_ref, pgw_ref, h0_ref, cbuf_ref, fnw_ref,
                o_ref, lruh_ref, lruc_ref,
                ext_scr, carry_scr, y_scr, sg_scr, *, tl, final_norm):
    nb, _, d = h_ref.shape
    w = wout_ref.shape[0]
    rows = nb * tl
    halo = (CONV_W - 1) * nb

    @pl.when(pl.program_id(1) == 0)
    def _():
        carry_scr[...] = h0_ref[...]
        ext_scr[0:halo, :] = cbuf_ref[...].reshape(halo, w)

    x = jnp.swapaxes(h_ref[...], 0, 1).reshape(rows, d)
    z = _mm(_rms(x, nw_ref[...]).astype(BF16), win_ref[...])
    xb = z[:, :w]
    sg_scr[...] = _silu(z[:, w:])
    ext_scr[halo:halo + rows, :] = xb
    xc = xb * cw_ref[CONV_W - 1:CONV_W, :] + cb_ref[...]
    for kk in range(1, CONV_W):
        xc = xc + ext_scr[halo - kk * nb:halo - kk * nb + rows, :] * cw_ref[CONV_W - 1 - kk:CONV_W - kk, :]
    last_steps = ext_scr[rows:rows + halo, :]
    lruc_ref[...] = last_steps.reshape(CONV_W - 1, nb, w)
    ext_scr[0:halo, :] = last_steps

    sp = jax.nn.softplus(-lam_ref[...])
    for n in range(LRU_BLOCKS):
        lanes = slice(n * LRU_BW, (n + 1) * LRU_BW)
        xn = xc[:, lanes]
        zz = _mm(xn.astype(BF16), gw_ref[n])
        r = jax.nn.sigmoid(zz[:, :LRU_BW] + grb_ref[:, lanes])
        gi = jax.nn.sigmoid(zz[:, LRU_BW:] + gib_ref[:, lanes])
        log_a = (-LRU_C * r) * sp[:, lanes]
        a = jnp.exp(log_a)
        one_minus = -jnp.tanh(log_a) * (a * a + 1.0)
        b = (jnp.sqrt(one_minus) * gi) * xn
        hs = carry_scr[:, lanes]
        for t in range(tl):
            step = slice(t * nb, (t + 1) * nb)
            hs = a[step] * hs + b[step]
            y_scr[step, lanes] = hs
        carry_scr[:, lanes] = hs
        lruh_ref[:, lanes] = hs

    mix = _mm((y_scr[...] * sg_scr[...]).astype(BF16), wout_ref[...])
    p = jnp.swapaxes(p_ref[...], 0, 1).reshape(rows, -1)
    h2 = _ple(x + mix, p, proj_ref[...], pnw_ref[...], pgw_ref[...])
    if final_norm:
        h2 = _rms(h2, fnw_ref[...])
    o_ref[...] = jnp.swapaxes(h2.reshape(tl, nb, d), 0, 1)


def _odd_layer(h, nw, win, cw, cb, gw, grb, gib, lam, wout, p, proj, pnw, pgw, h0, cbuf, fnw,
               *, layer, nb, tl, final_norm):
    b, l, d = h.shape
    w = wout.shape[0]
    seq = lambda width: pl.BlockSpec((nb, tl, width), lambda bi, li: (bi, li, 0))
    state = pl.BlockSpec((nb, w), lambda bi, li: (bi, 0))
    conv = pl.BlockSpec((CONV_W - 1, nb, w), lambda bi, li: (0, bi, 0))
    consts = [nw, win, cw, cb, gw, grb, gib, lam, wout]
    tail = [proj, pnw, pgw]
    rows = nb * tl
    return pl.pallas_call(
        functools.partial(_odd_kernel, tl=tl, final_norm=final_norm),
        grid=(b // nb, l // tl),
        in_specs=[seq(d)] + [_const_spec(c.shape) for c in consts]
                 + [pl.BlockSpec((None, nb, tl, p.shape[-1]), lambda bi, li: (layer, bi, li, 0))]
                 + [_const_spec(c.shape) for c in tail] + [state, conv, _const_spec(fnw.shape)],
        out_specs=[seq(d), state, conv],
        out_shape=[jax.ShapeDtypeStruct((b, l, d), F32),
                   jax.ShapeDtypeStruct((b, w), F32),
                   jax.ShapeDtypeStruct((CONV_W - 1, b, w), F32)],
        scratch_shapes=[pltpu.VMEM(((CONV_W - 1) * nb + rows, w), F32), pltpu.VMEM((nb, w), F32),
                        pltpu.VMEM((rows, w), F32), pltpu.VMEM((rows, w), F32)],
        compiler_params=_params("parallel", "arbitrary"),
        name="odd_layer",
    )(h, *consts, p, *tail, h0, cbuf, fnw)


def _fox_sample_kernel(tbl_ref, q_ref, kn_ref, vn_ref, cnc_ref, cnr_ref, tri_ref, *refs, t_new):
    k_pages = refs[:PAGES_PER_STEP]
    v_pages = refs[PAGES_PER_STEP:2 * PAGES_PER_STEP]
    f_pages = refs[2 * PAGES_PER_STEP:3 * PAGES_PER_STEP]
    o_ref = refs[3 * PAGES_PER_STEP]
    qbd_scr, m_scr, l_scr, acc_scr, rel_scr = refs[3 * PAGES_PER_STEP + 1:]
    del tbl_ref
    c = pl.program_id(1)
    cn_col = cnc_ref[...]

    @pl.when(c == 0)
    def _():
        rel_scr[...] = jnp.zeros_like(rel_scr)
        q = q_ref[...].astype(F32)
        lane_head = lax.broadcasted_iota(jnp.int32, q.shape, 1) // HEAD_DIM
        zero = jnp.zeros_like(q)
        for h in range(N_HEADS):
            qbd_scr[h * t_new:(h + 1) * t_new, :] = jnp.where(lane_head == h, q, zero)
        m_scr[...] = jnp.full(m_scr.shape, NEG_INF, F32)
        l_scr[...] = jnp.zeros_like(l_scr)
        acc_scr[...] = jnp.zeros_like(acc_scr)

    qbd = qbd_scr[...].astype(BF16)

    def online(s, pv_fn):
        m = m_scr[...]
        m_new = jnp.maximum(m, jnp.max(s, axis=-1, keepdims=True))
        alpha = jnp.exp(m - m_new)
        p = jnp.exp(s - m_new)
        l_scr[...] = alpha * l_scr[...] + jnp.sum(p, axis=-1, keepdims=True)
        acc_scr[...] = alpha * acc_scr[...] + pv_fn(p.astype(BF16))
        m_scr[...] = m_new

    xs = jnp.concatenate([f_pages[i][...] for i in range(PAGES_PER_STEP)], axis=0)
    hi = xs.astype(BF16)
    rest = xs - hi.astype(F32)
    mid = rest.astype(BF16)
    lo = (rest - mid.astype(F32)).astype(BF16)
    tri = tri_ref[...]
    sums = (_mm(hi, tri) + _mm(mid, tri)) + _mm(lo, tri)
    after = rel_scr[...]
    bias = [None] * PAGES_PER_STEP
    for i in reversed(range(PAGES_PER_STEP)):
        rows = slice(i * N_HEADS, (i + 1) * N_HEADS)
        rel = sums[rows, :PAGE] + after
        after = after + sums[rows, PAGE:]
        bias[i] = jnp.concatenate(
            [jnp.broadcast_to(rel[h:h + 1, :], (t_new, PAGE)) for h in range(N_HEADS)], axis=0)
    rel_scr[...] = after

    kt = jnp.concatenate([k_pages[i][...].astype(BF16) for i in range(PAGES_PER_STEP)], axis=1)
    vt = jnp.concatenate([v_pages[i][...].astype(BF16) for i in range(PAGES_PER_STEP)], axis=1)
    s = (_mm(qbd, kt) + cn_col) + jnp.concatenate(bias, axis=1)
    online(s, lambda p: _mm_nt(p, vt))

    @pl.when(c == pl.num_programs(1) - 1)
    def _():
        zpad = jnp.zeros((PAGE - t_new, HEAD_W), F32)
        kn = jnp.concatenate([kn_ref[...], zpad], axis=0).astype(BF16)
        vn = jnp.concatenate([vn_ref[...], zpad], axis=0).astype(BF16)
        s = (_mm_nt(qbd, kn) + cn_col) - cnr_ref[...]
        t_query = lax.broadcasted_iota(jnp.int32, s.shape, 0) % t_new
        t_key = lax.broadcasted_iota(jnp.int32, s.shape, 1)
        s = jnp.where(t_query >= t_key, s, NEG_INF)
        online(s, lambda p: _mm(p, vn))
        o = acc_scr[...] / l_scr[...]
        lane_head = lax.broadcasted_iota(jnp.int32, (t_new, HEAD_W), 1) // HEAD_DIM
        out = jnp.zeros((t_new, HEAD_W), F32)
        for h in range(N_HEADS):
            out = jnp.where(lane_head == h, o[h * t_new:(h + 1) * t_new, :], out)
        o_ref[...] = out


def _fox_sample(q, k_new, v_new, cn_col, cn_rows, cache_kt, cache_vt, cache_ft, table, layer, *, t_new):
    rows = q.shape[0]
    b, n_pages = table.shape
    n_steps = n_pages // PAGES_PER_STEP
    n_rows = N_HEADS * t_new
    tok = pl.BlockSpec((t_new, HEAD_W), lambda bi, c, tbl: (bi, 0))

    def page_spec(i, height):
        return pl.BlockSpec(
            (None, None, height, PAGE),
            lambda bi, c, tbl: (layer, tbl[bi, (n_steps - 1 - c) * PAGES_PER_STEP + i], 0, 0))

    kv_pages = [page_spec(i, HEAD_W) for i in range(PAGES_PER_STEP)]
    f_pages = [page_spec(i, N_HEADS) for i in range(PAGES_PER_STEP)]
    pos = jnp.arange(PAGE)
    later = (pos[:, None] > pos[None, :]).astype(BF16)
    tri = jnp.concatenate([later, jnp.ones((PAGE, PAGE), BF16)], axis=1)
    return pl.pallas_call(
        functools.partial(_fox_sample_kernel, t_new=t_new),
        grid_spec=pltpu.PrefetchScalarGridSpec(
            num_scalar_prefetch=1,
            grid=(b, n_steps),
            in_specs=[tok, tok, tok,
                      pl.BlockSpec((None, n_rows, 1), lambda bi, c, tbl: (bi, 0, 0)),
                      pl.BlockSpec((None, n_rows, PAGE), lambda bi, c, tbl: (bi, 0, 0)),
                      pl.BlockSpec(tri.shape, lambda bi, c, tbl: (0, 0))]
                     + kv_pages + kv_pages + f_pages,
            out_specs=tok,
            scratch_shapes=[pltpu.VMEM((n_rows, HEAD_W), F32), pltpu.VMEM((n_rows, 1), F32),
                            pltpu.VMEM((n_rows, 1), F32), pltpu.VMEM((n_rows, HEAD_W), F32),
                            pltpu.VMEM((N_HEADS, PAGE), F32)]),
        out_shape=jax.ShapeDtypeStruct((rows, HEAD_W), F32),
        compiler_params=_params("parallel", "arbitrary"),
        name="fox_sample",
    )(table, q, k_new, v_new, cn_col, cn_rows, tri,
      *([cache_kt] * PAGES_PER_STEP), *([cache_vt] * PAGES_PER_STEP), *([cache_ft] * PAGES_PER_STEP))


def _rope_tables(pos):
    inv = ROPE_BASE ** (-jnp.arange(HALF_DIM, dtype=F32) / HALF_DIM)
    ang = pos[:, None] * inv[None, :]
    cos = jnp.cos(ang)
    sin = jnp.sin(ang)
    cos_head = jnp.concatenate([cos, cos], axis=-1)
    sin_head = jnp.concatenate([-sin, sin], axis=-1)
    return jnp.tile(cos_head, (1, N_HEADS)), jnp.tile(sin_head, (1, N_HEADS))


def _log_gamma_table(log_gamma):
    pairs = log_gamma.reshape(N_PAIRS, 2)
    ones = jnp.ones((N_PAIRS, 1, PAIR_W), F32)
    first = pairs[:, 0][:, None, None] * ones
    second = pairs[:, 1][:, None, None] * ones
    mix = jnp.repeat(pairs, HEAD_DIM, axis=1)[:, None, :]
    return jnp.concatenate([first, second, mix, jnp.zeros((N_PAIRS, 5, PAIR_W), F32)], axis=1)


def kernel(x_prompt, x_sample, cache_fox_k, cache_fox_v, cache_fox_logf, page_table, state_ret, state_lru_h, state_lru_conv, p_prompt, p_sample, norm_w, w_in_even, b_forget, ret_gn_w, w_out_even, w_in_odd, conv_w, conv_b, gate_r_w, gate_r_b, gate_i_w, gate_i_b, lru_lambda, w_out_odd, ple_proj, ple_norm_w, ple_gate_w, final_norm_w):
    bp, lp, d = x_prompt.shape
    bs, ls, _ = x_sample.shape
    depth = norm_w.shape[0]
    n_pool = cache_fox_k.shape[1]
    n_pages = page_table.shape[1]
    past_len = n_pages * PAGE
    lru_w = w_out_odd.shape[1]
    rows_p, rows_s = bp * lp, bs * ls
    tm_p = 512

    log_gamma = jnp.log1p(-jnp.exp2(-5.0 - jnp.arange(N_HEADS, dtype=F32)))
    lg_tab = _log_gamma_table(log_gamma)
    cos_p, sin_p = _rope_tables(jnp.arange(lp, dtype=F32))
    cos_s, sin_s = _rope_tables(past_len + jnp.arange(ls, dtype=F32))
    cos_s, sin_s = jnp.tile(cos_s, (bs, 1)), jnp.tile(sin_s, (bs, 1))

    cache_kt = cache_fox_k.transpose(0, 1, 3, 4, 2).reshape(-1, n_pool, HEAD_W, PAGE)
    cache_vt = cache_fox_v.transpose(0, 1, 3, 4, 2).reshape(-1, n_pool, HEAD_W, PAGE)
    cache_ft = jnp.swapaxes(cache_fox_logf, 2, 3)

    row2 = lambda v: v.reshape(1, -1)
    hp = x_prompt.reshape(rows_p, d)
    hs = x_sample.reshape(rows_s, d)
    outs = {k: [] for k in ("fl_p", "rs_p", "lh_p", "lc_p", "fk_s", "fv_s", "fl_s", "rs_s", "lh_s", "lc_s")}
    y_p = y_s = None
    n_even = w_in_even.shape[0]
    stacked_kv = None

    for i in range(depth):
        j = i // 2
        proj = ple_proj[i].astype(BF16)
        pgw = ple_gate_w[i].astype(BF16)
        pnw = row2(ple_norm_w[i])
        nw = row2(norm_w[i])
        if i % 2 == 0:
            w = w_in_even[j]
            c0 = 3 * HEAD_W
            wf = w[:, :c0].astype(BF16)
            wl = jnp.pad(w[:, c0:c0 + N_HEADS], ((0, 0), (0, PAGE - N_HEADS))).astype(BF16)
            wr = w[:, c0 + N_HEADS:2 * c0 + N_HEADS].astype(BF16)
            wg = w[:, 2 * c0 + N_HEADS:].astype(BF16)
            bf = jnp.pad(b_forget[j], (0, PAGE - N_HEADS)).reshape(1, PAGE)
            wo = w_out_even[j].astype(BF16)
            gn = row2(ret_gn_w[j])
            wt = jnp.swapaxes(w, 0, 1)
            wqkvt = wt[:c0].astype(BF16)
            wlt = jnp.pad(wt[c0:c0 + N_HEADS], ((0, N_HEADS), (0, 0))).astype(BF16)
            bfc = jnp.pad(b_forget[j], (0, N_HEADS)).reshape(2 * N_HEADS, 1)

            fqt, fkt_all, fvt_all, fkb, fvtb, lft, rq, rk, rv, g = _even_in_prompt(
                hp.reshape(bp, lp, d), nw, wqkvt, wlt, wr, wg, bfc, cos_p, sin_p, stacked_kv,
                layer=j, n_layers=n_even, tm=tm_p)
            stacked_kv = (fkt_all, fvt_all)
            cum_rows = _seg_cumsum(lft.reshape(bp * N_HEADS, lp), seg=lp).reshape(bp, N_PAIRS, 2, lp)
            fo = _fox_prompt(fqt, fkb, fvtb, cum_rows, tq=512, tk=512)
            ro, s_p = _ret_prompt(rq, rk, rv, lg_tab, gn)
            hp = _even_out(hp, fo.reshape(rows_p, HEAD_W), ro.reshape(rows_p, HEAD_W),
                           g.reshape(rows_p, 2 * HEAD_W), wo, p_prompt.reshape(depth, rows_p, -1),
                           proj, pnw, pgw, layer=i, tm=tm_p)
            outs["fl_p"].append(jnp.swapaxes(lft, 1, 2))
            outs["rs_p"].append(s_p)

            fq, fk, fv, lf, lft, rq, rk, rv, g = _even_in_sample(hs, nw, wf, wl, wr, wg, bf, cos_s, sin_s)
            cn = _seg_cumsum(lft, seg=ls).reshape(N_HEADS, bs, ls).transpose(1, 0, 2)
            cn_col = cn.reshape(bs, N_HEADS * ls, 1)
            cn_rows = jnp.pad(jnp.repeat(cn, ls, axis=1), ((0, 0), (0, 0), (0, PAGE - ls)))
            fo = _fox_sample(fq, fk, fv, cn_col, cn_rows, cache_kt, cache_vt, cache_ft, page_table, j, t_new=ls)
            ro, s_s = _ret_sample(rq, rk, rv, state_ret, lg_tab, gn, layer=j, length=ls)
            hs = _even_out(hs, fo, ro, g, wo, p_sample.reshape(depth, rows_s, -1),
                           proj, pnw, pgw, layer=i, tm=rows_s)
            outs["fk_s"].append(fk.reshape(bs, ls, N_HEADS, HEAD_DIM))
            outs["fv_s"].append(fv.reshape(bs, ls, N_HEADS, HEAD_DIM))
            outs["fl_s"].append(lf.reshape(bs, ls, N_HEADS))
            outs["rs_s"].append(s_s)
        else:
            final = i == depth - 1
            gw = jnp.concatenate([gate_r_w[j], gate_i_w[j]], axis=-1).astype(BF16)
            args = (nw, w_in_odd[j].astype(BF16), conv_w[j], row2(conv_b[j]), gw, row2(gate_r_b[j]),
                    row2(gate_i_b[j]), row2(lru_lambda[j]), w_out_odd[j].astype(BF16))
            tail = (proj, pnw, pgw)
            fnw = row2(final_norm_w)
            o, lh, lc = _odd_layer(hp.reshape(bp, lp, d), *args, p_prompt, *tail,
                                   jnp.zeros((bp, lru_w), F32), jnp.zeros((CONV_W - 1, bp, lru_w), F32),
                                   fnw, layer=i, nb=SEQ_PER_BLOCK, tl=64, final_norm=final)
            hp = o.reshape(rows_p, d)
            outs["lh_p"].append(lh)
            outs["lc_p"].append(jnp.swapaxes(lc, 0, 1))
            o, lh, lc = _odd_layer(hs.reshape(bs, ls, d), *args, p_sample, *tail,
                                   state_lru_h[j], jnp.swapaxes(state_lru_conv[j], 0, 1),
                                   fnw, layer=i, nb=SEQ_PER_BLOCK, tl=ls, final_norm=final)
            hs = o.reshape(rows_s, d)
            outs["lh_s"].append(lh)
            outs["lc_s"].append(jnp.swapaxes(lc, 0, 1))
            if final:
                y_p, y_s = hp.reshape(bp, lp, d), hs.reshape(bs, ls, d)

    st = lambda k: jnp.stack(outs[k])
    to_blhd = lambda t: t.reshape(n_even, bp, N_HEADS, HEAD_DIM, lp).transpose(0, 1, 4, 2, 3)
    return (y_p, y_s, to_blhd(stacked_kv[0]), to_blhd(stacked_kv[1]), st("fl_p"), st("rs_p"), st("lh_p"), st("lc_p"),
            st("fk_s"), st("fv_s"), st("fl_s"), st("rs_s"), st("lh_s"), st("lc_s"))
```

```python
import functools
import math

import jax
import jax.numpy as jnp
from jax import lax
from jax.experimental import pallas as pl
from jax.experimental.pallas import tpu as pltpu

F32 = jnp.float32
BF16 = jnp.bfloat16

HEAD_DIM = 64
HALF_DIM = HEAD_DIM // 2
N_HEADS = 8
HEAD_W = N_HEADS * HEAD_DIM
PAIR_W = 2 * HEAD_DIM
N_PAIRS = N_HEADS // 2
PAGE = 128
RET_CHUNK = 128
RET_GROUP = 8
ROPE_BASE = 10000.0
LRU_BLOCKS = 8
LRU_BW = 128
CONV_W = 4
SEQ_PER_BLOCK = 8
LRU_C = 8.0
NORM_EPS = 1e-6
NEG_INF = -1e30
QK_SCALE = HEAD_DIM ** -0.5
VMEM_LIMIT = 56 * 1024 * 1024


def _params(*sem):
    return pltpu.CompilerParams(dimension_semantics=sem, vmem_limit_bytes=VMEM_LIMIT)


def _mm(a, b):
    return jnp.dot(a, b, preferred_element_type=F32)


def _mm_nt(a, b):
    return lax.dot_general(a, b, (((1,), (1,)), ((), ())), preferred_element_type=F32)


def _rms(x, w):
    return (x * lax.rsqrt(jnp.mean(x * x, axis=-1, keepdims=True) + NORM_EPS)) * w


def _const_spec(shape):
    nd = len(shape)
    return pl.BlockSpec(shape, lambda *_: (0,) * nd)


def _swap_halves(x):
    n = x.shape[-1]
    lane = lax.broadcasted_iota(jnp.int32, x.shape, x.ndim - 1)
    return jnp.where(lane % HEAD_DIM < HALF_DIM,
                     pltpu.roll(x, n - HALF_DIM, x.ndim - 1),
                     pltpu.roll(x, HALF_DIM, x.ndim - 1))


def _retention_gate_proj(ub, wr_ref, wg_ref, cos_ref, sin_ref, rq_ref, rk_ref, rv_ref, g_ref):
    zr = _mm(ub, wr_ref[...])
    cos = cos_ref[...]
    sin = sin_ref[...]
    rq = zr[:, :HEAD_W]
    rk = zr[:, HEAD_W:2 * HEAD_W]
    rq_ref[...] = (rq * cos + _swap_halves(rq) * sin).astype(BF16)
    rk_ref[...] = (rk * cos + _swap_halves(rk) * sin) * QK_SCALE
    rv_ref[...] = zr[:, 2 * HEAD_W:].astype(BF16)
    g_ref[...] = _mm(ub, wg_ref[...])


N_EVEN_IN_PROMPT_INPUTS = 9


def _even_in_prompt_kernel(*refs):
    h_ref, nw_ref, wqkvt_ref, wlt_ref, wr_ref, wg_ref, bfc_ref, cos_ref, sin_ref = refs[:N_EVEN_IN_PROMPT_INPUTS]
    (fqt_ref, fkt_ref, fvt_ref, fkb_ref, fvtb_ref, lft_ref, rq_ref, rk_ref, rv_ref, g_ref) = refs[-10:]
    ub = _rms(h_ref[...], nw_ref[...]).astype(BF16)
    zt = _mm_nt(wqkvt_ref[...], ub)
    fqt_ref[...] = (zt[:HEAD_W] * QK_SCALE).astype(BF16)
    fkt = zt[HEAD_W:2 * HEAD_W]
    fkt_ref[...] = fkt
    fkb_ref[...] = fkt.T.astype(BF16)
    fvt = zt[2 * HEAD_W:]
    fvt_ref[...] = fvt
    fvtb_ref[...] = fvt.astype(BF16)
    lft_ref[...] = jax.nn.log_sigmoid(_mm_nt(wlt_ref[...], ub) + bfc_ref[...])[:N_HEADS]
    _retention_gate_proj(ub, wr_ref, wg_ref, cos_ref, sin_ref, rq_ref, rk_ref, rv_ref, g_ref)


def _even_in_prompt(h, nw, wqkvt, wlt, wr, wg, bfc, cos, sin, stacked_kv, *, layer, n_layers, tm):
    b, l, d = h.shape
    tok = lambda w: pl.BlockSpec((None, tm, w), lambda bi, i: (bi, i, 0))
    feat = lambda r: pl.BlockSpec((None, r, tm), lambda bi, i: (bi, 0, i))
    stacked = pl.BlockSpec((None, None, HEAD_W, tm), lambda bi, i: (layer, bi, 0, i))
    pos = pl.BlockSpec((tm, HEAD_W), lambda bi, i: (i, 0))
    sd = jax.ShapeDtypeStruct
    consts = [nw, wqkvt, wlt, wr, wg, bfc]
    ins = [h, *consts, cos, sin]
    in_specs = [tok(d)] + [_const_spec(c.shape) for c in consts] + [pos, pos]
    assert len(ins) == N_EVEN_IN_PROMPT_INPUTS
    aliases = {}
    if stacked_kv is not None:
        aliases = {len(ins): 1, len(ins) + 1: 2}
        ins += list(stacked_kv)
        in_specs += [pl.BlockSpec(memory_space=pl.ANY)] * 2
    return pl.pallas_call(
        _even_in_prompt_kernel,
        grid=(b, l // tm),
        in_specs=in_specs,
        out_specs=[feat(HEAD_W), stacked, stacked, tok(HEAD_W), feat(HEAD_W), feat(N_HEADS),
                   tok(HEAD_W), tok(HEAD_W), tok(HEAD_W), tok(2 * HEAD_W)],
        out_shape=[sd((b, HEAD_W, l), BF16), sd((n_layers, b, HEAD_W, l), F32), sd((n_layers, b, HEAD_W, l), F32),
                   sd((b, l, HEAD_W), BF16), sd((b, HEAD_W, l), BF16), sd((b, N_HEADS, l), F32),
                   sd((b, l, HEAD_W), BF16), sd((b, l, HEAD_W), F32), sd((b, l, HEAD_W), BF16),
                   sd((b, l, 2 * HEAD_W), F32)],
        input_output_aliases=aliases,
        compiler_params=_params("parallel", "parallel"),
        name="even_in_prompt",
    )(*ins)


def _even_in_sample_kernel(h_ref, nw_ref, wf_ref, wl_ref, wr_ref, wg_ref, bf_ref, cos_ref, sin_ref,
                           fq_ref, fk_ref, fv_ref, lf_ref, lft_ref, rq_ref, rk_ref, rv_ref, g_ref):
    ub = _rms(h_ref[...], nw_ref[...]).astype(BF16)
    zf = _mm(ub, wf_ref[...])
    fq_ref[...] = (zf[:, :HEAD_W] * QK_SCALE).astype(BF16)
    fk_ref[...] = zf[:, HEAD_W:2 * HEAD_W]
    fv_ref[...] = zf[:, 2 * HEAD_W:]
    logf = jax.nn.log_sigmoid(_mm(ub, wl_ref[...]) + bf_ref[...])
    lf_ref[...] = logf[:, :N_HEADS]
    lft_ref[...] = logf.T[:N_HEADS]
    _retention_gate_proj(ub, wr_ref, wg_ref, cos_ref, sin_ref, rq_ref, rk_ref, rv_ref, g_ref)


def _even_in_sample(h, nw, wf, wl, wr, wg, bf, cos, sin):
    rows = h.shape[0]
    sd = jax.ShapeDtypeStruct
    ins = [h, nw, wf, wl, wr, wg, bf, cos, sin]
    outs = [sd((rows, HEAD_W), BF16), sd((rows, HEAD_W), F32), sd((rows, HEAD_W), F32),
            sd((rows, N_HEADS), F32), sd((N_HEADS, rows), F32),
            sd((rows, HEAD_W), BF16), sd((rows, HEAD_W), F32), sd((rows, HEAD_W), BF16),
            sd((rows, 2 * HEAD_W), F32)]
    return pl.pallas_call(
        _even_in_sample_kernel,
        grid=(1,),
        in_specs=[_const_spec(a.shape) for a in ins],
        out_specs=[_const_spec(o.shape) for o in outs],
        out_shape=outs,
        compiler_params=_params("arbitrary"),
        name="even_in_sample",
    )(*ins)


def _seg_cumsum_kernel(x_ref, o_ref, *, seg):
    x = x_ref[...]
    pos = lax.broadcasted_iota(jnp.int32, x.shape, 1) % seg
    s = 1
    while s < seg:
        x = x + jnp.where(pos >= s, pltpu.roll(x, s, 1), 0.0)
        s *= 2
    o_ref[...] = x


def _seg_cumsum(x, *, seg):
    r, n = x.shape
    return pl.pallas_call(
        functools.partial(_seg_cumsum_kernel, seg=seg),
        grid=(r // 8,),
        in_specs=[pl.BlockSpec((8, n), lambda i: (i, 0))],
        out_specs=pl.BlockSpec((8, n), lambda i: (i, 0)),
        out_shape=jax.ShapeDtypeStruct((r, n), F32),
        compiler_params=_params("parallel"),
        name="seg_cumsum",
    )(x)


def _pair_masks(shape):
    lane = lax.broadcasted_iota(jnp.int32, shape, len(shape) - 1)
    return lane < HEAD_DIM


def _split3(x):
    hi = x.astype(BF16).astype(F32)
    rest = x - hi
    mid = rest.astype(BF16).astype(F32)
    return hi, mid, rest - mid


AUG_ROWS = 8
SUM_ROWS = 16


def _fox_prompt_body(qi, qt_ref, k_ref, vt_ref, cr_ref, call_ref, o_ref, kaug_scr, *, tq, tk, with_step):
    seq = k_ref.shape[0]

    @pl.when(qi == 0)
    def _():
        c_all = call_ref[...]
        one = jnp.ones((1, seq), F32)
        zero_row = jnp.zeros((1, seq), F32)
        rows = []
        for h in range(2):
            rows += list(_split3(-c_all[h:h + 1, :])) + [one] * 3 + [zero_row] * (AUG_ROWS - 6)
        aug_t = jnp.concatenate(rows + [jnp.zeros((PAIR_W - 2 * AUG_ROWS, seq), F32)], axis=0)
        kaug_scr[:, :PAIR_W] = k_ref[...]
        kaug_scr[:, PAIR_W:] = aug_t.T.astype(BF16)

    qt = qt_ref[...]
    row = lax.broadcasted_iota(jnp.int32, qt.shape, 0)
    zero = jnp.zeros_like(qt)
    cr = cr_ref[...]
    w_heads = []
    for h in range(2):
        head_rows = (row < HEAD_DIM) if h == 0 else (row >= HEAD_DIM)
        pieces = _split3(cr[h:h + 1, :])
        bias = jnp.zeros(qt.shape, F32)
        for i in range(3):
            bias = jnp.where(row == h * AUG_ROWS + i, 1.0, bias)
            bias = jnp.where(row == h * AUG_ROWS + 3 + i, pieces[i], bias)
        w_heads.append(jnp.concatenate([jnp.where(head_rows, qt, zero), bias.astype(BF16)], axis=0))
    key_l = lax.broadcasted_iota(jnp.int32, (tk, tq), 0)
    qry_l = lax.broadcasted_iota(jnp.int32, (tk, tq), 1)
    ones = jnp.ones((SUM_ROWS, tk), BF16)

    def scores_of(j):
        kc = kaug_scr[j * tk:(j + 1) * tk, :]
        return tuple(_mm(kc, w_heads[h]) for h in range(2))

    def values_of(j, p_heads):
        start = j * tk
        return [_mm(jnp.concatenate([vt_ref[h * HEAD_DIM:(h + 1) * HEAD_DIM, pl.ds(start, tk)], ones], axis=0),
                    p_heads[h]) for h in range(2)]

    def softmax(s, m, diagonal):
        if diagonal:
            s = jnp.where(key_l <= qry_l, s, NEG_INF)
        m_new = jnp.maximum(m, jnp.max(s, axis=0, keepdims=True))
        return m_new, jnp.exp(m - m_new), jnp.exp(s - m_new).astype(BF16)

    def run(n_blocks, other):
        m_heads = [jnp.full((1, tq), NEG_INF, F32)] * 2
        acc_heads = [jnp.zeros((HEAD_DIM + SUM_ROWS, tq), F32)] * 2
        next(other)
        s_cur = scores_of(0)
        for j in range(n_blocks):
            s_next = scores_of(j + 1) if j + 1 < n_blocks else None
            if j == 0:
                next(other)
            p_cur, alphas = [], []
            for h in range(2):
                m_heads[h], alpha, p = softmax(s_cur[h], m_heads[h], j + 1 == n_blocks)
                alphas.append(alpha)
                p_cur.append(p)
            pv = values_of(j, p_cur)
            if j == 0:
                for _ in other:
                    pass
            acc_heads = [alphas[h] * acc_heads[h] + pv[h] for h in range(2)]
            s_cur = s_next
        out = [acc[:HEAD_DIM, :] / acc[HEAD_DIM:HEAD_DIM + 1, :] for acc in acc_heads]
        o_ref[...] = jnp.concatenate(out, axis=0).T

    def branch(q_block):
        run(q_block + 1, with_step(q_block))

    for q_block in range(seq // tq):
        pl.when(qi == q_block)(functools.partial(branch, q_block))


def _ret_tables(lg_ref, length):
    n = RET_CHUNK
    lg_a = lg_ref[0:1, :]
    lg_b = lg_ref[1:2, :]
    lg_m = lg_ref[2:3, :]
    row = lax.broadcasted_iota(jnp.int32, (n, n), 0)
    col = lax.broadcasted_iota(jnp.int32, (n, n), 1)
    diff = row - col
    lower = diff >= 0
    dpos = jnp.where(lower, diff, 0).astype(F32)
    decay = (jnp.where(lower, jnp.exp(dpos * lg_a), 0.0), jnp.where(lower, jnp.exp(dpos * lg_b), 0.0))
    rowf = row.astype(F32)
    q_dec = jnp.exp((rowf + 1.0) * lg_m)
    k_dec = jnp.exp((length - 1.0 - rowf) * lg_m)
    s_dec = jnp.exp(float(length) * jnp.where(row < HEAD_DIM, lg_a, lg_b))
    same_head = (row < HEAD_DIM) == (col < HEAD_DIM)
    return decay, q_dec, k_dec, s_dec, same_head


def _ret_chunk(q, k, v, state, tables):
    decay, q_dec, k_dec, s_dec, same_head = tables
    first = _pair_masks(q.shape)
    zero = jnp.zeros_like(q)
    kb = k.astype(BF16)
    o_heads = []
    for h, qh in enumerate((jnp.where(first, q, zero), jnp.where(first, zero, q))):
        scores = _mm_nt(qh, kb) * decay[h]
        o_heads.append(_mm(scores.astype(BF16), v))
    o = jnp.where(first, o_heads[0], o_heads[1])
    o = o + _mm(q, state.astype(BF16)) * q_dec
    kd_t = (k * k_dec).T.astype(BF16)
    state = state * s_dec + jnp.where(same_head, _mm(kd_t, v), 0.0)
    return o, state


def _group_norm(o, gn_w):
    first = _pair_masks(o.shape)
    inv = 1.0 / HEAD_DIM
    s0 = jnp.sum(jnp.where(first, o, 0.0), axis=-1, keepdims=True)
    s1 = jnp.sum(jnp.where(first, 0.0, o), axis=-1, keepdims=True)
    d = o - jnp.where(first, s0, s1) * inv
    dd = d * d
    v0 = jnp.sum(jnp.where(first, dd, 0.0), axis=-1, keepdims=True)
    v1 = jnp.sum(jnp.where(first, 0.0, dd), axis=-1, keepdims=True)
    var = jnp.where(first, v0, v1) * inv
    return (d * lax.rsqrt(var + NORM_EPS)) * gn_w


def _ret_prompt_kernel(q_ref, k_ref, v_ref, lg_ref, gn_ref, o_ref, s_ref, *, n_chunks):
    tables = _ret_tables(lg_ref, RET_CHUNK)
    gn_w = gn_ref[...]

    decay, q_dec, k_dec, s_dec, same_head = tables
    group = RET_GROUP

    def body(cc, state):
        rows = [pl.ds(pl.multiple_of((cc * group + g) * RET_CHUNK, RET_CHUNK), RET_CHUNK)
                for g in range(group)]
        qs = [q_ref[r, :] for r in rows]
        ks = [k_ref[r, :] for r in rows]
        vs = [v_ref[r, :] for r in rows]
        first = _pair_masks(qs[0].shape)
        zero = jnp.zeros_like(qs[0])
        kbs = [k.astype(BF16) for k in ks]
        scores = [[_mm_nt(jnp.where(first, q, zero) if h == 0 else jnp.where(first, zero, q), kb)
                   for h in range(2)] for q, kb in zip(qs, kbs)]
        incs = [jnp.where(same_head, _mm((k * k_dec).T.astype(BF16), v), 0.0) for k, v in zip(ks, vs)]
        intra = [[_mm((sc[h] * decay[h]).astype(BF16), v) for h in range(2)] for sc, v in zip(scores, vs)]
        states = []
        for g in range(group):
            states.append(state)
            state = state * s_dec + incs[g]
        cross = [_mm(q, s.astype(BF16)) * q_dec for q, s in zip(qs, states)]
        for g in range(group):
            o = jnp.where(first, intra[g][0], intra[g][1]) + cross[g]
            o_ref[rows[g], :] = _group_norm(o, gn_w)
        return state

    state = lax.fori_loop(0, n_chunks // group, body, jnp.zeros((PAIR_W, PAIR_W), F32))
    s_ref[0] = state[:HEAD_DIM, :HEAD_DIM]
    s_ref[1] = state[HEAD_DIM:, HEAD_DIM:]


def _ret_prompt(rq, rk, rv, lg_tab, gn_w):
    b, l, _ = rq.shape
    seq = pl.BlockSpec((None, l, PAIR_W), lambda bi, hp: (bi, 0, hp))
    return pl.pallas_call(
        functools.partial(_ret_prompt_kernel, n_chunks=l // RET_CHUNK),
        grid=(b, N_PAIRS),
        in_specs=[seq, seq, seq,
                  pl.BlockSpec((None, 8, PAIR_W), lambda bi, hp: (hp, 0, 0)),
                  pl.BlockSpec((1, PAIR_W), lambda bi, hp: (0, hp))],
        out_specs=[seq, pl.BlockSpec((None, 2, HEAD_DIM, HEAD_DIM), lambda bi, hp: (bi, hp, 0, 0))],
        out_shape=[jax.ShapeDtypeStruct((b, l, HEAD_W), F32),
                   jax.ShapeDtypeStruct((b, N_HEADS, HEAD_DIM, HEAD_DIM), F32)],
        compiler_params=_params("parallel", "parallel"),
        name="ret_prompt",
    )(rq, rk, rv, lg_tab, gn_w)


def _ret_sample_kernel(q_ref, k_ref, v_ref, s0_ref, lg_ref, gn_ref, o_ref, s_ref, *, length):
    pad = RET_CHUNK - length
    zrow_f = jnp.zeros((pad, PAIR_W), F32)
    zblk = jnp.zeros((HEAD_DIM, HEAD_DIM), F32)
    for hp in range(N_PAIRS):
        lanes = slice(hp * PAIR_W, (hp + 1) * PAIR_W)
        tables = _ret_tables(lg_ref.at[hp], length)
        q = jnp.concatenate([q_ref[:, lanes].astype(F32), zrow_f], axis=0).astype(BF16)
        k = jnp.concatenate([k_ref[:, lanes], zrow_f], axis=0)
        v = jnp.concatenate([v_ref[:, lanes].astype(F32), zrow_f], axis=0).astype(BF16)
        state = jnp.concatenate(
            [jnp.concatenate([s0_ref[2 * hp], zblk], axis=1),
             jnp.concatenate([zblk, s0_ref[2 * hp + 1]], axis=1)], axis=0)
        o, state = _ret_chunk(q, k, v, state, tables)
        o_ref[:, lanes] = _group_norm(o[:length], gn_ref[:, lanes])
        s_ref[2 * hp] = state[:HEAD_DIM, :HEAD_DIM]
        s_ref[2 * hp + 1] = state[HEAD_DIM:, HEAD_DIM:]


def _ret_sample(rq, rk, rv, state, lg_tab, gn_w, *, layer, length):
    rows = rq.shape[0]
    b = rows // length
    tok = pl.BlockSpec((length, HEAD_W), lambda bi: (bi, 0))
    st_in = pl.BlockSpec((None, None, N_HEADS, HEAD_DIM, HEAD_DIM), lambda bi: (layer, bi, 0, 0, 0))
    st = pl.BlockSpec((None, N_HEADS, HEAD_DIM, HEAD_DIM), lambda bi: (bi, 0, 0, 0))
    return pl.pallas_call(
        functools.partial(_ret_sample_kernel, length=length),
        grid=(b,),
        in_specs=[tok, tok, tok, st_in, _const_spec(lg_tab.shape), _const_spec(gn_w.shape)],
        out_specs=[tok, st],
        out_shape=[jax.ShapeDtypeStruct((rows, HEAD_W), F32),
                   jax.ShapeDtypeStruct(state.shape[1:], F32)],
        compiler_params=_params("parallel"),
        name="ret_sample",
    )(rq, rk, rv, state, lg_tab, gn_w)


def _ple(h1, p, proj, pnw, pgw):
    e = _rms(_mm(p.astype(BF16), proj), pnw)
    return h1 + jax.nn.sigmoid(_mm(h1.astype(BF16), pgw)) * e


def _silu(g):
    return g * jax.nn.sigmoid(g)


def _even_out_kernel(h_ref, fo_ref, ro_ref, g_ref, wo_ref, p_ref, proj_ref, pnw_ref, pgw_ref, o_ref):
    sg = _silu(g_ref[...])
    cat_f = (fo_ref[...] * sg[:, :HEAD_W]).astype(BF16)
    cat_r = (ro_ref[...] * sg[:, HEAD_W:]).astype(BF16)
    mix = _mm(cat_f, wo_ref[:HEAD_W, :]) + _mm(cat_r, wo_ref[HEAD_W:, :])
    o_ref[...] = _ple(h_ref[...] + mix, p_ref[...], proj_ref[...], pnw_ref[...], pgw_ref[...])


def _even_out(h, fo, ro, g, wo, p, proj, pnw, pgw, *, layer, tm):
    rows, d = h.shape
    row = lambda w: pl.BlockSpec((tm, w), lambda i: (i, 0))
    return pl.pallas_call(
        _even_out_kernel,
        grid=(rows // tm,),
        in_specs=[row(d), row(HEAD_W), row(HEAD_W), row(2 * HEAD_W), _const_spec(wo.shape),
                  pl.BlockSpec((None, tm, p.shape[-1]), lambda i: (layer, i, 0)),
                  _const_spec(proj.shape), _const_spec(pnw.shape), _const_spec(pgw.shape)],
        out_specs=row(d),
        out_shape=jax.ShapeDtypeStruct((rows, d), F32),
        compiler_params=_params("parallel"),
        name="even_out",
    )(h, fo, ro, g, wo, p, proj, pnw, pgw)


def _odd_kernel(h_ref, nw_ref, win_ref, cw_ref, cb_ref, gw_ref, grb_ref, gib_ref, lam_ref, wout_ref,
                p_ref, proj_ref, pnw_ref, pgw_ref, h0_ref, cbuf_ref, fnw_ref,
                o_ref, lruh_ref, lruc_ref,
                ext_scr, carry_scr, y_scr, sg_scr, *, tl, final_norm):
    nb, _, d = h_ref.shape
    w = wout_ref.shape[0]
    rows = nb * tl
    halo = (CONV_W - 1) * nb

    @pl.when(pl.program_id(1) == 0)
    def _():
        carry_scr[...] = h0_ref[...]
        ext_scr[0:halo, :] = cbuf_ref[...].reshape(halo, w)

    x = jnp.swapaxes(h_ref[...], 0, 1).reshape(rows, d)
    z = _mm(_rms(x, nw_ref[...]).astype(BF16), win_ref[...])
    xb = z[:, :w]
    sg_scr[...] = _silu(z[:, w:])
    ext_scr[halo:halo + rows, :] = xb
    xc = xb * cw_ref[CONV_W - 1:CONV_W, :] + cb_ref[...]
    for kk in range(1, CONV_W):
        xc = xc + ext_scr[halo - kk * nb:halo - kk * nb + rows, :] * cw_ref[CONV_W - 1 - kk:CONV_W - kk, :]
    last_steps = ext_scr[rows:rows + halo, :]
    lruc_ref[...] = last_steps.reshape(CONV_W - 1, nb, w)
    ext_scr[0:halo, :] = last_steps

    sp = jax.nn.softplus(-lam_ref[...])
    for n in range(LRU_BLOCKS):
        lanes = slice(n * LRU_BW, (n + 1) * LRU_BW)
        xn = xc[:, lanes]
        zz = _mm(xn.astype(BF16), gw_ref[n])
        r = jax.nn.sigmoid(zz[:, :LRU_BW] + grb_ref[:, lanes])
        gi = jax.nn.sigmoid(zz[:, LRU_BW:] + gib_ref[:, lanes])
        log_a = (-LRU_C * r) * sp[:, lanes]
        a = jnp.exp(log_a)
        one_minus = -jnp.tanh(log_a) * (a * a + 1.0)
        b = (jnp.sqrt(one_minus) * gi) * xn
        hs = carry_scr[:, lanes]
        for t in range(tl):
            step = slice(t * nb, (t + 1) * nb)
            hs = a[step] * hs + b[step]
            y_scr[step, lanes] = hs
        carry_scr[:, lanes] = hs
        lruh_ref[:, lanes] = hs

    mix = _mm((y_scr[...] * sg_scr[...]).astype(BF16), wout_ref[...])
    p = jnp.swapaxes(p_ref[...], 0, 1).reshape(rows, -1)
    h2 = _ple(x + mix, p, proj_ref[...], pnw_ref[...], pgw_ref[...])
    if final_norm:
        h2 = _rms(h2, fnw_ref[...])
    o_ref[...] = jnp.swapaxes(h2.reshape(tl, nb, d), 0, 1)


def _odd_layer(h, nw, win, cw, cb, gw, grb, gib, lam, wout, p, proj, pnw, pgw, h0, cbuf, fnw,
               *, layer, nb, tl, final_norm):
    b, l, d = h.shape
    w = wout.shape[0]
    seq = lambda width: pl.BlockSpec((nb, tl, width), lambda bi, li: (bi, li, 0))
    state = pl.BlockSpec((nb, w), lambda bi, li: (bi, 0))
    conv = pl.BlockSpec((CONV_W - 1, nb, w), lambda bi, li: (0, bi, 0))
    consts = [nw, win, cw, cb, gw, grb, gib, lam, wout]
    tail = [proj, pnw, pgw]
    rows = nb * tl
    return pl.pallas_call(
        functools.partial(_odd_kernel, tl=tl, final_norm=final_norm),
        grid=(b // nb, l // tl),
        in_specs=[seq(d)] + [_const_spec(c.shape) for c in consts]
                 + [pl.BlockSpec((None, nb, tl, p.shape[-1]), lambda bi, li: (layer, bi, li, 0))]
                 + [_const_spec(c.shape) for c in tail] + [state, conv, _const_spec(fnw.shape)],
        out_specs=[seq(d), state, conv],
        out_shape=[jax.ShapeDtypeStruct((b, l, d), F32),
                   jax.ShapeDtypeStruct((b, w), F32),
                   jax.ShapeDtypeStruct((CONV_W - 1, b, w), F32)],
        scratch_shapes=[pltpu.VMEM(((CONV_W - 1) * nb + rows, w), F32), pltpu.VMEM((nb, w), F32),
                        pltpu.VMEM((rows, w), F32), pltpu.VMEM((rows, w), F32)],
        compiler_params=_params("parallel", "arbitrary"),
        name="odd_layer",
    )(h, *consts, p, *tail, h0, cbuf, fnw)


def _fox_sample_body(c, n_steps, q_ref, kn_ref, vn_ref, cnc_ref, cnr_ref, tri_ref, k_pages, v_pages, f_pages,
                     o_ref, qbd_scr, m_scr, l_scr, acc_scr, rel_scr, *, t_new):
    pages_per_step = len(k_pages)
    cn_col = cnc_ref[...]

    def start():
        rel_scr[...] = jnp.zeros_like(rel_scr)
        q = q_ref[...].astype(F32)
        lane_head = lax.broadcasted_iota(jnp.int32, q.shape, 1) // HEAD_DIM
        zero = jnp.zeros_like(q)
        for h in range(N_HEADS):
            qbd_scr[h * t_new:(h + 1) * t_new, :] = jnp.where(lane_head == h, q, zero)
        m_scr[...] = jnp.full(m_scr.shape, NEG_INF, F32)
        l_scr[...] = jnp.zeros_like(l_scr)
        acc_scr[...] = jnp.zeros_like(acc_scr)

    if c == 0:
        start()
    qbd = qbd_scr[...].astype(BF16)

    def online(s, pv_fn):
        m = m_scr[...]
        m_new = jnp.maximum(m, jnp.max(s, axis=-1, keepdims=True))
        alpha = jnp.exp(m - m_new)
        p = jnp.exp(s - m_new)
        l_scr[...] = alpha * l_scr[...] + jnp.sum(p, axis=-1, keepdims=True)
        acc_scr[...] = alpha * acc_scr[...] + pv_fn(p.astype(BF16))
        m_scr[...] = m_new

    xs = jnp.concatenate([f[...] for f in f_pages], axis=0)
    hi = xs.astype(BF16)
    rest = xs - hi.astype(F32)
    mid = rest.astype(BF16)
    lo = (rest - mid.astype(F32)).astype(BF16)
    tri = tri_ref[...]
    sums = (_mm(hi, tri) + _mm(mid, tri)) + _mm(lo, tri)
    after = rel_scr[...]
    bias = [None] * pages_per_step
    for i in reversed(range(pages_per_step)):
        rows = slice(i * N_HEADS, (i + 1) * N_HEADS)
        rel = sums[rows, :PAGE] + after
        after = after + sums[rows, PAGE:]
        bias[i] = jnp.concatenate(
            [jnp.broadcast_to(rel[h:h + 1, :], (t_new, PAGE)) for h in range(N_HEADS)], axis=0)
    rel_scr[...] = after

    kt = jnp.concatenate([k[...].astype(BF16) for k in k_pages], axis=1)
    vt = jnp.concatenate([v[...].astype(BF16) for v in v_pages], axis=1)
    s = (_mm(qbd, kt) + cn_col) + jnp.concatenate(bias, axis=1)
    yield
    m = m_scr[...]
    m_new = jnp.maximum(m, jnp.max(s, axis=-1, keepdims=True))
    alpha = jnp.exp(m - m_new)
    p = jnp.exp(s - m_new)
    yield
    l_scr[...] = alpha * l_scr[...] + jnp.sum(p, axis=-1, keepdims=True)
    acc_scr[...] = alpha * acc_scr[...] + _mm_nt(p.astype(BF16), vt)
    m_scr[...] = m_new

    def finish():
        zpad = jnp.zeros((PAGE - t_new, HEAD_W), F32)
        kn = jnp.concatenate([kn_ref[...], zpad], axis=0).astype(BF16)
        vn = jnp.concatenate([vn_ref[...], zpad], axis=0).astype(BF16)
        s = (_mm_nt(qbd, kn) + cn_col) - cnr_ref[...]
        t_query = lax.broadcasted_iota(jnp.int32, s.shape, 0) % t_new
        t_key = lax.broadcasted_iota(jnp.int32, s.shape, 1)
        s = jnp.where(t_query >= t_key, s, NEG_INF)
        online(s, lambda p: _mm(p, vn))
        o = acc_scr[...] / l_scr[...]
        lane_head = lax.broadcasted_iota(jnp.int32, (t_new, HEAD_W), 1) // HEAD_DIM
        out = jnp.zeros((t_new, HEAD_W), F32)
        for h in range(N_HEADS):
            out = jnp.where(lane_head == h, o[h * t_new:(h + 1) * t_new, :], out)
        o_ref[...] = out

    if c == n_steps - 1:
        finish()


N_FOX_PROMPT_INPUTS = 5
N_FOX_SAMPLE_INPUTS = 6


def _fox_kernel(tbl_ref, *refs, tq, tk, t_new, pages_per_step):
    del tbl_ref
    prompt_in = refs[:N_FOX_PROMPT_INPUTS]
    sample_in = refs[N_FOX_PROMPT_INPUTS:N_FOX_PROMPT_INPUTS + N_FOX_SAMPLE_INPUTS]
    pages = refs[N_FOX_PROMPT_INPUTS + N_FOX_SAMPLE_INPUTS:][:3 * pages_per_step]
    rest = refs[N_FOX_PROMPT_INPUTS + N_FOX_SAMPLE_INPUTS + 3 * pages_per_step:]
    o_prompt_ref, o_sample_ref, kaug_scr = rest[0], rest[1], rest[2]
    sample_scr = rest[3:]
    n_steps = prompt_in[1].shape[0] // tq

    def sample_step(c):
        return _fox_sample_body(c, n_steps, *sample_in, pages[:pages_per_step],
                         pages[pages_per_step:2 * pages_per_step], pages[2 * pages_per_step:],
                         o_sample_ref, *sample_scr, t_new=t_new)

    _fox_prompt_body(pl.program_id(2), *prompt_in, o_prompt_ref, kaug_scr, tq=tq, tk=tk, with_step=sample_step)


def _fox(qt, kb, vtb, cum_rows, q, k_new, v_new, cn_col, cn_rows, cache_kt, cache_vt, cache_ft, table, layer,
         *, tq, t_new):
    b, l, _ = kb.shape
    bs, n_pages = table.shape
    n_steps = l // tq
    assert bs == b * N_PAIRS and n_pages % n_steps == 0, "one sample sequence per prompt (batch, head pair)"
    pages_per_step = n_pages // n_steps
    n_rows = N_HEADS * t_new
    seq_of = lambda bi, hp: bi * N_PAIRS + hp
    tok = pl.BlockSpec((t_new, HEAD_W), lambda bi, hp, qi, tbl: (seq_of(bi, hp), 0))

    def page_spec(i, height):
        return pl.BlockSpec(
            (None, None, height, PAGE),
            lambda bi, hp, qi, tbl: (layer, tbl[seq_of(bi, hp), (n_steps - 1 - qi) * pages_per_step + i], 0, 0))

    kv_pages = [page_spec(i, HEAD_W) for i in range(pages_per_step)]
    f_pages = [page_spec(i, N_HEADS) for i in range(pages_per_step)]
    pos = jnp.arange(PAGE)
    later = (pos[:, None] > pos[None, :]).astype(BF16)
    tri = jnp.concatenate([later, jnp.ones((PAGE, PAGE), BF16)], axis=1)
    prompt_specs = [pl.BlockSpec((None, PAIR_W, tq), lambda bi, hp, qi, tbl: (bi, hp, qi)),
                    pl.BlockSpec((None, l, PAIR_W), lambda bi, hp, qi, tbl: (bi, 0, hp)),
                    pl.BlockSpec((None, PAIR_W, l), lambda bi, hp, qi, tbl: (bi, hp, 0)),
                    pl.BlockSpec((None, None, 2, tq), lambda bi, hp, qi, tbl: (bi, hp, 0, qi)),
                    pl.BlockSpec((None, None, 2, l), lambda bi, hp, qi, tbl: (bi, hp, 0, 0))]
    sample_specs = [tok, tok, tok,
                    pl.BlockSpec((None, n_rows, 1), lambda bi, hp, qi, tbl: (seq_of(bi, hp), 0, 0)),
                    pl.BlockSpec((None, n_rows, PAGE), lambda bi, hp, qi, tbl: (seq_of(bi, hp), 0, 0)),
                    pl.BlockSpec(tri.shape, lambda bi, hp, qi, tbl: (0, 0))]
    assert len(prompt_specs) == N_FOX_PROMPT_INPUTS and len(sample_specs) == N_FOX_SAMPLE_INPUTS
    return pl.pallas_call(
        functools.partial(_fox_kernel, tq=tq, tk=tq, t_new=t_new, pages_per_step=pages_per_step),
        grid_spec=pltpu.PrefetchScalarGridSpec(
            num_scalar_prefetch=1,
            grid=(b, N_PAIRS, n_steps),
            in_specs=prompt_specs + sample_specs + kv_pages + kv_pages + f_pages,
            out_specs=[pl.BlockSpec((None, tq, PAIR_W), lambda bi, hp, qi, tbl: (bi, qi, hp)), tok],
            scratch_shapes=[pltpu.VMEM((l, 2 * PAIR_W), BF16),
                            pltpu.VMEM((n_rows, HEAD_W), F32), pltpu.VMEM((n_rows, 1), F32),
                            pltpu.VMEM((n_rows, 1), F32), pltpu.VMEM((n_rows, HEAD_W), F32),
                            pltpu.VMEM((N_HEADS, PAGE), F32)]),
        out_shape=[jax.ShapeDtypeStruct((b, l, HEAD_W), F32),
                   jax.ShapeDtypeStruct((bs * t_new, HEAD_W), F32)],
        compiler_params=_params("parallel", "parallel", "arbitrary"),
        name="fox",
    )(table, qt, kb, vtb, cum_rows, cum_rows, q, k_new, v_new, cn_col, cn_rows, tri,
      *([cache_kt] * pages_per_step), *([cache_vt] * pages_per_step), *([cache_ft] * pages_per_step))


def _rope_tables(pos):
    inv = ROPE_BASE ** (-jnp.arange(HALF_DIM, dtype=F32) / HALF_DIM)
    ang = pos[:, None] * inv[None, :]
    cos = jnp.cos(ang)
    sin = jnp.sin(ang)
    cos_head = jnp.concatenate([cos, cos], axis=-1)
    sin_head = jnp.concatenate([-sin, sin], axis=-1)
    return jnp.tile(cos_head, (1, N_HEADS)), jnp.tile(sin_head, (1, N_HEADS))


def _log_gamma_table(log_gamma):
    pairs = log_gamma.reshape(N_PAIRS, 2)
    ones = jnp.ones((N_PAIRS, 1, PAIR_W), F32)
    first = pairs[:, 0][:, None, None] * ones
    second = pairs[:, 1][:, None, None] * ones
    mix = jnp.repeat(pairs, HEAD_DIM, axis=1)[:, None, :]
    return jnp.concatenate([first, second, mix, jnp.zeros((N_PAIRS, 5, PAIR_W), F32)], axis=1)


def kernel(x_prompt, x_sample, cache_fox_k, cache_fox_v, cache_fox_logf, page_table, state_ret, state_lru_h, state_lru_conv, p_prompt, p_sample, norm_w, w_in_even, b_forget, ret_gn_w, w_out_even, w_in_odd, conv_w, conv_b, gate_r_w, gate_r_b, gate_i_w, gate_i_b, lru_lambda, w_out_odd, ple_proj, ple_norm_w, ple_gate_w, final_norm_w):
    bp, lp, d = x_prompt.shape
    bs, ls, _ = x_sample.shape
    depth = norm_w.shape[0]
    n_pool = cache_fox_k.shape[1]
    n_pages = page_table.shape[1]
    past_len = n_pages * PAGE
    lru_w = w_out_odd.shape[1]
    rows_p, rows_s = bp * lp, bs * ls
    tm_p = 512

    log_gamma = jnp.log1p(-jnp.exp2(-5.0 - jnp.arange(N_HEADS, dtype=F32)))
    lg_tab = _log_gamma_table(log_gamma)
    cos_p, sin_p = _rope_tables(jnp.arange(lp, dtype=F32))
    cos_s, sin_s = _rope_tables(past_len + jnp.arange(ls, dtype=F32))
    cos_s, sin_s = jnp.tile(cos_s, (bs, 1)), jnp.tile(sin_s, (bs, 1))

    cache_kt = cache_fox_k.transpose(0, 1, 3, 4, 2).reshape(-1, n_pool, HEAD_W, PAGE)
    cache_vt = cache_fox_v.transpose(0, 1, 3, 4, 2).reshape(-1, n_pool, HEAD_W, PAGE)
    cache_ft = jnp.swapaxes(cache_fox_logf, 2, 3)

    row2 = lambda v: v.reshape(1, -1)
    hp = x_prompt.reshape(rows_p, d)
    hs = x_sample.reshape(rows_s, d)
    outs = {k: [] for k in ("fl_p", "rs_p", "lh_p", "lc_p", "fk_s", "fv_s", "fl_s", "rs_s", "lh_s", "lc_s")}
    y_p = y_s = None
    n_even = w_in_even.shape[0]
    stacked_kv = None

    for i in range(depth):
        j = i // 2
        proj = ple_proj[i].astype(BF16)
        pgw = ple_gate_w[i].astype(BF16)
        pnw = row2(ple_norm_w[i])
        nw = row2(norm_w[i])
        if i % 2 == 0:
            w = w_in_even[j]
            c0 = 3 * HEAD_W
            wf = w[:, :c0].astype(BF16)
            wl = jnp.pad(w[:, c0:c0 + N_HEADS], ((0, 0), (0, PAGE - N_HEADS))).astype(BF16)
            wr = w[:, c0 + N_HEADS:2 * c0 + N_HEADS].astype(BF16)
            wg = w[:, 2 * c0 + N_HEADS:].astype(BF16)
            bf = jnp.pad(b_forget[j], (0, PAGE - N_HEADS)).reshape(1, PAGE)
            wo = w_out_even[j].astype(BF16)
            gn = row2(ret_gn_w[j])
            wt = jnp.swapaxes(w, 0, 1)
            wqkvt = wt[:c0].astype(BF16)
            wlt = jnp.pad(wt[c0:c0 + N_HEADS], ((0, N_HEADS), (0, 0))).astype(BF16)
            bfc = jnp.pad(b_forget[j], (0, N_HEADS)).reshape(2 * N_HEADS, 1)

            fqt, fkt_all, fvt_all, fkb, fvtb, lft, rq, rk, rv, g = _even_in_prompt(
                hp.reshape(bp, lp, d), nw, wqkvt, wlt, wr, wg, bfc, cos_p, sin_p, stacked_kv,
                layer=j, n_layers=n_even, tm=tm_p)
            stacked_kv = (fkt_all, fvt_all)
            cum_rows = _seg_cumsum(lft.reshape(bp * N_HEADS, lp), seg=lp).reshape(bp, N_PAIRS, 2, lp)
            fq, fk, fv, lf, lft_s, rq_s, rk_s, rv_s, g_s = _even_in_sample(
                hs, nw, wf, wl, wr, wg, bf, cos_s, sin_s)
            cn = _seg_cumsum(lft_s, seg=ls).reshape(N_HEADS, bs, ls).transpose(1, 0, 2)
            cn_col = cn.reshape(bs, N_HEADS * ls, 1)
            cn_rows = jnp.pad(jnp.repeat(cn, ls, axis=1), ((0, 0), (0, 0), (0, PAGE - ls)))
            fo, fo_s = _fox(fqt, fkb, fvtb, cum_rows, fq, fk, fv, cn_col, cn_rows,
                            cache_kt, cache_vt, cache_ft, page_table, j, tq=512, t_new=ls)

            ro, s_p = _ret_prompt(rq, rk, rv, lg_tab, gn)
            hp = _even_out(hp, fo.reshape(rows_p, HEAD_W), ro.reshape(rows_p, HEAD_W),
                           g.reshape(rows_p, 2 * HEAD_W), wo, p_prompt.reshape(depth, rows_p, -1),
                           proj, pnw, pgw, layer=i, tm=tm_p)
            outs["fl_p"].append(jnp.swapaxes(lft, 1, 2))
            outs["rs_p"].append(s_p)

            ro, s_s = _ret_sample(rq_s, rk_s, rv_s, state_ret, lg_tab, gn, layer=j, length=ls)
            hs = _even_out(hs, fo_s, ro, g_s, wo, p_sample.reshape(depth, rows_s, -1),
                           proj, pnw, pgw, layer=i, tm=rows_s)
            outs["fk_s"].append(fk.reshape(bs, ls, N_HEADS, HEAD_DIM))
            outs["fv_s"].append(fv.reshape(bs, ls, N_HEADS, HEAD_DIM))
            outs["fl_s"].append(lf.reshape(bs, ls, N_HEADS))
            outs["rs_s"].append(s_s)
        else:
            final = i == depth - 1
            gw = jnp.concatenate([gate_r_w[j], gate_i_w[j]], axis=-1).astype(BF16)
            args = (nw, w_in_odd[j].astype(BF16), conv_w[j], row2(conv_b[j]), gw, row2(gate_r_b[j]),
                    row2(gate_i_b[j]), row2(lru_lambda[j]), w_out_odd[j].astype(BF16))
            tail = (proj, pnw, pgw)
            fnw = row2(final_norm_w)
            o, lh, lc = _odd_layer(hp.reshape(bp, lp, d), *args, p_prompt, *tail,
                                   jnp.zeros((bp, lru_w), F32), jnp.zeros((CONV_W - 1, bp, lru_w), F32),
                                   fnw, layer=i, nb=SEQ_PER_BLOCK, tl=64, final_norm=final)
            hp = o.reshape(rows_p, d)
            outs["lh_p"].append(lh)
            outs["lc_p"].append(jnp.swapaxes(lc, 0, 1))
            o, lh, lc = _odd_layer(hs.reshape(bs, ls, d), *args, p_sample, *tail,
                                   state_lru_h[j], jnp.swapaxes(state_lru_conv[j], 0, 1),
                                   fnw, layer=i, nb=SEQ_PER_BLOCK, tl=ls, final_norm=final)
            hs = o.reshape(rows_s, d)
            outs["lh_s"].append(lh)
            outs["lc_s"].append(jnp.swapaxes(lc, 0, 1))
            if final:
                y_p, y_s = hp.reshape(bp, lp, d), hs.reshape(bs, ls, d)

    st = lambda k: jnp.stack(outs[k])
    to_blhd = lambda t: t.reshape(n_even, bp, N_HEADS, HEAD_DIM, lp).transpose(0, 1, 4, 2, 3)
    return (y_p, y_s, to_blhd(stacked_kv[0]), to_blhd(stacked_kv[1]), st("fl_p"), st("rs_p"), st("lh_p"), st("lc_p"),
            st("fk_s"), st("fv_s"), st("fl_s"), st("rs_s"), st("lh_s"), st("lc_s"))
```

```python
import functools
import math

import jax
import jax.numpy as jnp
from jax import lax
from jax.experimental import pallas as pl
from jax.experimental.pallas import tpu as pltpu

F32 = jnp.float32
BF16 = jnp.bfloat16

HEAD_DIM = 64
HALF_DIM = HEAD_DIM // 2
N_HEADS = 8
HEAD_W = N_HEADS * HEAD_DIM
PAIR_W = 2 * HEAD_DIM
N_PAIRS = N_HEADS // 2
PAGE = 128
RET_CHUNK = 128
RET_GROUP = 8
ROPE_BASE = 10000.0
LRU_BLOCKS = 8
LRU_BW = 128
CONV_W = 4
SEQ_PER_BLOCK = 8
LRU_C = 8.0
NORM_EPS = 1e-6
NEG_INF = -1e30
QK_SCALE = HEAD_DIM ** -0.5
VMEM_LIMIT = 56 * 1024 * 1024


def _params(*sem):
    return pltpu.CompilerParams(dimension_semantics=sem, vmem_limit_bytes=VMEM_LIMIT)


def _mm(a, b):
    return jnp.dot(a, b, preferred_element_type=F32)


def _mm_nt(a, b):
    return lax.dot_general(a, b, (((1,), (1,)), ((), ())), preferred_element_type=F32)


def _rms(x, w):
    return (x * lax.rsqrt(jnp.mean(x * x, axis=-1, keepdims=True) + NORM_EPS)) * w


def _const_spec(shape):
    nd = len(shape)
    return pl.BlockSpec(shape, lambda *_: (0,) * nd)


def _swap_halves(x):
    n = x.shape[-1]
    lane = lax.broadcasted_iota(jnp.int32, x.shape, x.ndim - 1)
    return jnp.where(lane % HEAD_DIM < HALF_DIM,
                     pltpu.roll(x, n - HALF_DIM, x.ndim - 1),
                     pltpu.roll(x, HALF_DIM, x.ndim - 1))


def _retention_gate_proj(ub, wr_ref, wg_ref, cos_ref, sin_ref, rq_ref, rk_ref, rv_ref, g_ref):
    zr = _mm(ub, wr_ref[...])
    cos = cos_ref[...]
    sin = sin_ref[...]
    rq = zr[:, :HEAD_W]
    rk = zr[:, HEAD_W:2 * HEAD_W]
    rq_ref[...] = (rq * cos + _swap_halves(rq) * sin).astype(BF16)
    rk_ref[...] = (rk * cos + _swap_halves(rk) * sin) * QK_SCALE
    rv_ref[...] = zr[:, 2 * HEAD_W:].astype(BF16)
    g_ref[...] = _mm(ub, wg_ref[...])


N_EVEN_IN_PROMPT_INPUTS = 9


def _even_in_prompt_kernel(*refs):
    h_ref, nw_ref, wqkvt_ref, wlt_ref, wr_ref, wg_ref, bfc_ref, cos_ref, sin_ref = refs[:N_EVEN_IN_PROMPT_INPUTS]
    (fqt_ref, fkt_ref, fvt_ref, fkb_ref, fvtb_ref, lft_ref, rq_ref, rk_ref, rv_ref, g_ref) = refs[-10:]
    ub = _rms(h_ref[...], nw_ref[...]).astype(BF16)
    zt = _mm_nt(wqkvt_ref[...], ub)
    fqt_ref[...] = (zt[:HEAD_W] * QK_SCALE).astype(BF16)
    fkt = zt[HEAD_W:2 * HEAD_W]
    fkt_ref[...] = fkt
    fkb_ref[...] = fkt.T.astype(BF16)
    fvt = zt[2 * HEAD_W:]
    fvt_ref[...] = fvt
    fvtb_ref[...] = fvt.astype(BF16)
    lft_ref[...] = jax.nn.log_sigmoid(_mm_nt(wlt_ref[...], ub) + bfc_ref[...])[:N_HEADS]
    _retention_gate_proj(ub, wr_ref, wg_ref, cos_ref, sin_ref, rq_ref, rk_ref, rv_ref, g_ref)


def _even_in_prompt(h, nw, wqkvt, wlt, wr, wg, bfc, cos, sin, stacked_kv, *, layer, n_layers, tm):
    b, l, d = h.shape
    tok = lambda w: pl.BlockSpec((None, tm, w), lambda bi, i: (bi, i, 0))
    feat = lambda r: pl.BlockSpec((None, r, tm), lambda bi, i: (bi, 0, i))
    stacked = pl.BlockSpec((None, None, HEAD_W, tm), lambda bi, i: (layer, bi, 0, i))
    pos = pl.BlockSpec((tm, HEAD_W), lambda bi, i: (i, 0))
    sd = jax.ShapeDtypeStruct
    consts = [nw, wqkvt, wlt, wr, wg, bfc]
    ins = [h, *consts, cos, sin]
    in_specs = [tok(d)] + [_const_spec(c.shape) for c in consts] + [pos, pos]
    assert len(ins) == N_EVEN_IN_PROMPT_INPUTS
    aliases = {}
    if stacked_kv is not None:
        aliases = {len(ins): 1, len(ins) + 1: 2}
        ins += list(stacked_kv)
        in_specs += [pl.BlockSpec(memory_space=pl.ANY)] * 2
    return pl.pallas_call(
        _even_in_prompt_kernel,
        grid=(b, l // tm),
        in_specs=in_specs,
        out_specs=[feat(HEAD_W), stacked, stacked, tok(HEAD_W), feat(HEAD_W), feat(N_HEADS),
                   tok(HEAD_W), tok(HEAD_W), tok(HEAD_W), tok(2 * HEAD_W)],
        out_shape=[sd((b, HEAD_W, l), BF16), sd((n_layers, b, HEAD_W, l), F32), sd((n_layers, b, HEAD_W, l), F32),
                   sd((b, l, HEAD_W), BF16), sd((b, HEAD_W, l), BF16), sd((b, N_HEADS, l), F32),
                   sd((b, l, HEAD_W), BF16), sd((b, l, HEAD_W), F32), sd((b, l, HEAD_W), BF16),
                   sd((b, l, 2 * HEAD_W), F32)],
        input_output_aliases=aliases,
        compiler_params=_params("parallel", "parallel"),
        name="even_in_prompt",
    )(*ins)


def _even_in_sample_kernel(h_ref, nw_ref, wf_ref, wl_ref, wr_ref, wg_ref, bf_ref, cos_ref, sin_ref,
                           fq_ref, fk_ref, fv_ref, lf_ref, lft_ref, rq_ref, rk_ref, rv_ref, g_ref):
    ub = _rms(h_ref[...], nw_ref[...]).astype(BF16)
    zf = _mm(ub, wf_ref[...])
    fq_ref[...] = (zf[:, :HEAD_W] * QK_SCALE).astype(BF16)
    fk_ref[...] = zf[:, HEAD_W:2 * HEAD_W]
    fv_ref[...] = zf[:, 2 * HEAD_W:]
    logf = jax.nn.log_sigmoid(_mm(ub, wl_ref[...]) + bf_ref[...])
    lf_ref[...] = logf[:, :N_HEADS]
    lft_ref[...] = logf.T[:N_HEADS]
    _retention_gate_proj(ub, wr_ref, wg_ref, cos_ref, sin_ref, rq_ref, rk_ref, rv_ref, g_ref)


def _even_in_sample(h, nw, wf, wl, wr, wg, bf, cos, sin):
    rows = h.shape[0]
    sd = jax.ShapeDtypeStruct
    ins = [h, nw, wf, wl, wr, wg, bf, cos, sin]
    outs = [sd((rows, HEAD_W), BF16), sd((rows, HEAD_W), F32), sd((rows, HEAD_W), F32),
            sd((rows, N_HEADS), F32), sd((N_HEADS, rows), F32),
            sd((rows, HEAD_W), BF16), sd((rows, HEAD_W), F32), sd((rows, HEAD_W), BF16),
            sd((rows, 2 * HEAD_W), F32)]
    return pl.pallas_call(
        _even_in_sample_kernel,
        grid=(1,),
        in_specs=[_const_spec(a.shape) for a in ins],
        out_specs=[_const_spec(o.shape) for o in outs],
        out_shape=outs,
        compiler_params=_params("arbitrary"),
        name="even_in_sample",
    )(*ins)


def _seg_cumsum_kernel(x_ref, o_ref, *, seg):
    x = x_ref[...]
    pos = lax.broadcasted_iota(jnp.int32, x.shape, 1) % seg
    s = 1
    while s < seg:
        x = x + jnp.where(pos >= s, pltpu.roll(x, s, 1), 0.0)
        s *= 2
    o_ref[...] = x


def _seg_cumsum(x, *, seg):
    r, n = x.shape
    return pl.pallas_call(
        functools.partial(_seg_cumsum_kernel, seg=seg),
        grid=(r // 8,),
        in_specs=[pl.BlockSpec((8, n), lambda i: (i, 0))],
        out_specs=pl.BlockSpec((8, n), lambda i: (i, 0)),
        out_shape=jax.ShapeDtypeStruct((r, n), F32),
        compiler_params=_params("parallel"),
        name="seg_cumsum",
    )(x)


def _pair_masks(shape):
    lane = lax.broadcasted_iota(jnp.int32, shape, len(shape) - 1)
    return lane < HEAD_DIM


def _split3(x):
    hi = x.astype(BF16).astype(F32)
    rest = x - hi
    mid = rest.astype(BF16).astype(F32)
    return hi, mid, rest - mid


AUG_ROWS = 8
SUM_ROWS = 16


def _fox_prompt_body(qi, qt_ref, k_ref, vt_ref, cr_ref, call_ref, o_ref, kaug_scr, *, tq, tk, with_step):
    seq = k_ref.shape[0]

    @pl.when(qi == 0)
    def _():
        c_all = call_ref[...]
        one = jnp.ones((1, seq), F32)
        zero_row = jnp.zeros((1, seq), F32)
        rows = []
        for h in range(2):
            rows += list(_split3(-c_all[h:h + 1, :])) + [one] * 3 + [zero_row] * (AUG_ROWS - 6)
        aug_t = jnp.concatenate(rows + [jnp.zeros((PAIR_W - 2 * AUG_ROWS, seq), F32)], axis=0)
        kaug_scr[:, :PAIR_W] = k_ref[...]
        kaug_scr[:, PAIR_W:] = aug_t.T.astype(BF16)

    qt = qt_ref[...]
    row = lax.broadcasted_iota(jnp.int32, qt.shape, 0)
    zero = jnp.zeros_like(qt)
    cr = cr_ref[...]
    w_heads = []
    for h in range(2):
        head_rows = (row < HEAD_DIM) if h == 0 else (row >= HEAD_DIM)
        pieces = _split3(cr[h:h + 1, :])
        bias = jnp.zeros(qt.shape, F32)
        for i in range(3):
            bias = jnp.where(row == h * AUG_ROWS + i, 1.0, bias)
            bias = jnp.where(row == h * AUG_ROWS + 3 + i, pieces[i], bias)
        w_heads.append(jnp.concatenate([jnp.where(head_rows, qt, zero), bias.astype(BF16)], axis=0))
    key_l = lax.broadcasted_iota(jnp.int32, (tk, tq), 0)
    qry_l = lax.broadcasted_iota(jnp.int32, (tk, tq), 1)
    ones = jnp.ones((SUM_ROWS, tk), BF16)

    def scores_of(j):
        kc = kaug_scr[j * tk:(j + 1) * tk, :]
        return tuple(_mm(kc, w_heads[h]) for h in range(2))

    def values_of(j, p_heads):
        start = j * tk
        return [_mm(jnp.concatenate([vt_ref[h * HEAD_DIM:(h + 1) * HEAD_DIM, pl.ds(start, tk)], ones], axis=0),
                    p_heads[h]) for h in range(2)]

    def softmax(s, m, diagonal):
        if diagonal:
            s = jnp.where(key_l <= qry_l, s, NEG_INF)
        m_new = jnp.maximum(m, jnp.max(s, axis=0, keepdims=True))
        return m_new, jnp.exp(m - m_new), jnp.exp(s - m_new).astype(BF16)

    def run(n_blocks, other):
        m_heads = [jnp.full((1, tq), NEG_INF, F32)] * 2
        acc_heads = [jnp.zeros((HEAD_DIM + SUM_ROWS, tq), F32)] * 2
        next(other)
        s_cur = scores_of(0)
        for j in range(n_blocks):
            s_next = scores_of(j + 1) if j + 1 < n_blocks else None
            if j == 0:
                next(other)
            p_cur, alphas = [], []
            for h in range(2):
                m_heads[h], alpha, p = softmax(s_cur[h], m_heads[h], j + 1 == n_blocks)
                alphas.append(alpha)
                p_cur.append(p)
            pv = values_of(j, p_cur)
            if j == 0:
                for _ in other:
                    pass
            acc_heads = [alphas[h] * acc_heads[h] + pv[h] for h in range(2)]
            s_cur = s_next
        out = [acc[:HEAD_DIM, :] / acc[HEAD_DIM:HEAD_DIM + 1, :] for acc in acc_heads]
        o_ref[...] = jnp.concatenate(out, axis=0).T

    def branch(q_block):
        run(q_block + 1, with_step(q_block))

    for q_block in range(seq // tq):
        pl.when(qi == q_block)(functools.partial(branch, q_block))


def _ret_tables(lg_ref, length):
    n = RET_CHUNK
    lg_a = lg_ref[0:1, :]
    lg_b = lg_ref[1:2, :]
    lg_m = lg_ref[2:3, :]
    row = lax.broadcasted_iota(jnp.int32, (n, n), 0)
    col = lax.broadcasted_iota(jnp.int32, (n, n), 1)
    diff = row - col
    lower = diff >= 0
    dpos = jnp.where(lower, diff, 0).astype(F32)
    decay = (jnp.where(lower, jnp.exp(dpos * lg_a), 0.0), jnp.where(lower, jnp.exp(dpos * lg_b), 0.0))
    rowf = row.astype(F32)
    q_dec = jnp.exp((rowf + 1.0) * lg_m)
    k_dec = jnp.exp((length - 1.0 - rowf) * lg_m)
    s_dec = jnp.exp(float(length) * jnp.where(row < HEAD_DIM, lg_a, lg_b))
    same_head = (row < HEAD_DIM) == (col < HEAD_DIM)
    return decay, q_dec, k_dec, s_dec, same_head


def _ret_chunk(q, k, v, state, tables):
    decay, q_dec, k_dec, s_dec, same_head = tables
    first = _pair_masks(q.shape)
    zero = jnp.zeros_like(q)
    kb = k.astype(BF16)
    o_heads = []
    for h, qh in enumerate((jnp.where(first, q, zero), jnp.where(first, zero, q))):
        scores = _mm_nt(qh, kb) * decay[h]
        o_heads.append(_mm(scores.astype(BF16), v))
    o = jnp.where(first, o_heads[0], o_heads[1])
    o = o + _mm(q, state.astype(BF16)) * q_dec
    kd_t = (k * k_dec).T.astype(BF16)
    state = state * s_dec + jnp.where(same_head, _mm(kd_t, v), 0.0)
    return o, state


def _group_norm(o, gn_w):
    first = _pair_masks(o.shape)
    inv = 1.0 / HEAD_DIM
    s0 = jnp.sum(jnp.where(first, o, 0.0), axis=-1, keepdims=True)
    s1 = jnp.sum(jnp.where(first, 0.0, o), axis=-1, keepdims=True)
    d = o - jnp.where(first, s0, s1) * inv
    dd = d * d
    v0 = jnp.sum(jnp.where(first, dd, 0.0), axis=-1, keepdims=True)
    v1 = jnp.sum(jnp.where(first, 0.0, dd), axis=-1, keepdims=True)
    var = jnp.where(first, v0, v1) * inv
    return (d * lax.rsqrt(var + NORM_EPS)) * gn_w


def _ret_prompt_kernel(q_ref, k_ref, v_ref, lg_ref, gn_ref, o_ref, s_ref, *, n_chunks):
    tables = _ret_tables(lg_ref, RET_CHUNK)
    gn_w = gn_ref[...]

    decay, q_dec, k_dec, s_dec, same_head = tables
    group = RET_GROUP

    def body(cc, state):
        rows = [pl.ds(pl.multiple_of((cc * group + g) * RET_CHUNK, RET_CHUNK), RET_CHUNK)
                for g in range(group)]
        qs = [q_ref[r, :] for r in rows]
        ks = [k_ref[r, :] for r in rows]
        vs = [v_ref[r, :] for r in rows]
        first = _pair_masks(qs[0].shape)
        zero = jnp.zeros_like(qs[0])
        kbs = [k.astype(BF16) for k in ks]
        scores = [[_mm_nt(jnp.where(first, q, zero) if h == 0 else jnp.where(first, zero, q), kb)
                   for h in range(2)] for q, kb in zip(qs, kbs)]
        incs = [jnp.where(same_head, _mm((k * k_dec).T.astype(BF16), v), 0.0) for k, v in zip(ks, vs)]
        intra = [[_mm((sc[h] * decay[h]).astype(BF16), v) for h in range(2)] for sc, v in zip(scores, vs)]
        states = []
        for g in range(group):
            states.append(state)
            state = state * s_dec + incs[g]
        cross = [_mm(q, s.astype(BF16)) * q_dec for q, s in zip(qs, states)]
        for g in range(group):
            o = jnp.where(first, intra[g][0], intra[g][1]) + cross[g]
            o_ref[rows[g], :] = _group_norm(o, gn_w)
        return state

    state = lax.fori_loop(0, n_chunks // group, body, jnp.zeros((PAIR_W, PAIR_W), F32))
    s_ref[0] = state[:HEAD_DIM, :HEAD_DIM]
    s_ref[1] = state[HEAD_DIM:, HEAD_DIM:]


def _ret_prompt(rq, rk, rv, lg_tab, gn_w):
    b, l, _ = rq.shape
    seq = pl.BlockSpec((None, l, PAIR_W), lambda bi, hp: (bi, 0, hp))
    return pl.pallas_call(
        functools.partial(_ret_prompt_kernel, n_chunks=l // RET_CHUNK),
        grid=(b, N_PAIRS),
        in_specs=[seq, seq, seq,
                  pl.BlockSpec((None, 8, PAIR_W), lambda bi, hp: (hp, 0, 0)),
                  pl.BlockSpec((1, PAIR_W), lambda bi, hp: (0, hp))],
        out_specs=[seq, pl.BlockSpec((None, 2, HEAD_DIM, HEAD_DIM), lambda bi, hp: (bi, hp, 0, 0))],
        out_shape=[jax.ShapeDtypeStruct((b, l, HEAD_W), F32),
                   jax.ShapeDtypeStruct((b, N_HEADS, HEAD_DIM, HEAD_DIM), F32)],
        compiler_params=_params("parallel", "parallel"),
        name="ret_prompt",
    )(rq, rk, rv, lg_tab, gn_w)


def _ret_sample_kernel(q_ref, k_ref, v_ref, s0_ref, lg_ref, gn_ref, o_ref, s_ref, *, length):
    pad = RET_CHUNK - length
    zrow_f = jnp.zeros((pad, PAIR_W), F32)
    zblk = jnp.zeros((HEAD_DIM, HEAD_DIM), F32)
    for hp in range(N_PAIRS):
        lanes = slice(hp * PAIR_W, (hp + 1) * PAIR_W)
        tables = _ret_tables(lg_ref.at[hp], length)
        q = jnp.concatenate([q_ref[:, lanes].astype(F32), zrow_f], axis=0).astype(BF16)
        k = jnp.concatenate([k_ref[:, lanes], zrow_f], axis=0)
        v = jnp.concatenate([v_ref[:, lanes].astype(F32), zrow_f], axis=0).astype(BF16)
        state = jnp.concatenate(
            [jnp.concatenate([s0_ref[2 * hp], zblk], axis=1),
             jnp.concatenate([zblk, s0_ref[2 * hp + 1]], axis=1)], axis=0)
        o, state = _ret_chunk(q, k, v, state, tables)
        o_ref[:, lanes] = _group_norm(o[:length], gn_ref[:, lanes])
        s_ref[2 * hp] = state[:HEAD_DIM, :HEAD_DIM]
        s_ref[2 * hp + 1] = state[HEAD_DIM:, HEAD_DIM:]


def _ret_sample(rq, rk, rv, state, lg_tab, gn_w, *, layer, length):
    rows = rq.shape[0]
    b = rows // length
    tok = pl.BlockSpec((length, HEAD_W), lambda bi: (bi, 0))
    st_in = pl.BlockSpec((None, None, N_HEADS, HEAD_DIM, HEAD_DIM), lambda bi: (layer, bi, 0, 0, 0))
    st = pl.BlockSpec((None, N_HEADS, HEAD_DIM, HEAD_DIM), lambda bi: (bi, 0, 0, 0))
    return pl.pallas_call(
        functools.partial(_ret_sample_kernel, length=length),
        grid=(b,),
        in_specs=[tok, tok, tok, st_in, _const_spec(lg_tab.shape), _const_spec(gn_w.shape)],
        out_specs=[tok, st],
        out_shape=[jax.ShapeDtypeStruct((rows, HEAD_W), F32),
                   jax.ShapeDtypeStruct(state.shape[1:], F32)],
        compiler_params=_params("parallel"),
        name="ret_sample",
    )(rq, rk, rv, state, lg_tab, gn_w)


def _ple(h1, p, proj, pnw, pgw):
    e = _rms(_mm(p.astype(BF16), proj), pnw)
    return h1 + jax.nn.sigmoid(_mm(h1.astype(BF16), pgw)) * e


def _silu(g):
    return g * jax.nn.sigmoid(g)


def _even_out_kernel(h_ref, fo_ref, ro_ref, g_ref, wo_ref, p_ref, proj_ref, pnw_ref, pgw_ref, o_ref):
    sg = _silu(g_ref[...])
    cat_f = (fo_ref[...] * sg[:, :HEAD_W]).astype(BF16)
    cat_r = (ro_ref[...] * sg[:, HEAD_W:]).astype(BF16)
    mix = _mm(cat_f, wo_ref[:HEAD_W, :]) + _mm(cat_r, wo_ref[HEAD_W:, :])
    o_ref[...] = _ple(h_ref[...] + mix, p_ref[...], proj_ref[...], pnw_ref[...], pgw_ref[...])


def _even_out(h, fo, ro, g, wo, p, proj, pnw, pgw, *, layer, tm):
    rows, d = h.shape
    row = lambda w: pl.BlockSpec((tm, w), lambda i: (i, 0))
    return pl.pallas_call(
        _even_out_kernel,
        grid=(rows // tm,),
        in_specs=[row(d), row(HEAD_W), row(HEAD_W), row(2 * HEAD_W), _const_spec(wo.shape),
                  pl.BlockSpec((None, tm, p.shape[-1]), lambda i: (layer, i, 0)),
                  _const_spec(proj.shape), _const_spec(pnw.shape), _const_spec(pgw.shape)],
        out_specs=row(d),
        out_shape=jax.ShapeDtypeStruct((rows, d), F32),
        compiler_params=_params("parallel"),
        name="even_out",
    )(h, fo, ro, g, wo, p, proj, pnw, pgw)


def _odd_kernel(h_ref, nw_ref, win_ref, cw_ref, cb_ref, gw_ref, grb_ref, gib_ref, lam_ref, wout_ref,
                p_ref, proj_ref, pnw_ref, pgw_ref, h0_ref, cbuf_ref, fnw_ref,
                o_ref, lruh_ref, lruc_ref,
                ext_scr, carry_scr, y_scr, sg_scr, *, tl, final_norm):
    nb, _, d = h_ref.shape
    w = wout_ref.shape[0]
    rows = nb * tl
    halo = (CONV_W - 1) * nb

    @pl.when(pl.program_id(1) == 0)
    def _():
        carry_scr[...] = h0_ref[...]
        ext_scr[0:halo, :] = cbuf_ref[...].reshape(halo, w)

    x = jnp.swapaxes(h_ref[...], 0, 1).reshape(rows, d)
    z = _mm(_rms(x, nw_ref[...]).astype(BF16), win_ref[...])
    xb = z[:, :w]
    sg_scr[...] = _silu(z[:, w:])
    ext_scr[halo:halo + rows, :] = xb
    xc = xb * cw_ref[CONV_W - 1:CONV_W, :] + cb_ref[...]
    for kk in range(1, CONV_W):
        xc = xc + ext_scr[halo - kk * nb:halo - kk * nb + rows, :] * cw_ref[CONV_W - 1 - kk:CONV_W - kk, :]
    last_steps = ext_scr[rows:rows + halo, :]
    lruc_ref[...] = last_steps.reshape(CONV_W - 1, nb, w)
    ext_scr[0:halo, :] = last_steps

    sp = jax.nn.softplus(-lam_ref[...])
    for n in range(LRU_BLOCKS):
        lanes = slice(n * LRU_BW, (n + 1) * LRU_BW)
        xn = xc[:, lanes]
        zz = _mm(xn.astype(BF16), gw_ref[n])
        r = jax.nn.sigmoid(zz[:, :LRU_BW] + grb_ref[:, lanes])
        gi = jax.nn.sigmoid(zz[:, LRU_BW:] + gib_ref[:, lanes])
        log_a = (-LRU_C * r) * sp[:, lanes]
        a = jnp.exp(log_a)
        one_minus = -jnp.tanh(log_a) * (a * a + 1.0)
        b = (jnp.sqrt(one_minus) * gi) * xn
        hs = carry_scr[:, lanes]
        for t in range(tl):
            step = slice(t * nb, (t + 1) * nb)
            hs = a[step] * hs + b[step]
            y_scr[step, lanes] = hs
        carry_scr[:, lanes] = hs
        lruh_ref[:, lanes] = hs

    mix = _mm((y_scr[...] * sg_scr[...]).astype(BF16), wout_ref[...])
    p = jnp.swapaxes(p_ref[...], 0, 1).reshape(rows, -1)
    h2 = _ple(x + mix, p, proj_ref[...], pnw_ref[...], pgw_ref[...])
    if final_norm:
        h2 = _rms(h2, fnw_ref[...])
    o_ref[...] = jnp.swapaxes(h2.reshape(tl, nb, d), 0, 1)


def _odd_layer(h, nw, win, cw, cb, gw, grb, gib, lam, wout, p, proj, pnw, pgw, h0, cbuf, fnw,
               *, layer, nb, tl, final_norm):
    b, l, d = h.shape
    w = wout.shape[0]
    seq = lambda width: pl.BlockSpec((nb, tl, width), lambda bi, li: (bi, li, 0))
    state = pl.BlockSpec((nb, w), lambda bi, li: (bi, 0))
    conv = pl.BlockSpec((CONV_W - 1, nb, w), lambda bi, li: (0, bi, 0))
    consts = [nw, win, cw, cb, gw, grb, gib, lam, wout]
    tail = [proj, pnw, pgw]
    rows = nb * tl
    return pl.pallas_call(
        functools.partial(_odd_kernel, tl=tl, final_norm=final_norm),
        grid=(b // nb, l // tl),
        in_specs=[seq(d)] + [_const_spec(c.shape) for c in consts]
                 + [pl.BlockSpec((None, nb, tl, p.shape[-1]), lambda bi, li: (layer, bi, li, 0))]
                 + [_const_spec(c.shape) for c in tail] + [state, conv, _const_spec(fnw.shape)],
        out_specs=[seq(d), state, conv],
        out_shape=[jax.ShapeDtypeStruct((b, l, d), F32),
                   jax.ShapeDtypeStruct((b, w), F32),
                   jax.ShapeDtypeStruct((CONV_W - 1, b, w), F32)],
        scratch_shapes=[pltpu.VMEM(((CONV_W - 1) * nb + rows, w), F32), pltpu.VMEM((nb, w), F32),
                        pltpu.VMEM((rows, w), F32), pltpu.VMEM((rows, w), F32)],
        compiler_params=_params("parallel", "arbitrary"),
        name="odd_layer",
    )(h, *consts, p, *tail, h0, cbuf, fnw)


def _fox_sample_body(c, n_steps, q_ref, kn_ref, vn_ref, cnc_ref, cnr_ref, tri_ref, k_pages, v_pages, f_pages,
                     o_ref, qbd_scr, m_scr, l_scr, acc_scr, rel_scr, *, t_new):
    pages_per_step = len(k_pages)
    cn_col = cnc_ref[...]

    def start():
        rel_scr[...] = jnp.zeros_like(rel_scr)
        q = q_ref[...].astype(F32)
        lane_head = lax.broadcasted_iota(jnp.int32, q.shape, 1) // HEAD_DIM
        zero = jnp.zeros_like(q)
        for h in range(N_HEADS):
            qbd_scr[h * t_new:(h + 1) * t_new, :] = jnp.where(lane_head == h, q, zero)
        m_scr[...] = jnp.full(m_scr.shape, NEG_INF, F32)
        l_scr[...] = jnp.zeros_like(l_scr)
        acc_scr[...] = jnp.zeros_like(acc_scr)

    if c == 0:
        start()
    qbd = qbd_scr[...].astype(BF16)

    def online(s, pv_fn):
        m = m_scr[...]
        m_new = jnp.maximum(m, jnp.max(s, axis=-1, keepdims=True))
        alpha = jnp.exp(m - m_new)
        p = jnp.exp(s - m_new)
        l_scr[...] = alpha * l_scr[...] + jnp.sum(p, axis=-1, keepdims=True)
        acc_scr[...] = alpha * acc_scr[...] + pv_fn(p.astype(BF16))
        m_scr[...] = m_new

    xs = jnp.concatenate([f[...] for f in f_pages], axis=0)
    hi = xs.astype(BF16)
    rest = xs - hi.astype(F32)
    mid = rest.astype(BF16)
    lo = (rest - mid.astype(F32)).astype(BF16)
    tri = tri_ref[...]
    sums = (_mm(hi, tri) + _mm(mid, tri)) + _mm(lo, tri)
    after = rel_scr[...]
    bias = [None] * pages_per_step
    for i in reversed(range(pages_per_step)):
        rows = slice(i * N_HEADS, (i + 1) * N_HEADS)
        rel = sums[rows, :PAGE] + after
        after = after + sums[rows, PAGE:]
        bias[i] = jnp.concatenate(
            [jnp.broadcast_to(rel[h:h + 1, :], (t_new, PAGE)) for h in range(N_HEADS)], axis=0)
    rel_scr[...] = after

    kt = jnp.concatenate([k[...].astype(BF16) for k in k_pages], axis=1)
    vt = jnp.concatenate([v[...].astype(BF16) for v in v_pages], axis=1)
    s = (_mm(qbd, kt) + cn_col) + jnp.concatenate(bias, axis=1)
    yield
    m = m_scr[...]
    m_new = jnp.maximum(m, jnp.max(s, axis=-1, keepdims=True))
    alpha = jnp.exp(m - m_new)
    p = jnp.exp(s - m_new)
    yield
    l_scr[...] = alpha * l_scr[...] + jnp.sum(p, axis=-1, keepdims=True)
    acc_scr[...] = alpha * acc_scr[...] + _mm_nt(p.astype(BF16), vt)
    m_scr[...] = m_new

    def finish():
        zpad = jnp.zeros((PAGE - t_new, HEAD_W), F32)
        kn = jnp.concatenate([kn_ref[...], zpad], axis=0).astype(BF16)
        vn = jnp.concatenate([vn_ref[...], zpad], axis=0).astype(BF16)
        s = (_mm_nt(qbd, kn) + cn_col) - cnr_ref[...]
        t_query = lax.broadcasted_iota(jnp.int32, s.shape, 0) % t_new
        t_key = lax.broadcasted_iota(jnp.int32, s.shape, 1)
        s = jnp.where(t_query >= t_key, s, NEG_INF)
        online(s, lambda p: _mm(p, vn))
        o = acc_scr[...] / l_scr[...]
        lane_head = lax.broadcasted_iota(jnp.int32, (t_new, HEAD_W), 1) // HEAD_DIM
        out = jnp.zeros((t_new, HEAD_W), F32)
        for h in range(N_HEADS):
            out = jnp.where(lane_head == h, o[h * t_new:(h + 1) * t_new, :], out)
        o_ref[...] = out

    if c == n_steps - 1:
        finish()


N_FOX_PROMPT_INPUTS = 5
N_FOX_SAMPLE_INPUTS = 6


N_PAGE_ARRAYS = 3
N_PAGE_SLOTS = 2


def _fox_kernel(tbl_ref, *refs, tq, tk, t_new, pages_per_step, layer):
    prompt_in = refs[:N_FOX_PROMPT_INPUTS]
    sample_in = refs[N_FOX_PROMPT_INPUTS:N_FOX_PROMPT_INPUTS + N_FOX_SAMPLE_INPUTS]
    rest = refs[N_FOX_PROMPT_INPUTS + N_FOX_SAMPLE_INPUTS:]
    caches = rest[:N_PAGE_ARRAYS]
    o_prompt_ref, o_sample_ref, kaug_scr = rest[N_PAGE_ARRAYS:N_PAGE_ARRAYS + 3]
    sample_scr = rest[N_PAGE_ARRAYS + 3:N_PAGE_ARRAYS + 8]
    page_bufs = rest[N_PAGE_ARRAYS + 8:2 * N_PAGE_ARRAYS + 8]
    sems = rest[2 * N_PAGE_ARRAYS + 8]
    n_steps = prompt_in[1].shape[0] // tq
    seq = pl.program_id(0) * N_PAIRS + pl.program_id(1)
    n_seqs = pl.num_programs(0) * N_PAIRS

    def page_copies(of_seq, step):
        slot = step % N_PAGE_SLOTS
        copies = []
        for i in range(pages_per_step):
            page = tbl_ref[of_seq, (n_steps - 1 - step) * pages_per_step + i]
            for a in range(N_PAGE_ARRAYS):
                copies.append(pltpu.make_async_copy(
                    caches[a].at[layer, page], page_bufs[a].at[slot, i], sems.at[slot, a]))
        return copies

    def start(copies):
        for cp in copies:
            cp.start()

    def sample_step(c):
        if c == 0:
            pl.when(seq == 0)(lambda: start(page_copies(seq, 0)))
        if c + 1 < n_steps:
            start(page_copies(seq, c + 1))
        else:
            pl.when(seq + 1 < n_seqs)(lambda: start(page_copies(seq + 1, 0)))
        for cp in page_copies(seq, c):
            cp.wait()
        slot = c % N_PAGE_SLOTS
        views = [[buf.at[slot, i] for i in range(pages_per_step)] for buf in page_bufs]
        return _fox_sample_body(c, n_steps, *sample_in, *views, o_sample_ref, *sample_scr, t_new=t_new)

    _fox_prompt_body(pl.program_id(2), *prompt_in, o_prompt_ref, kaug_scr, tq=tq, tk=tk, with_step=sample_step)


def _fox(qt, kb, vtb, cum_rows, q, k_new, v_new, cn_col, cn_rows, cache_kt, cache_vt, cache_ft, table, layer,
         *, tq, t_new):
    b, l, _ = kb.shape
    bs, n_pages = table.shape
    n_steps = l // tq
    assert bs == b * N_PAIRS and n_pages % n_steps == 0, "one sample sequence per prompt (batch, head pair)"
    pages_per_step = n_pages // n_steps
    n_rows = N_HEADS * t_new
    seq_of = lambda bi, hp: bi * N_PAIRS + hp
    tok = pl.BlockSpec((t_new, HEAD_W), lambda bi, hp, qi, tbl: (seq_of(bi, hp), 0))

    assert n_steps % N_PAGE_SLOTS == 0, "a sequence must start on buffer slot 0"
    caches = [cache_kt, cache_vt, cache_ft]
    page_bufs = [pltpu.VMEM((N_PAGE_SLOTS, pages_per_step) + c.shape[2:], F32) for c in caches]
    pos = jnp.arange(PAGE)
    later = (pos[:, None] > pos[None, :]).astype(BF16)
    tri = jnp.concatenate([later, jnp.ones((PAGE, PAGE), BF16)], axis=1)
    prompt_specs = [pl.BlockSpec((None, PAIR_W, tq), lambda bi, hp, qi, tbl: (bi, hp, qi)),
                    pl.BlockSpec((None, l, PAIR_W), lambda bi, hp, qi, tbl: (bi, 0, hp)),
                    pl.BlockSpec((None, PAIR_W, l), lambda bi, hp, qi, tbl: (bi, hp, 0)),
                    pl.BlockSpec((None, None, 2, tq), lambda bi, hp, qi, tbl: (bi, hp, 0, qi)),
                    pl.BlockSpec((None, None, 2, l), lambda bi, hp, qi, tbl: (bi, hp, 0, 0))]
    sample_specs = [tok, tok, tok,
                    pl.BlockSpec((None, n_rows, 1), lambda bi, hp, qi, tbl: (seq_of(bi, hp), 0, 0)),
                    pl.BlockSpec((None, n_rows, PAGE), lambda bi, hp, qi, tbl: (seq_of(bi, hp), 0, 0)),
                    pl.BlockSpec(tri.shape, lambda bi, hp, qi, tbl: (0, 0))]
    assert len(prompt_specs) == N_FOX_PROMPT_INPUTS and len(sample_specs) == N_FOX_SAMPLE_INPUTS
    return pl.pallas_call(
        functools.partial(_fox_kernel, tq=tq, tk=tq, t_new=t_new, pages_per_step=pages_per_step, layer=layer),
        grid_spec=pltpu.PrefetchScalarGridSpec(
            num_scalar_prefetch=1,
            grid=(b, N_PAIRS, n_steps),
            in_specs=prompt_specs + sample_specs + [pl.BlockSpec(memory_space=pl.ANY)] * N_PAGE_ARRAYS,
            out_specs=[pl.BlockSpec((None, tq, PAIR_W), lambda bi, hp, qi, tbl: (bi, qi, hp)), tok],
            scratch_shapes=[pltpu.VMEM((l, 2 * PAIR_W), BF16),
                            pltpu.VMEM((n_rows, HEAD_W), F32), pltpu.VMEM((n_rows, 1), F32),
                            pltpu.VMEM((n_rows, 1), F32), pltpu.VMEM((n_rows, HEAD_W), F32),
                            pltpu.VMEM((N_HEADS, PAGE), F32)]
                           + page_bufs + [pltpu.SemaphoreType.DMA((N_PAGE_SLOTS, N_PAGE_ARRAYS))]),
        out_shape=[jax.ShapeDtypeStruct((b, l, HEAD_W), F32),
                   jax.ShapeDtypeStruct((bs * t_new, HEAD_W), F32)],
        compiler_params=_params("arbitrary", "arbitrary", "arbitrary"),
        name="fox",
    )(table, qt, kb, vtb, cum_rows, cum_rows, q, k_new, v_new, cn_col, cn_rows, tri, *caches)


def _rope_tables(pos):
    inv = ROPE_BASE ** (-jnp.arange(HALF_DIM, dtype=F32) / HALF_DIM)
    ang = pos[:, None] * inv[None, :]
    cos = jnp.cos(ang)
    sin = jnp.sin(ang)
    cos_head = jnp.concatenate([cos, cos], axis=-1)
    sin_head = jnp.concatenate([-sin, sin], axis=-1)
    return jnp.tile(cos_head, (1, N_HEADS)), jnp.tile(sin_head, (1, N_HEADS))


def _log_gamma_table(log_gamma):
    pairs = log_gamma.reshape(N_PAIRS, 2)
    ones = jnp.ones((N_PAIRS, 1, PAIR_W), F32)
    first = pairs[:, 0][:, None, None] * ones
    second = pairs[:, 1][:, None, None] * ones
    mix = jnp.repeat(pairs, HEAD_DIM, axis=1)[:, None, :]
    return jnp.concatenate([first, second, mix, jnp.zeros((N_PAIRS, 5, PAIR_W), F32)], axis=1)


def kernel(x_prompt, x_sample, cache_fox_k, cache_fox_v, cache_fox_logf, page_table, state_ret, state_lru_h, state_lru_conv, p_prompt, p_sample, norm_w, w_in_even, b_forget, ret_gn_w, w_out_even, w_in_odd, conv_w, conv_b, gate_r_w, gate_r_b, gate_i_w, gate_i_b, lru_lambda, w_out_odd, ple_proj, ple_norm_w, ple_gate_w, final_norm_w):
    bp, lp, d = x_prompt.shape
    bs, ls, _ = x_sample.shape
    depth = norm_w.shape[0]
    n_pool = cache_fox_k.shape[1]
    n_pages = page_table.shape[1]
    past_len = n_pages * PAGE
    lru_w = w_out_odd.shape[1]
    rows_p, rows_s = bp * lp, bs * ls
    tm_p = 512

    log_gamma = jnp.log1p(-jnp.exp2(-5.0 - jnp.arange(N_HEADS, dtype=F32)))
    lg_tab = _log_gamma_table(log_gamma)
    cos_p, sin_p = _rope_tables(jnp.arange(lp, dtype=F32))
    cos_s, sin_s = _rope_tables(past_len + jnp.arange(ls, dtype=F32))
    cos_s, sin_s = jnp.tile(cos_s, (bs, 1)), jnp.tile(sin_s, (bs, 1))

    cache_kt = cache_fox_k.transpose(0, 1, 3, 4, 2).reshape(-1, n_pool, HEAD_W, PAGE)
    cache_vt = cache_fox_v.transpose(0, 1, 3, 4, 2).reshape(-1, n_pool, HEAD_W, PAGE)
    cache_ft = jnp.swapaxes(cache_fox_logf, 2, 3)

    row2 = lambda v: v.reshape(1, -1)
    hp = x_prompt.reshape(rows_p, d)
    hs = x_sample.reshape(rows_s, d)
    outs = {k: [] for k in ("fl_p", "rs_p", "lh_p", "lc_p", "fk_s", "fv_s", "fl_s", "rs_s", "lh_s", "lc_s")}
    y_p = y_s = None
    n_even = w_in_even.shape[0]
    stacked_kv = None

    for i in range(depth):
        j = i // 2
        proj = ple_proj[i].astype(BF16)
        pgw = ple_gate_w[i].astype(BF16)
        pnw = row2(ple_norm_w[i])
        nw = row2(norm_w[i])
        if i % 2 == 0:
            w = w_in_even[j]
            c0 = 3 * HEAD_W
            wf = w[:, :c0].astype(BF16)
            wl = jnp.pad(w[:, c0:c0 + N_HEADS], ((0, 0), (0, PAGE - N_HEADS))).astype(BF16)
            wr = w[:, c0 + N_HEADS:2 * c0 + N_HEADS].astype(BF16)
            wg = w[:, 2 * c0 + N_HEADS:].astype(BF16)
            bf = jnp.pad(b_forget[j], (0, PAGE - N_HEADS)).reshape(1, PAGE)
            wo = w_out_even[j].astype(BF16)
            gn = row2(ret_gn_w[j])
            wt = jnp.swapaxes(w, 0, 1)
            wqkvt = wt[:c0].astype(BF16)
            wlt = jnp.pad(wt[c0:c0 + N_HEADS], ((0, N_HEADS), (0, 0))).astype(BF16)
            bfc = jnp.pad(b_forget[j], (0, N_HEADS)).reshape(2 * N_HEADS, 1)

            fqt, fkt_all, fvt_all, fkb, fvtb, lft, rq, rk, rv, g = _even_in_prompt(
                hp.reshape(bp, lp, d), nw, wqkvt, wlt, wr, wg, bfc, cos_p, sin_p, stacked_kv,
                layer=j, n_layers=n_even, tm=tm_p)
            stacked_kv = (fkt_all, fvt_all)
            cum_rows = _seg_cumsum(lft.reshape(bp * N_HEADS, lp), seg=lp).reshape(bp, N_PAIRS, 2, lp)
            fq, fk, fv, lf, lft_s, rq_s, rk_s, rv_s, g_s = _even_in_sample(
                hs, nw, wf, wl, wr, wg, bf, cos_s, sin_s)
            cn = _seg_cumsum(lft_s, seg=ls).reshape(N_HEADS, bs, ls).transpose(1, 0, 2)
            cn_col = cn.reshape(bs, N_HEADS * ls, 1)
            cn_rows = jnp.pad(jnp.repeat(cn, ls, axis=1), ((0, 0), (0, 0), (0, PAGE - ls)))
            fo, fo_s = _fox(fqt, fkb, fvtb, cum_rows, fq, fk, fv, cn_col, cn_rows,
                            cache_kt, cache_vt, cache_ft, page_table, j, tq=512, t_new=ls)

            ro, s_p = _ret_prompt(rq, rk, rv, lg_tab, gn)
            hp = _even_out(hp, fo.reshape(rows_p, HEAD_W), ro.reshape(rows_p, HEAD_W),
                           g.reshape(rows_p, 2 * HEAD_W), wo, p_prompt.reshape(depth, rows_p, -1),
                           proj, pnw, pgw, layer=i, tm=tm_p)
            outs["fl_p"].append(jnp.swapaxes(lft, 1, 2))
            outs["rs_p"].append(s_p)

            ro, s_s = _ret_sample(rq_s, rk_s, rv_s, state_ret, lg_tab, gn, layer=j, length=ls)
            hs = _even_out(hs, fo_s, ro, g_s, wo, p_sample.reshape(depth, rows_s, -1),
                           proj, pnw, pgw, layer=i, tm=rows_s)
            outs["fk_s"].append(fk.reshape(bs, ls, N_HEADS, HEAD_DIM))
            outs["fv_s"].append(fv.reshape(bs, ls, N_HEADS, HEAD_DIM))
            outs["fl_s"].append(lf.reshape(bs, ls, N_HEADS))
            outs["rs_s"].append(s_s)
        else:
            final = i == depth - 1
            gw = jnp.concatenate([gate_r_w[j], gate_i_w[j]], axis=-1).astype(BF16)
            args = (nw, w_in_odd[j].astype(BF16), conv_w[j], row2(conv_b[j]), gw, row2(gate_r_b[j]),
                    row2(gate_i_b[j]), row2(lru_lambda[j]), w_out_odd[j].astype(BF16))
            tail = (proj, pnw, pgw)
            fnw = row2(final_norm_w)
            o, lh, lc = _odd_layer(hp.reshape(bp, lp, d), *args, p_prompt, *tail,
                                   jnp.zeros((bp, lru_w), F32), jnp.zeros((CONV_W - 1, bp, lru_w), F32),
                                   fnw, layer=i, nb=SEQ_PER_BLOCK, tl=64, final_norm=final)
            hp = o.reshape(rows_p, d)
            outs["lh_p"].append(lh)
            outs["lc_p"].append(jnp.swapaxes(lc, 0, 1))
            o, lh, lc = _odd_layer(hs.reshape(bs, ls, d), *args, p_sample, *tail,
                                   state_lru_h[j], jnp.swapaxes(state_lru_conv[j], 0, 1),
                                   fnw, layer=i, nb=SEQ_PER_BLOCK, tl=ls, final_norm=final)
            hs = o.reshape(rows_s, d)
            outs["lh_s"].append(lh)
            outs["lc_s"].append(jnp.swapaxes(lc, 0, 1))
            if final:
                y_p, y_s = hp.reshape(bp, lp, d), hs.reshape(bs, ls, d)

    st = lambda k: jnp.stack(outs[k])
    to_blhd = lambda t: t.reshape(n_even, bp, N_HEADS, HEAD_DIM, lp).transpose(0, 1, 4, 2, 3)
    return (y_p, y_s, to_blhd(stacked_kv[0]), to_blhd(stacked_kv[1]), st("fl_p"), st("rs_p"), st("lh_p"), st("lc_p"),
            st("fk_s"), st("fv_s"), st("fl_s"), st("rs_s"), st("lh_s"), st("lc_s"))
```

```python
import functools

import jax
import jax.numpy as jnp
import numpy as np
from jax import lax
from jax.experimental import pallas as pl
from jax.experimental.pallas import tpu as pltpu

F32 = jnp.float32
BF16 = jnp.bfloat16

HEAD_DIM = 64
HALF_DIM = HEAD_DIM // 2
N_HEADS = 8
HEAD_W = N_HEADS * HEAD_DIM
PAIR_W = 2 * HEAD_DIM
N_PAIRS = N_HEADS // 2
PAGE = 128
RET_CHUNK = 128
RET_GROUP = 8
ROPE_BASE = 10000.0
LRU_BLOCKS = 8
LRU_BW = 128
CONV_W = 4
SEQ_PER_BLOCK = 8
LRU_C = 8.0
NORM_EPS = 1e-6
NEG_INF = -1e30
QK_SCALE = HEAD_DIM ** -0.5
VMEM_LIMIT = 56 * 1024 * 1024


def _params(*sem):
    return pltpu.CompilerParams(dimension_semantics=sem, vmem_limit_bytes=VMEM_LIMIT)


def _mm(a, b):
    return jnp.dot(a, b, preferred_element_type=F32)


def _mm_nt(a, b):
    return lax.dot_general(a, b, (((1,), (1,)), ((), ())), preferred_element_type=F32)


def _rms(x, w):
    return (x * lax.rsqrt(jnp.mean(x * x, axis=-1, keepdims=True) + NORM_EPS)) * w


def _const_spec(shape):
    nd = len(shape)
    return pl.BlockSpec(shape, lambda *_: (0,) * nd)


def _swap_halves(x):
    n = x.shape[-1]
    lane = lax.broadcasted_iota(jnp.int32, x.shape, x.ndim - 1)
    return jnp.where(lane % HEAD_DIM < HALF_DIM,
                     pltpu.roll(x, n - HALF_DIM, x.ndim - 1),
                     pltpu.roll(x, HALF_DIM, x.ndim - 1))


def _retention_gate_proj(ub, wr_ref, wg_ref, cos_ref, sin_ref, rq_ref, rk_ref, rv_ref, g_ref):
    zr = _mm(ub, wr_ref[...])
    cos = cos_ref[...]
    sin = sin_ref[...]
    rq = zr[:, :HEAD_W]
    rk = zr[:, HEAD_W:2 * HEAD_W]
    rq_ref[...] = (rq * cos + _swap_halves(rq) * sin).astype(BF16)
    rk_ref[...] = (rk * cos + _swap_halves(rk) * sin) * QK_SCALE
    rv_ref[...] = zr[:, 2 * HEAD_W:].astype(BF16)
    g_ref[...] = _mm(ub, wg_ref[...])


N_EVEN_IN_PROMPT_INPUTS = 9


def _even_in_prompt_kernel(*refs):
    h_ref, nw_ref, wqkvt_ref, wlt_ref, wr_ref, wg_ref, bfc_ref, cos_ref, sin_ref = refs[:N_EVEN_IN_PROMPT_INPUTS]
    (fqt_ref, fkt_ref, fvt_ref, fkb_ref, fvtb_ref, lft_ref, rq_ref, rk_ref, rv_ref, g_ref) = refs[-10:]
    ub = _rms(h_ref[...], nw_ref[...]).astype(BF16)
    zt = _mm_nt(wqkvt_ref[...], ub)
    fqt_ref[...] = (zt[:HEAD_W] * QK_SCALE).astype(BF16)
    fkt = zt[HEAD_W:2 * HEAD_W]
    fkt_ref[...] = fkt
    fkb_ref[...] = fkt.T.astype(BF16)
    fvt = zt[2 * HEAD_W:]
    fvt_ref[...] = fvt
    fvtb_ref[...] = fvt.astype(BF16)
    lft_ref[...] = jax.nn.log_sigmoid(_mm_nt(wlt_ref[...], ub) + bfc_ref[...])[:N_HEADS]
    _retention_gate_proj(ub, wr_ref, wg_ref, cos_ref, sin_ref, rq_ref, rk_ref, rv_ref, g_ref)


def _even_in_prompt(h, nw, wqkvt, wlt, wr, wg, bfc, cos, sin, stacked_kv, *, layer, n_layers, tm):
    b, l, d = h.shape
    tok = lambda w: pl.BlockSpec((None, tm, w), lambda bi, i: (bi, i, 0))
    feat = lambda r: pl.BlockSpec((None, r, tm), lambda bi, i: (bi, 0, i))
    stacked = pl.BlockSpec((None, None, HEAD_W, tm), lambda bi, i: (layer, bi, 0, i))
    pos = pl.BlockSpec((tm, HEAD_W), lambda bi, i: (i, 0))
    sd = jax.ShapeDtypeStruct
    consts = [nw, wqkvt, wlt, wr, wg, bfc]
    ins = [h, *consts, cos, sin]
    in_specs = [tok(d)] + [_const_spec(c.shape) for c in consts] + [pos, pos]
    assert len(ins) == N_EVEN_IN_PROMPT_INPUTS
    aliases = {}
    if stacked_kv is not None:
        aliases = {len(ins): 1, len(ins) + 1: 2}
        ins += list(stacked_kv)
        in_specs += [pl.BlockSpec(memory_space=pl.ANY)] * 2
    return pl.pallas_call(
        _even_in_prompt_kernel,
        grid=(b, l // tm),
        in_specs=in_specs,
        out_specs=[feat(HEAD_W), stacked, stacked, tok(HEAD_W), feat(HEAD_W), feat(N_HEADS),
                   tok(HEAD_W), tok(HEAD_W), tok(HEAD_W), tok(2 * HEAD_W)],
        out_shape=[sd((b, HEAD_W, l), BF16), sd((n_layers, b, HEAD_W, l), F32), sd((n_layers, b, HEAD_W, l), F32),
                   sd((b, l, HEAD_W), BF16), sd((b, HEAD_W, l), BF16), sd((b, N_HEADS, l), F32),
                   sd((b, l, HEAD_W), BF16), sd((b, l, HEAD_W), F32), sd((b, l, HEAD_W), BF16),
                   sd((b, l, 2 * HEAD_W), F32)],
        input_output_aliases=aliases,
        compiler_params=_params("parallel", "parallel"),
        name="even_in_prompt",
    )(*ins)


def _even_in_sample_kernel(h_ref, nw_ref, wf_ref, wl_ref, wr_ref, wg_ref, bf_ref, cos_ref, sin_ref,
                           fq_ref, fk_ref, fv_ref, lf_ref, lft_ref, rq_ref, rk_ref, rv_ref, g_ref):
    ub = _rms(h_ref[...], nw_ref[...]).astype(BF16)
    zf = _mm(ub, wf_ref[...])
    fq_ref[...] = (zf[:, :HEAD_W] * QK_SCALE).astype(BF16)
    fk_ref[...] = zf[:, HEAD_W:2 * HEAD_W]
    fv_ref[...] = zf[:, 2 * HEAD_W:]
    logf = jax.nn.log_sigmoid(_mm(ub, wl_ref[...]) + bf_ref[...])
    lf_ref[...] = logf[:, :N_HEADS]
    lft_ref[...] = logf.T[:N_HEADS]
    _retention_gate_proj(ub, wr_ref, wg_ref, cos_ref, sin_ref, rq_ref, rk_ref, rv_ref, g_ref)


def _even_in_sample(h, nw, wf, wl, wr, wg, bf, cos, sin):
    rows = h.shape[0]
    sd = jax.ShapeDtypeStruct
    ins = [h, nw, wf, wl, wr, wg, bf, cos, sin]
    outs = [sd((rows, HEAD_W), BF16), sd((rows, HEAD_W), F32), sd((rows, HEAD_W), F32),
            sd((rows, N_HEADS), F32), sd((N_HEADS, rows), F32),
            sd((rows, HEAD_W), BF16), sd((rows, HEAD_W), F32), sd((rows, HEAD_W), BF16),
            sd((rows, 2 * HEAD_W), F32)]
    return pl.pallas_call(
        _even_in_sample_kernel,
        grid=(1,),
        in_specs=[_const_spec(a.shape) for a in ins],
        out_specs=[_const_spec(o.shape) for o in outs],
        out_shape=outs,
        compiler_params=_params("arbitrary"),
        name="even_in_sample",
    )(*ins)


def _seg_cumsum_kernel(x_ref, o_ref, *, seg):
    x = x_ref[...]
    pos = lax.broadcasted_iota(jnp.int32, x.shape, 1) % seg
    s = 1
    while s < seg:
        x = x + jnp.where(pos >= s, pltpu.roll(x, s, 1), 0.0)
        s *= 2
    o_ref[...] = x


def _seg_cumsum(x, *, seg):
    r, n = x.shape
    return pl.pallas_call(
        functools.partial(_seg_cumsum_kernel, seg=seg),
        grid=(r // 8,),
        in_specs=[pl.BlockSpec((8, n), lambda i: (i, 0))],
        out_specs=pl.BlockSpec((8, n), lambda i: (i, 0)),
        out_shape=jax.ShapeDtypeStruct((r, n), F32),
        compiler_params=_params("parallel"),
        name="seg_cumsum",
    )(x)


def _pair_masks(shape):
    lane = lax.broadcasted_iota(jnp.int32, shape, len(shape) - 1)
    return lane < HEAD_DIM


def _split3(x):
    hi = x.astype(BF16).astype(F32)
    rest = x - hi
    mid = rest.astype(BF16).astype(F32)
    return hi, mid, rest - mid


AUG_ROWS = 8
SUM_ROWS = 16


def _fox_prompt_body(qi, qt_ref, k_ref, vt_ref, cr_ref, call_ref, o_ref, kaug_scr, *, tq, tk, with_step):
    seq = k_ref.shape[0]

    @pl.when(qi == 0)
    def _():
        c_all = call_ref[...]
        one = jnp.ones((1, seq), F32)
        zero_row = jnp.zeros((1, seq), F32)
        rows = []
        for h in range(2):
            rows += list(_split3(-c_all[h:h + 1, :])) + [one] * 3 + [zero_row] * (AUG_ROWS - 6)
        aug_t = jnp.concatenate(rows + [jnp.zeros((PAIR_W - 2 * AUG_ROWS, seq), F32)], axis=0)
        kaug_scr[:, :PAIR_W] = k_ref[...]
        kaug_scr[:, PAIR_W:] = aug_t.T.astype(BF16)

    qt = qt_ref[...]
    row = lax.broadcasted_iota(jnp.int32, qt.shape, 0)
    zero = jnp.zeros_like(qt)
    cr = cr_ref[...]
    w_heads = []
    for h in range(2):
        head_rows = (row < HEAD_DIM) if h == 0 else (row >= HEAD_DIM)
        pieces = _split3(cr[h:h + 1, :])
        bias = jnp.zeros(qt.shape, F32)
        for i in range(3):
            bias = jnp.where(row == h * AUG_ROWS + i, 1.0, bias)
            bias = jnp.where(row == h * AUG_ROWS + 3 + i, pieces[i], bias)
        w_heads.append(jnp.concatenate([jnp.where(head_rows, qt, zero), bias.astype(BF16)], axis=0))
    key_l = lax.broadcasted_iota(jnp.int32, (tk, tq), 0)
    qry_l = lax.broadcasted_iota(jnp.int32, (tk, tq), 1)
    ones = jnp.ones((SUM_ROWS, tk), BF16)

    def scores_of(j):
        kc = kaug_scr[j * tk:(j + 1) * tk, :]
        return tuple(_mm(kc, w_heads[h]) for h in range(2))

    def values_of(j, p_heads):
        start = j * tk
        return [_mm(jnp.concatenate([vt_ref[h * HEAD_DIM:(h + 1) * HEAD_DIM, pl.ds(start, tk)], ones], axis=0),
                    p_heads[h]) for h in range(2)]

    def softmax(s, m, diagonal):
        if diagonal:
            s = jnp.where(key_l <= qry_l, s, NEG_INF)
        m_new = jnp.maximum(m, jnp.max(s, axis=0, keepdims=True))
        return m_new, jnp.exp(m - m_new), jnp.exp(s - m_new).astype(BF16)

    def run(n_blocks, other):
        m_heads = [jnp.full((1, tq), NEG_INF, F32)] * 2
        acc_heads = [jnp.zeros((HEAD_DIM + SUM_ROWS, tq), F32)] * 2
        next(other)
        s_cur = scores_of(0)
        for j in range(n_blocks):
            s_next = scores_of(j + 1) if j + 1 < n_blocks else None
            if j == 0:
                next(other)
            p_cur, alphas = [], []
            for h in range(2):
                m_heads[h], alpha, p = softmax(s_cur[h], m_heads[h], j + 1 == n_blocks)
                alphas.append(alpha)
                p_cur.append(p)
            pv = values_of(j, p_cur)
            if j == 0:
                for _ in other:
                    pass
            acc_heads = [alphas[h] * acc_heads[h] + pv[h] for h in range(2)]
            s_cur = s_next
        out = [acc[:HEAD_DIM, :] / acc[HEAD_DIM:HEAD_DIM + 1, :] for acc in acc_heads]
        o_ref[...] = jnp.concatenate(out, axis=0).T

    def branch(q_block):
        run(q_block + 1, with_step(q_block))

    for q_block in range(seq // tq):
        pl.when(qi == q_block)(functools.partial(branch, q_block))


def _ret_tables(lg_ref, length):
    n = RET_CHUNK
    lg_a = lg_ref[0:1, :]
    lg_b = lg_ref[1:2, :]
    lg_m = lg_ref[2:3, :]
    row = lax.broadcasted_iota(jnp.int32, (n, n), 0)
    col = lax.broadcasted_iota(jnp.int32, (n, n), 1)
    diff = row - col
    lower = diff >= 0
    dpos = jnp.where(lower, diff, 0).astype(F32)
    decay = (jnp.where(lower, jnp.exp(dpos * lg_a), 0.0), jnp.where(lower, jnp.exp(dpos * lg_b), 0.0))
    rowf = row.astype(F32)
    q_dec = jnp.exp((rowf + 1.0) * lg_m)
    k_dec = jnp.exp((length - 1.0 - rowf) * lg_m)
    s_dec = jnp.exp(float(length) * jnp.where(row < HEAD_DIM, lg_a, lg_b))
    same_head = (row < HEAD_DIM) == (col < HEAD_DIM)
    return decay, q_dec, k_dec, s_dec, same_head


def _group_norm(o, gn_w):
    first = _pair_masks(o.shape)
    inv = 1.0 / HEAD_DIM
    s0 = jnp.sum(jnp.where(first, o, 0.0), axis=-1, keepdims=True)
    s1 = jnp.sum(jnp.where(first, 0.0, o), axis=-1, keepdims=True)
    d = o - jnp.where(first, s0, s1) * inv
    dd = d * d
    v0 = jnp.sum(jnp.where(first, dd, 0.0), axis=-1, keepdims=True)
    v1 = jnp.sum(jnp.where(first, 0.0, dd), axis=-1, keepdims=True)
    var = jnp.where(first, v0, v1) * inv
    return (d * lax.rsqrt(var + NORM_EPS)) * gn_w


def _ret_prompt_kernel(q_ref, k_ref, v_ref, lg_ref, gn_ref, o_ref, s_ref, *, n_chunks):
    tables = _ret_tables(lg_ref, RET_CHUNK)
    gn_w = gn_ref[...]

    decay, q_dec, k_dec, s_dec, same_head = tables
    group = RET_GROUP

    def body(cc, state):
        rows = [pl.ds(pl.multiple_of((cc * group + g) * RET_CHUNK, RET_CHUNK), RET_CHUNK)
                for g in range(group)]
        qs = [q_ref[r, :] for r in rows]
        ks = [k_ref[r, :] for r in rows]
        vs = [v_ref[r, :] for r in rows]
        first = _pair_masks(qs[0].shape)
        zero = jnp.zeros_like(qs[0])
        kbs = [k.astype(BF16) for k in ks]
        scores = [[_mm_nt(jnp.where(first, q, zero) if h == 0 else jnp.where(first, zero, q), kb)
                   for h in range(2)] for q, kb in zip(qs, kbs)]
        incs = [jnp.where(same_head, _mm((k * k_dec).T.astype(BF16), v), 0.0) for k, v in zip(ks, vs)]
        intra = [[_mm((sc[h] * decay[h]).astype(BF16), v) for h in range(2)] for sc, v in zip(scores, vs)]
        states = []
        for g in range(group):
            states.append(state)
            state = state * s_dec + incs[g]
        cross = [_mm(q, s.astype(BF16)) * q_dec for q, s in zip(qs, states)]
        for g in range(group):
            o = jnp.where(first, intra[g][0], intra[g][1]) + cross[g]
            o_ref[rows[g], :] = _group_norm(o, gn_w)
        return state

    state = lax.fori_loop(0, n_chunks // group, body, jnp.zeros((PAIR_W, PAIR_W), F32))
    s_ref[0] = state[:HEAD_DIM, :HEAD_DIM]
    s_ref[1] = state[HEAD_DIM:, HEAD_DIM:]


def _ret_prompt(rq, rk, rv, lg_tab, gn_w):
    b, l, _ = rq.shape
    seq = pl.BlockSpec((None, l, PAIR_W), lambda bi, hp: (bi, 0, hp))
    return pl.pallas_call(
        functools.partial(_ret_prompt_kernel, n_chunks=l // RET_CHUNK),
        grid=(b, N_PAIRS),
        in_specs=[seq, seq, seq,
                  pl.BlockSpec((None, 8, PAIR_W), lambda bi, hp: (hp, 0, 0)),
                  pl.BlockSpec((1, PAIR_W), lambda bi, hp: (0, hp))],
        out_specs=[seq, pl.BlockSpec((None, 2, HEAD_DIM, HEAD_DIM), lambda bi, hp: (bi, hp, 0, 0))],
        out_shape=[jax.ShapeDtypeStruct((b, l, HEAD_W), F32),
                   jax.ShapeDtypeStruct((b, N_HEADS, HEAD_DIM, HEAD_DIM), F32)],
        compiler_params=_params("parallel", "parallel"),
        name="ret_prompt",
    )(rq, rk, rv, lg_tab, gn_w)


def _ret_sample_kernel(q_ref, k_ref, v_ref, s0_ref, lg_ref, gn_ref, o_ref, s_ref, *, length):
    pad = RET_CHUNK - length
    zrow_f = jnp.zeros((pad, PAIR_W), F32)
    zblk = jnp.zeros((HEAD_DIM, HEAD_DIM), F32)
    pairs = range(N_PAIRS)
    lanes = [slice(hp * PAIR_W, (hp + 1) * PAIR_W) for hp in pairs]
    tables = [_ret_tables(lg_ref.at[hp], length) for hp in pairs]
    qs = [jnp.concatenate([q_ref[:, ln].astype(F32), zrow_f], axis=0).astype(BF16) for ln in lanes]
    ks = [jnp.concatenate([k_ref[:, ln], zrow_f], axis=0) for ln in lanes]
    vs = [jnp.concatenate([v_ref[:, ln].astype(F32), zrow_f], axis=0).astype(BF16) for ln in lanes]
    states = [jnp.concatenate([jnp.concatenate([s0_ref[2 * hp], zblk], axis=1),
                               jnp.concatenate([zblk, s0_ref[2 * hp + 1]], axis=1)], axis=0) for hp in pairs]
    first = _pair_masks(qs[0].shape)
    zero = jnp.zeros_like(qs[0])
    kbs = [k.astype(BF16) for k in ks]
    scores = [[_mm_nt(jnp.where(first, q, zero) if h == 0 else jnp.where(first, zero, q), kb)
               for h in range(2)] for q, kb in zip(qs, kbs)]
    cross = [_mm(q, s.astype(BF16)) * t[1] for q, s, t in zip(qs, states, tables)]
    incs = [jnp.where(t[4], _mm((k * t[2]).T.astype(BF16), v), 0.0) for k, v, t in zip(ks, vs, tables)]
    intra = [[_mm((sc[h] * t[0][h]).astype(BF16), v) for h in range(2)]
             for sc, v, t in zip(scores, vs, tables)]
    for hp in pairs:
        o = jnp.where(first, intra[hp][0], intra[hp][1]) + cross[hp]
        o_ref[:, lanes[hp]] = _group_norm(o[:length], gn_ref[:, lanes[hp]])
        state = states[hp] * tables[hp][3] + incs[hp]
        s_ref[2 * hp] = state[:HEAD_DIM, :HEAD_DIM]
        s_ref[2 * hp + 1] = state[HEAD_DIM:, HEAD_DIM:]


def _ret_sample(rq, rk, rv, state, lg_tab, gn_w, *, layer, length):
    rows = rq.shape[0]
    b = rows // length
    tok = pl.BlockSpec((length, HEAD_W), lambda bi: (bi, 0))
    st_in = pl.BlockSpec((None, None, N_HEADS, HEAD_DIM, HEAD_DIM), lambda bi: (layer, bi, 0, 0, 0))
    st = pl.BlockSpec((None, N_HEADS, HEAD_DIM, HEAD_DIM), lambda bi: (bi, 0, 0, 0))
    return pl.pallas_call(
        functools.partial(_ret_sample_kernel, length=length),
        grid=(b,),
        in_specs=[tok, tok, tok, st_in, _const_spec(lg_tab.shape), _const_spec(gn_w.shape)],
        out_specs=[tok, st],
        out_shape=[jax.ShapeDtypeStruct((rows, HEAD_W), F32),
                   jax.ShapeDtypeStruct(state.shape[1:], F32)],
        compiler_params=_params("parallel"),
        name="ret_sample",
    )(rq, rk, rv, state, lg_tab, gn_w)


def _ple(h1, p, proj, pnw, pgw):
    e = _rms(_mm(p.astype(BF16), proj), pnw)
    return h1 + jax.nn.sigmoid(_mm(h1.astype(BF16), pgw)) * e


def _silu(g):
    return g * jax.nn.sigmoid(g)


def _even_out_kernel(h_ref, fo_ref, ro_ref, g_ref, wo_ref, p_ref, proj_ref, pnw_ref, pgw_ref, o_ref):
    sg = _silu(g_ref[...])
    cat_f = (fo_ref[...] * sg[:, :HEAD_W]).astype(BF16)
    cat_r = (ro_ref[...] * sg[:, HEAD_W:]).astype(BF16)
    mix = _mm(cat_f, wo_ref[:HEAD_W, :]) + _mm(cat_r, wo_ref[HEAD_W:, :])
    o_ref[...] = _ple(h_ref[...] + mix, p_ref[...], proj_ref[...], pnw_ref[...], pgw_ref[...])


def _even_out(h, fo, ro, g, wo, p, proj, pnw, pgw, *, layer, tm):
    rows, d = h.shape
    row = lambda w: pl.BlockSpec((tm, w), lambda i: (i, 0))
    return pl.pallas_call(
        _even_out_kernel,
        grid=(rows // tm,),
        in_specs=[row(d), row(HEAD_W), row(HEAD_W), row(2 * HEAD_W), _const_spec(wo.shape),
                  pl.BlockSpec((None, tm, p.shape[-1]), lambda i: (layer, i, 0)),
                  _const_spec(proj.shape), _const_spec(pnw.shape), _const_spec(pgw.shape)],
        out_specs=row(d),
        out_shape=jax.ShapeDtypeStruct((rows, d), F32),
        compiler_params=_params("parallel"),
        name="even_out",
    )(h, fo, ro, g, wo, p, proj, pnw, pgw)


def _odd_kernel(h_ref, nw_ref, win_ref, cw_ref, cb_ref, gw_ref, grb_ref, gib_ref, lam_ref, wout_ref,
                p_ref, proj_ref, pnw_ref, pgw_ref, h0_ref, cbuf_ref, fnw_ref,
                o_ref, lruh_ref, lruc_ref,
                ext_scr, carry_scr, y_scr, sg_scr, *, tl, final_norm):
    nb, _, d = h_ref.shape
    w = wout_ref.shape[0]
    rows = nb * tl
    halo = (CONV_W - 1) * nb

    @pl.when(pl.program_id(1) == 0)
    def _():
        carry_scr[...] = h0_ref[...]
        ext_scr[0:halo, :] = cbuf_ref[...].reshape(halo, w)

    x = jnp.swapaxes(h_ref[...], 0, 1).reshape(rows, d)
    z = _mm(_rms(x, nw_ref[...]).astype(BF16), win_ref[...])
    xb = z[:, :w]
    sg_scr[...] = _silu(z[:, w:])
    ext_scr[halo:halo + rows, :] = xb
    xc = xb * cw_ref[CONV_W - 1:CONV_W, :] + cb_ref[...]
    for kk in range(1, CONV_W):
        xc = xc + ext_scr[halo - kk * nb:halo - kk * nb + rows, :] * cw_ref[CONV_W - 1 - kk:CONV_W - kk, :]
    last_steps = ext_scr[rows:rows + halo, :]
    lruc_ref[...] = last_steps.reshape(CONV_W - 1, nb, w)
    ext_scr[0:halo, :] = last_steps

    sp = jax.nn.softplus(-lam_ref[...])
    for n in range(LRU_BLOCKS):
        lanes = slice(n * LRU_BW, (n + 1) * LRU_BW)
        xn = xc[:, lanes]
        zz = _mm(xn.astype(BF16), gw_ref[n])
        r = jax.nn.sigmoid(zz[:, :LRU_BW] + grb_ref[:, lanes])
        gi = jax.nn.sigmoid(zz[:, LRU_BW:] + gib_ref[:, lanes])
        log_a = (-LRU_C * r) * sp[:, lanes]
        a = jnp.exp(log_a)
        one_minus = -jnp.tanh(log_a) * (a * a + 1.0)
        b = (jnp.sqrt(one_minus) * gi) * xn
        hs = carry_scr[:, lanes]
        for t in range(tl):
            step = slice(t * nb, (t + 1) * nb)
            hs = a[step] * hs + b[step]
            y_scr[step, lanes] = hs
        carry_scr[:, lanes] = hs
        lruh_ref[:, lanes] = hs

    mix = _mm((y_scr[...] * sg_scr[...]).astype(BF16), wout_ref[...])
    p = jnp.swapaxes(p_ref[...], 0, 1).reshape(rows, -1)
    h2 = _ple(x + mix, p, proj_ref[...], pnw_ref[...], pgw_ref[...])
    if final_norm:
        h2 = _rms(h2, fnw_ref[...])
    o_ref[...] = jnp.swapaxes(h2.reshape(tl, nb, d), 0, 1)


def _odd_layer(h, nw, win, cw, cb, gw, grb, gib, lam, wout, p, proj, pnw, pgw, h0, cbuf, fnw,
               *, layer, nb, tl, final_norm):
    b, l, d = h.shape
    w = wout.shape[0]
    seq = lambda width: pl.BlockSpec((nb, tl, width), lambda bi, li: (bi, li, 0))
    state = pl.BlockSpec((nb, w), lambda bi, li: (bi, 0))
    conv = pl.BlockSpec((CONV_W - 1, nb, w), lambda bi, li: (0, bi, 0))
    consts = [nw, win, cw, cb, gw, grb, gib, lam, wout]
    tail = [proj, pnw, pgw]
    rows = nb * tl
    return pl.pallas_call(
        functools.partial(_odd_kernel, tl=tl, final_norm=final_norm),
        grid=(b // nb, l // tl),
        in_specs=[seq(d)] + [_const_spec(c.shape) for c in consts]
                 + [pl.BlockSpec((None, nb, tl, p.shape[-1]), lambda bi, li: (layer, bi, li, 0))]
                 + [_const_spec(c.shape) for c in tail] + [state, conv, _const_spec(fnw.shape)],
        out_specs=[seq(d), state, conv],
        out_shape=[jax.ShapeDtypeStruct((b, l, d), F32),
                   jax.ShapeDtypeStruct((b, w), F32),
                   jax.ShapeDtypeStruct((CONV_W - 1, b, w), F32)],
        scratch_shapes=[pltpu.VMEM(((CONV_W - 1) * nb + rows, w), F32), pltpu.VMEM((nb, w), F32),
                        pltpu.VMEM((rows, w), F32), pltpu.VMEM((rows, w), F32)],
        compiler_params=_params("parallel", "arbitrary"),
        name="odd_layer",
    )(h, *consts, p, *tail, h0, cbuf, fnw)


def _fox_sample_body(c, n_steps, q_ref, kn_ref, vn_ref, cnc_ref, cnr_ref, tri_ref, k_pages, v_pages, f_pages,
                     o_ref, qbd_scr, m_scr, l_scr, acc_scr, rel_scr, *, t_new):
    pages_per_step = len(k_pages)
    cn_col = cnc_ref[...]

    def start():
        rel_scr[...] = jnp.zeros_like(rel_scr)
        q = q_ref[...].astype(F32)
        lane_head = lax.broadcasted_iota(jnp.int32, q.shape, 1) // HEAD_DIM
        zero = jnp.zeros_like(q)
        for h in range(N_HEADS):
            qbd_scr[h * t_new:(h + 1) * t_new, :] = jnp.where(lane_head == h, q, zero)
        m_scr[...] = jnp.full(m_scr.shape, NEG_INF, F32)
        l_scr[...] = jnp.zeros_like(l_scr)
        acc_scr[...] = jnp.zeros_like(acc_scr)

    if c == 0:
        start()
    qbd = qbd_scr[...].astype(BF16)

    def online(s, pv_fn):
        m = m_scr[...]
        m_new = jnp.maximum(m, jnp.max(s, axis=-1, keepdims=True))
        alpha = jnp.exp(m - m_new)
        p = jnp.exp(s - m_new)
        l_scr[...] = alpha * l_scr[...] + jnp.sum(p, axis=-1, keepdims=True)
        acc_scr[...] = alpha * acc_scr[...] + pv_fn(p.astype(BF16))
        m_scr[...] = m_new

    xs = jnp.concatenate([f[...] for f in f_pages], axis=0)
    hi = xs.astype(BF16)
    rest = xs - hi.astype(F32)
    mid = rest.astype(BF16)
    lo = (rest - mid.astype(F32)).astype(BF16)
    tri = tri_ref[...]
    sums = (_mm(hi, tri) + _mm(mid, tri)) + _mm(lo, tri)
    after = rel_scr[...]
    bias = [None] * pages_per_step
    for i in reversed(range(pages_per_step)):
        rows = slice(i * N_HEADS, (i + 1) * N_HEADS)
        rel = sums[rows, :PAGE] + after
        after = after + sums[rows, PAGE:]
        bias[i] = jnp.concatenate(
            [jnp.broadcast_to(rel[h:h + 1, :], (t_new, PAGE)) for h in range(N_HEADS)], axis=0)
    rel_scr[...] = after

    kt = jnp.concatenate([k[...].astype(BF16) for k in k_pages], axis=1)
    vt = jnp.concatenate([v[...].astype(BF16) for v in v_pages], axis=1)
    s = (_mm(qbd, kt) + cn_col) + jnp.concatenate(bias, axis=1)
    yield
    m = m_scr[...]
    m_new = jnp.maximum(m, jnp.max(s, axis=-1, keepdims=True))
    alpha = jnp.exp(m - m_new)
    p = jnp.exp(s - m_new)
    yield
    l_scr[...] = alpha * l_scr[...] + jnp.sum(p, axis=-1, keepdims=True)
    acc_scr[...] = alpha * acc_scr[...] + _mm_nt(p.astype(BF16), vt)
    m_scr[...] = m_new

    def finish():
        zpad = jnp.zeros((PAGE - t_new, HEAD_W), F32)
        kn = jnp.concatenate([kn_ref[...], zpad], axis=0).astype(BF16)
        vn = jnp.concatenate([vn_ref[...], zpad], axis=0).astype(BF16)
        s = (_mm_nt(qbd, kn) + cn_col) - cnr_ref[...]
        t_query = lax.broadcasted_iota(jnp.int32, s.shape, 0) % t_new
        t_key = lax.broadcasted_iota(jnp.int32, s.shape, 1)
        s = jnp.where(t_query >= t_key, s, NEG_INF)
        online(s, lambda p: _mm(p, vn))
        o = acc_scr[...] / l_scr[...]
        lane_head = lax.broadcasted_iota(jnp.int32, (t_new, HEAD_W), 1) // HEAD_DIM
        out = jnp.zeros((t_new, HEAD_W), F32)
        for h in range(N_HEADS):
            out = jnp.where(lane_head == h, o[h * t_new:(h + 1) * t_new, :], out)
        o_ref[...] = out

    if c == n_steps - 1:
        finish()


N_FOX_PROMPT_INPUTS = 5
N_FOX_SAMPLE_INPUTS = 6


N_PAGE_ARRAYS = 3
N_PAGE_SLOTS = 2


def _fox_kernel(tbl_ref, *refs, tq, tk, t_new, pages_per_step, layer):
    prompt_in = refs[:N_FOX_PROMPT_INPUTS]
    sample_in = refs[N_FOX_PROMPT_INPUTS:N_FOX_PROMPT_INPUTS + N_FOX_SAMPLE_INPUTS]
    rest = refs[N_FOX_PROMPT_INPUTS + N_FOX_SAMPLE_INPUTS:]
    caches = rest[:N_PAGE_ARRAYS]
    o_prompt_ref, o_sample_ref, kaug_scr = rest[N_PAGE_ARRAYS:N_PAGE_ARRAYS + 3]
    sample_scr = rest[N_PAGE_ARRAYS + 3:N_PAGE_ARRAYS + 8]
    page_bufs = rest[N_PAGE_ARRAYS + 8:2 * N_PAGE_ARRAYS + 8]
    sems = rest[2 * N_PAGE_ARRAYS + 8]
    n_steps = prompt_in[1].shape[0] // tq
    seq = pl.program_id(0) * N_PAIRS + pl.program_id(1)
    n_seqs = pl.num_programs(0) * N_PAIRS

    def page_copies(of_seq, step):
        slot = step % N_PAGE_SLOTS
        copies = []
        for i in range(pages_per_step):
            page = tbl_ref[of_seq, (n_steps - 1 - step) * pages_per_step + i]
            for a in range(N_PAGE_ARRAYS):
                copies.append(pltpu.make_async_copy(
                    caches[a].at[layer, page], page_bufs[a].at[slot, i], sems.at[slot, a]))
        return copies

    def start(copies):
        for cp in copies:
            cp.start()

    def sample_step(c):
        if c == 0:
            pl.when(seq == 0)(lambda: start(page_copies(seq, 0)))
        if c + 1 < n_steps:
            start(page_copies(seq, c + 1))
        else:
            pl.when(seq + 1 < n_seqs)(lambda: start(page_copies(seq + 1, 0)))
        for cp in page_copies(seq, c):
            cp.wait()
        slot = c % N_PAGE_SLOTS
        views = [[buf.at[slot, i] for i in range(pages_per_step)] for buf in page_bufs]
        return _fox_sample_body(c, n_steps, *sample_in, *views, o_sample_ref, *sample_scr, t_new=t_new)

    _fox_prompt_body(pl.program_id(2), *prompt_in, o_prompt_ref, kaug_scr, tq=tq, tk=tk, with_step=sample_step)


def _fox(qt, kb, vtb, cum_rows, q, k_new, v_new, cn_col, cn_rows, cache_kt, cache_vt, cache_ft, table, layer,
         *, tq, t_new):
    b, l, _ = kb.shape
    bs, n_pages = table.shape
    n_steps = l // tq
    assert bs == b * N_PAIRS and n_pages % n_steps == 0, "one sample sequence per prompt (batch, head pair)"
    pages_per_step = n_pages // n_steps
    n_rows = N_HEADS * t_new
    seq_of = lambda bi, hp: bi * N_PAIRS + hp
    tok = pl.BlockSpec((t_new, HEAD_W), lambda bi, hp, qi, tbl: (seq_of(bi, hp), 0))

    assert n_steps % N_PAGE_SLOTS == 0, "a sequence must start on buffer slot 0"
    caches = [cache_kt, cache_vt, cache_ft]
    page_bufs = [pltpu.VMEM((N_PAGE_SLOTS, pages_per_step) + c.shape[2:], F32) for c in caches]
    pos = jnp.arange(PAGE)
    later = (pos[:, None] > pos[None, :]).astype(BF16)
    tri = jnp.concatenate([later, jnp.ones((PAGE, PAGE), BF16)], axis=1)
    prompt_specs = [pl.BlockSpec((None, PAIR_W, tq), lambda bi, hp, qi, tbl: (bi, hp, qi)),
                    pl.BlockSpec((None, l, PAIR_W), lambda bi, hp, qi, tbl: (bi, 0, hp)),
                    pl.BlockSpec((None, PAIR_W, l), lambda bi, hp, qi, tbl: (bi, hp, 0)),
                    pl.BlockSpec((None, None, 2, tq), lambda bi, hp, qi, tbl: (bi, hp, 0, qi)),
                    pl.BlockSpec((None, None, 2, l), lambda bi, hp, qi, tbl: (bi, hp, 0, 0))]
    sample_specs = [tok, tok, tok,
                    pl.BlockSpec((None, n_rows, 1), lambda bi, hp, qi, tbl: (seq_of(bi, hp), 0, 0)),
                    pl.BlockSpec((None, n_rows, PAGE), lambda bi, hp, qi, tbl: (seq_of(bi, hp), 0, 0)),
                    pl.BlockSpec(tri.shape, lambda bi, hp, qi, tbl: (0, 0))]
    assert len(prompt_specs) == N_FOX_PROMPT_INPUTS and len(sample_specs) == N_FOX_SAMPLE_INPUTS
    return pl.pallas_call(
        functools.partial(_fox_kernel, tq=tq, tk=tq, t_new=t_new, pages_per_step=pages_per_step, layer=layer),
        grid_spec=pltpu.PrefetchScalarGridSpec(
            num_scalar_prefetch=1,
            grid=(b, N_PAIRS, n_steps),
            in_specs=prompt_specs + sample_specs + [pl.BlockSpec(memory_space=pl.ANY)] * N_PAGE_ARRAYS,
            out_specs=[pl.BlockSpec((None, tq, PAIR_W), lambda bi, hp, qi, tbl: (bi, qi, hp)), tok],
            scratch_shapes=[pltpu.VMEM((l, 2 * PAIR_W), BF16),
                            pltpu.VMEM((n_rows, HEAD_W), F32), pltpu.VMEM((n_rows, 1), F32),
                            pltpu.VMEM((n_rows, 1), F32), pltpu.VMEM((n_rows, HEAD_W), F32),
                            pltpu.VMEM((N_HEADS, PAGE), F32)]
                           + page_bufs + [pltpu.SemaphoreType.DMA((N_PAGE_SLOTS, N_PAGE_ARRAYS))]),
        out_shape=[jax.ShapeDtypeStruct((b, l, HEAD_W), F32),
                   jax.ShapeDtypeStruct((bs * t_new, HEAD_W), F32)],
        compiler_params=_params("arbitrary", "arbitrary", "arbitrary"),
        name="fox",
    )(table, qt, kb, vtb, cum_rows, cum_rows, q, k_new, v_new, cn_col, cn_rows, tri, *caches)


def _rope_tables(pos):
    inv = ROPE_BASE ** (-np.arange(HALF_DIM, dtype=np.float64) / HALF_DIM)
    ang = np.asarray(pos, np.float64)[:, None] * inv[None, :]
    cos_head = np.concatenate([np.cos(ang), np.cos(ang)], axis=-1)
    sin_head = np.concatenate([-np.sin(ang), np.sin(ang)], axis=-1)
    return (jnp.asarray(np.tile(cos_head, (1, N_HEADS)), F32), jnp.asarray(np.tile(sin_head, (1, N_HEADS)), F32))


def _log_gamma_table(log_gamma):
    pairs = log_gamma.reshape(N_PAIRS, 2)
    ones = jnp.ones((N_PAIRS, 1, PAIR_W), F32)
    first = pairs[:, 0][:, None, None] * ones
    second = pairs[:, 1][:, None, None] * ones
    mix = jnp.repeat(pairs, HEAD_DIM, axis=1)[:, None, :]
    return jnp.concatenate([first, second, mix, jnp.zeros((N_PAIRS, 5, PAIR_W), F32)], axis=1)


def kernel(x_prompt, x_sample, cache_fox_k, cache_fox_v, cache_fox_logf, page_table, state_ret, state_lru_h, state_lru_conv, p_prompt, p_sample, norm_w, w_in_even, b_forget, ret_gn_w, w_out_even, w_in_odd, conv_w, conv_b, gate_r_w, gate_r_b, gate_i_w, gate_i_b, lru_lambda, w_out_odd, ple_proj, ple_norm_w, ple_gate_w, final_norm_w):
    bp, lp, d = x_prompt.shape
    bs, ls, _ = x_sample.shape
    depth = norm_w.shape[0]
    n_pool = cache_fox_k.shape[1]
    n_pages = page_table.shape[1]
    past_len = n_pages * PAGE
    lru_w = w_out_odd.shape[1]
    rows_p, rows_s = bp * lp, bs * ls
    tm_p = 512

    log_gamma = jnp.log1p(-jnp.exp2(-5.0 - jnp.arange(N_HEADS, dtype=F32)))
    lg_tab = _log_gamma_table(log_gamma)
    cos_p, sin_p = _rope_tables(np.arange(lp))
    cos_s, sin_s = _rope_tables(np.tile(past_len + np.arange(ls), bs))

    cache_kt = cache_fox_k.transpose(0, 1, 3, 4, 2).reshape(-1, n_pool, HEAD_W, PAGE)
    cache_vt = cache_fox_v.transpose(0, 1, 3, 4, 2).reshape(-1, n_pool, HEAD_W, PAGE)
    cache_ft = jnp.swapaxes(cache_fox_logf, 2, 3)

    row2 = lambda v: v.reshape(1, -1)
    hp = x_prompt.reshape(rows_p, d)
    hs = x_sample.reshape(rows_s, d)
    outs = {k: [] for k in ("fl_p", "rs_p", "lh_p", "lc_p", "fk_s", "fv_s", "fl_s", "rs_s", "lh_s", "lc_s")}
    y_p = y_s = None
    n_even = w_in_even.shape[0]
    stacked_kv = None

    for i in range(depth):
        j = i // 2
        proj = ple_proj[i].astype(BF16)
        pgw = ple_gate_w[i].astype(BF16)
        pnw = row2(ple_norm_w[i])
        nw = row2(norm_w[i])
        if i % 2 == 0:
            w = w_in_even[j]
            c0 = 3 * HEAD_W
            wf = w[:, :c0].astype(BF16)
            wl = jnp.pad(w[:, c0:c0 + N_HEADS], ((0, 0), (0, PAGE - N_HEADS))).astype(BF16)
            wr = w[:, c0 + N_HEADS:2 * c0 + N_HEADS].astype(BF16)
            wg = w[:, 2 * c0 + N_HEADS:].astype(BF16)
            bf = jnp.pad(b_forget[j], (0, PAGE - N_HEADS)).reshape(1, PAGE)
            wo = w_out_even[j].astype(BF16)
            gn = row2(ret_gn_w[j])
            wt = jnp.swapaxes(w, 0, 1)
            wqkvt = wt[:c0].astype(BF16)
            wlt = jnp.pad(wt[c0:c0 + N_HEADS], ((0, N_HEADS), (0, 0))).astype(BF16)
            bfc = jnp.pad(b_forget[j], (0, N_HEADS)).reshape(2 * N_HEADS, 1)

            fqt, fkt_all, fvt_all, fkb, fvtb, lft, rq, rk, rv, g = _even_in_prompt(
                hp.reshape(bp, lp, d), nw, wqkvt, wlt, wr, wg, bfc, cos_p, sin_p, stacked_kv,
                layer=j, n_layers=n_even, tm=tm_p)
            stacked_kv = (fkt_all, fvt_all)
            cum_rows = _seg_cumsum(lft.reshape(bp * N_HEADS, lp), seg=lp).reshape(bp, N_PAIRS, 2, lp)
            fq, fk, fv, lf, lft_s, rq_s, rk_s, rv_s, g_s = _even_in_sample(
                hs, nw, wf, wl, wr, wg, bf, cos_s, sin_s)
            cn = _seg_cumsum(lft_s, seg=ls).reshape(N_HEADS, bs, ls).transpose(1, 0, 2)
            cn_col = cn.reshape(bs, N_HEADS * ls, 1)
            cn_rows = jnp.pad(jnp.repeat(cn, ls, axis=1), ((0, 0), (0, 0), (0, PAGE - ls)))
            fo, fo_s = _fox(fqt, fkb, fvtb, cum_rows, fq, fk, fv, cn_col, cn_rows,
                            cache_kt, cache_vt, cache_ft, page_table, j, tq=512, t_new=ls)

            ro, s_p = _ret_prompt(rq, rk, rv, lg_tab, gn)
            hp = _even_out(hp, fo.reshape(rows_p, HEAD_W), ro.reshape(rows_p, HEAD_W),
                           g.reshape(rows_p, 2 * HEAD_W), wo, p_prompt.reshape(depth, rows_p, -1),
                           proj, pnw, pgw, layer=i, tm=tm_p)
            outs["fl_p"].append(jnp.swapaxes(lft, 1, 2))
            outs["rs_p"].append(s_p)

            ro, s_s = _ret_sample(rq_s, rk_s, rv_s, state_ret, lg_tab, gn, layer=j, length=ls)
            hs = _even_out(hs, fo_s, ro, g_s, wo, p_sample.reshape(depth, rows_s, -1),
                           proj, pnw, pgw, layer=i, tm=rows_s)
            outs["fk_s"].append(fk.reshape(bs, ls, N_HEADS, HEAD_DIM))
            outs["fv_s"].append(fv.reshape(bs, ls, N_HEADS, HEAD_DIM))
            outs["fl_s"].append(lf.reshape(bs, ls, N_HEADS))
            outs["rs_s"].append(s_s)
        else:
            final = i == depth - 1
            gw = jnp.concatenate([gate_r_w[j], gate_i_w[j]], axis=-1).astype(BF16)
            args = (nw, w_in_odd[j].astype(BF16), conv_w[j], row2(conv_b[j]), gw, row2(gate_r_b[j]),
                    row2(gate_i_b[j]), row2(lru_lambda[j]), w_out_odd[j].astype(BF16))
            tail = (proj, pnw, pgw)
            fnw = row2(final_norm_w)
            o, lh, lc = _odd_layer(hp.reshape(bp, lp, d), *args, p_prompt, *tail,
                                   jnp.zeros((bp, lru_w), F32), jnp.zeros((CONV_W - 1, bp, lru_w), F32),
                                   fnw, layer=i, nb=SEQ_PER_BLOCK, tl=64, final_norm=final)
            hp = o.reshape(rows_p, d)
            outs["lh_p"].append(lh)
            outs["lc_p"].append(jnp.swapaxes(lc, 0, 1))
            o, lh, lc = _odd_layer(hs.reshape(bs, ls, d), *args, p_sample, *tail,
                                   state_lru_h[j], jnp.swapaxes(state_lru_conv[j], 0, 1),
                                   fnw, layer=i, nb=SEQ_PER_BLOCK, tl=ls, final_norm=final)
            hs = o.reshape(rows_s, d)
            outs["lh_s"].append(lh)
            outs["lc_s"].append(jnp.swapaxes(lc, 0, 1))
            if final:
                y_p, y_s = hp.reshape(bp, lp, d), hs.reshape(bs, ls, d)

    st = lambda k: jnp.stack(outs[k])
    to_blhd = lambda t: t.reshape(n_even, bp, N_HEADS, HEAD_DIM, lp).transpose(0, 1, 4, 2, 3)
    return (y_p, y_s, to_blhd(stacked_kv[0]), to_blhd(stacked_kv[1]), st("fl_p"), st("rs_p"), st("lh_p"), st("lc_p"),
            st("fk_s"), st("fv_s"), st("fl_s"), st("rs_s"), st("lh_s"), st("lc_s"))
```

```python
import functools

import jax
import jax.numpy as jnp
import numpy as np
from jax import lax
from jax.experimental import pallas as pl
from jax.experimental.pallas import tpu as pltpu

F32 = jnp.float32
BF16 = jnp.bfloat16

HEAD_DIM = 64
HALF_DIM = HEAD_DIM // 2
N_HEADS = 8
HEAD_W = N_HEADS * HEAD_DIM
PAIR_W = 2 * HEAD_DIM
N_PAIRS = N_HEADS // 2
PAGE = 128
RET_CHUNK = 128
RET_GROUP = 16
ROPE_BASE = 10000.0
LRU_BLOCKS = 8
LRU_BW = 128
CONV_W = 4
SEQ_PER_BLOCK = 8
STEPS_PER_BLOCK = 64
ROW_TILE = 512
FOX_BLOCK = 512
LRU_C = 8.0
NORM_EPS = 1e-6
NEG_INF = -1e30
QK_SCALE = HEAD_DIM ** -0.5
VMEM_LIMIT = 56 * 1024 * 1024


def _params(*sem):
    return pltpu.CompilerParams(dimension_semantics=sem, vmem_limit_bytes=VMEM_LIMIT)


def _mm(a, b):
    return jnp.dot(a, b, preferred_element_type=F32)


def _mm_nt(a, b):
    return lax.dot_general(a, b, (((1,), (1,)), ((), ())), preferred_element_type=F32)


def _rms(x, w):
    return (x * lax.rsqrt(jnp.mean(x * x, axis=-1, keepdims=True) + NORM_EPS)) * w


def _const_spec(shape):
    nd = len(shape)
    return pl.BlockSpec(shape, lambda *_: (0,) * nd)


def _swap_halves(x):
    n = x.shape[-1]
    lane = lax.broadcasted_iota(jnp.int32, x.shape, x.ndim - 1)
    return jnp.where(lane % HEAD_DIM < HALF_DIM,
                     pltpu.roll(x, n - HALF_DIM, x.ndim - 1),
                     pltpu.roll(x, HALF_DIM, x.ndim - 1))


def _retention_gate_proj(ub, wr_ref, wg_ref, cos_ref, sin_ref, rq_ref, rk_ref, rv_ref, g_ref):
    zr = _mm(ub, wr_ref[...])
    cos = cos_ref[...]
    sin = sin_ref[...]
    rq = zr[:, :HEAD_W]
    rk = zr[:, HEAD_W:2 * HEAD_W]
    rq_ref[...] = (rq * cos + _swap_halves(rq) * sin).astype(BF16)
    rk_ref[...] = (rk * cos + _swap_halves(rk) * sin) * QK_SCALE
    rv_ref[...] = zr[:, 2 * HEAD_W:].astype(BF16)
    g_ref[...] = _mm(ub, wg_ref[...])


N_EVEN_IN_PROMPT_INPUTS = 9


def _even_in_prompt_kernel(*refs):
    h_ref, nw_ref, wqkvt_ref, wlt_ref, wr_ref, wg_ref, bfc_ref, cos_ref, sin_ref = refs[:N_EVEN_IN_PROMPT_INPUTS]
    (fqt_ref, fkt_ref, fvt_ref, fkb_ref, fvtb_ref, lft_ref, rq_ref, rk_ref, rv_ref, g_ref) = refs[-10:]
    ub = _rms(h_ref[...], nw_ref[...]).astype(BF16)
    zt = _mm_nt(wqkvt_ref[...], ub)
    fqt_ref[...] = (zt[:HEAD_W] * QK_SCALE).astype(BF16)
    fkt = zt[HEAD_W:2 * HEAD_W]
    fkt_ref[...] = fkt
    fkb_ref[...] = fkt.T.astype(BF16)
    fvt = zt[2 * HEAD_W:]
    fvt_ref[...] = fvt
    fvtb_ref[...] = fvt.astype(BF16)
    lft_ref[...] = jax.nn.log_sigmoid(_mm_nt(wlt_ref[...], ub) + bfc_ref[...])[:N_HEADS]
    _retention_gate_proj(ub, wr_ref, wg_ref, cos_ref, sin_ref, rq_ref, rk_ref, rv_ref, g_ref)


def _even_in_prompt(h, nw, wqkvt, wlt, wr, wg, bfc, cos, sin, stacked_kv, *, layer, n_layers, tm):
    b, l, d = h.shape
    tok = lambda w: pl.BlockSpec((None, tm, w), lambda bi, i: (bi, i, 0))
    feat = lambda r: pl.BlockSpec((None, r, tm), lambda bi, i: (bi, 0, i))
    stacked = pl.BlockSpec((None, None, HEAD_W, tm), lambda bi, i: (layer, bi, 0, i))
    pos = pl.BlockSpec((tm, HEAD_W), lambda bi, i: (i, 0))
    sd = jax.ShapeDtypeStruct
    consts = [nw, wqkvt, wlt, wr, wg, bfc]
    ins = [h, *consts, cos, sin]
    in_specs = [tok(d)] + [_const_spec(c.shape) for c in consts] + [pos, pos]
    assert len(ins) == N_EVEN_IN_PROMPT_INPUTS
    aliases = {}
    if stacked_kv is not None:
        aliases = {len(ins): 1, len(ins) + 1: 2}
        ins += list(stacked_kv)
        in_specs += [pl.BlockSpec(memory_space=pl.ANY)] * 2
    return pl.pallas_call(
        _even_in_prompt_kernel,
        grid=(b, l // tm),
        in_specs=in_specs,
        out_specs=[feat(HEAD_W), stacked, stacked, tok(HEAD_W), feat(HEAD_W), feat(N_HEADS),
                   tok(HEAD_W), tok(HEAD_W), tok(HEAD_W), tok(2 * HEAD_W)],
        out_shape=[sd((b, HEAD_W, l), BF16), sd((n_layers, b, HEAD_W, l), F32), sd((n_layers, b, HEAD_W, l), F32),
                   sd((b, l, HEAD_W), BF16), sd((b, HEAD_W, l), BF16), sd((b, N_HEADS, l), F32),
                   sd((b, l, HEAD_W), BF16), sd((b, l, HEAD_W), F32), sd((b, l, HEAD_W), BF16),
                   sd((b, l, 2 * HEAD_W), F32)],
        input_output_aliases=aliases,
        compiler_params=_params("parallel", "parallel"),
        name="even_in_prompt",
    )(*ins)


def _even_in_sample_kernel(h_ref, nw_ref, wf_ref, wl_ref, wr_ref, wg_ref, bf_ref, cos_ref, sin_ref,
                           fq_ref, fk_ref, fv_ref, lf_ref, lft_ref, rq_ref, rk_ref, rv_ref, g_ref):
    ub = _rms(h_ref[...], nw_ref[...]).astype(BF16)
    zf = _mm(ub, wf_ref[...])
    fq_ref[...] = (zf[:, :HEAD_W] * QK_SCALE).astype(BF16)
    fk_ref[...] = zf[:, HEAD_W:2 * HEAD_W]
    fv_ref[...] = zf[:, 2 * HEAD_W:]
    logf = jax.nn.log_sigmoid(_mm(ub, wl_ref[...]) + bf_ref[...])
    lf_ref[...] = logf[:, :N_HEADS]
    lft_ref[...] = logf.T[:N_HEADS]
    _retention_gate_proj(ub, wr_ref, wg_ref, cos_ref, sin_ref, rq_ref, rk_ref, rv_ref, g_ref)


def _even_in_sample(h, nw, wf, wl, wr, wg, bf, cos, sin):
    rows = h.shape[0]
    sd = jax.ShapeDtypeStruct
    ins = [h, nw, wf, wl, wr, wg, bf, cos, sin]
    outs = [sd((rows, HEAD_W), BF16), sd((rows, HEAD_W), F32), sd((rows, HEAD_W), F32),
            sd((rows, N_HEADS), F32), sd((N_HEADS, rows), F32),
            sd((rows, HEAD_W), BF16), sd((rows, HEAD_W), F32), sd((rows, HEAD_W), BF16),
            sd((rows, 2 * HEAD_W), F32)]
    return pl.pallas_call(
        _even_in_sample_kernel,
        grid=(1,),
        in_specs=[_const_spec(a.shape) for a in ins],
        out_specs=[_const_spec(o.shape) for o in outs],
        out_shape=outs,
        compiler_params=_params("arbitrary"),
        name="even_in_sample",
    )(*ins)


def _seg_cumsum_kernel(x_ref, o_ref, *, seg):
    x = x_ref[...]
    pos = lax.broadcasted_iota(jnp.int32, x.shape, 1) % seg
    s = 1
    while s < seg:
        x = x + jnp.where(pos >= s, pltpu.roll(x, s, 1), 0.0)
        s *= 2
    o_ref[...] = x


def _seg_cumsum(x, *, seg):
    r, n = x.shape
    return pl.pallas_call(
        functools.partial(_seg_cumsum_kernel, seg=seg),
        grid=(r // 8,),
        in_specs=[pl.BlockSpec((8, n), lambda i: (i, 0))],
        out_specs=pl.BlockSpec((8, n), lambda i: (i, 0)),
        out_shape=jax.ShapeDtypeStruct((r, n), F32),
        compiler_params=_params("parallel"),
        name="seg_cumsum",
    )(x)


def _pair_masks(shape):
    lane = lax.broadcasted_iota(jnp.int32, shape, len(shape) - 1)
    return lane < HEAD_DIM


def _split3(x):
    hi = x.astype(BF16).astype(F32)
    rest = x - hi
    mid = rest.astype(BF16).astype(F32)
    return hi, mid, rest - mid


AUG_ROWS = 8
SUM_ROWS = 16


def _fox_prompt_body(qi, qt_ref, k_ref, vt_ref, cr_ref, call_ref, o_ref, kaug_scr, *, tq, tk, with_step):
    seq = k_ref.shape[0]

    @pl.when(qi == 0)
    def _():
        c_all = call_ref[...]
        one = jnp.ones((1, seq), F32)
        zero_row = jnp.zeros((1, seq), F32)
        rows = []
        for h in range(2):
            rows += list(_split3(-c_all[h:h + 1, :])) + [one] * 3 + [zero_row] * (AUG_ROWS - 6)
        aug_t = jnp.concatenate(rows + [jnp.zeros((PAIR_W - 2 * AUG_ROWS, seq), F32)], axis=0)
        kaug_scr[:, :PAIR_W] = k_ref[...]
        kaug_scr[:, PAIR_W:] = aug_t.T.astype(BF16)

    qt = qt_ref[...]
    row = lax.broadcasted_iota(jnp.int32, qt.shape, 0)
    zero = jnp.zeros_like(qt)
    cr = cr_ref[...]
    w_heads = []
    for h in range(2):
        head_rows = (row < HEAD_DIM) if h == 0 else (row >= HEAD_DIM)
        pieces = _split3(cr[h:h + 1, :])
        bias = jnp.zeros(qt.shape, F32)
        for i in range(3):
            bias = jnp.where(row == h * AUG_ROWS + i, 1.0, bias)
            bias = jnp.where(row == h * AUG_ROWS + 3 + i, pieces[i], bias)
        w_heads.append(jnp.concatenate([jnp.where(head_rows, qt, zero), bias.astype(BF16)], axis=0))
    key_l = lax.broadcasted_iota(jnp.int32, (tk, tq), 0)
    qry_l = lax.broadcasted_iota(jnp.int32, (tk, tq), 1)
    ones = jnp.ones((SUM_ROWS, tk), BF16)

    def scores_of(j):
        kc = kaug_scr[j * tk:(j + 1) * tk, :]
        return tuple(_mm(kc, w_heads[h]) for h in range(2))

    def values_of(j, p_heads):
        start = j * tk
        return [_mm(jnp.concatenate([vt_ref[h * HEAD_DIM:(h + 1) * HEAD_DIM, pl.ds(start, tk)], ones], axis=0),
                    p_heads[h]) for h in range(2)]

    def softmax(s, m, diagonal):
        if diagonal:
            s = jnp.where(key_l <= qry_l, s, NEG_INF)
        m_new = jnp.maximum(m, jnp.max(s, axis=0, keepdims=True))
        return m_new, jnp.exp(m - m_new), jnp.exp(s - m_new).astype(BF16)

    def run(n_blocks, other):
        m_heads = [jnp.full((1, tq), NEG_INF, F32)] * 2
        acc_heads = [jnp.zeros((HEAD_DIM + SUM_ROWS, tq), F32)] * 2
        next(other)
        s_cur = scores_of(0)
        for j in range(n_blocks):
            s_next = scores_of(j + 1) if j + 1 < n_blocks else None
            if j == 0:
                next(other)
            p_cur, alphas = [], []
            for h in range(2):
                m_heads[h], alpha, p = softmax(s_cur[h], m_heads[h], j + 1 == n_blocks)
                alphas.append(alpha)
                p_cur.append(p)
            pv = values_of(j, p_cur)
            if j == 0:
                for _ in other:
                    pass
            acc_heads = [alphas[h] * acc_heads[h] + pv[h] for h in range(2)]
            s_cur = s_next
        out = [acc[:HEAD_DIM, :] / acc[HEAD_DIM:HEAD_DIM + 1, :] for acc in acc_heads]
        o_ref[...] = jnp.concatenate(out, axis=0).T

    def branch(q_block):
        run(q_block + 1, with_step(q_block))

    for q_block in range(seq // tq):
        pl.when(qi == q_block)(functools.partial(branch, q_block))


def _ret_tables(lg_ref, length):
    n = RET_CHUNK
    lg_a = lg_ref[0:1, :]
    lg_b = lg_ref[1:2, :]
    lg_m = lg_ref[2:3, :]
    row = lax.broadcasted_iota(jnp.int32, (n, n), 0)
    col = lax.broadcasted_iota(jnp.int32, (n, n), 1)
    diff = row - col
    lower = diff >= 0
    dpos = jnp.where(lower, diff, 0).astype(F32)
    decay = (jnp.where(lower, jnp.exp(dpos * lg_a), 0.0), jnp.where(lower, jnp.exp(dpos * lg_b), 0.0))
    rowf = row.astype(F32)
    q_dec = jnp.exp((rowf + 1.0) * lg_m)
    k_dec = jnp.exp((length - 1.0 - rowf) * lg_m)
    s_dec = jnp.exp(float(length) * jnp.where(row < HEAD_DIM, lg_a, lg_b))
    same_head = (row < HEAD_DIM) == (col < HEAD_DIM)
    return decay, q_dec, k_dec, s_dec, same_head


def _group_norm(o, gn_w):
    first = _pair_masks(o.shape)
    inv = 1.0 / HEAD_DIM
    s0 = jnp.sum(jnp.where(first, o, 0.0), axis=-1, keepdims=True)
    s1 = jnp.sum(jnp.where(first, 0.0, o), axis=-1, keepdims=True)
    d = o - jnp.where(first, s0, s1) * inv
    dd = d * d
    v0 = jnp.sum(jnp.where(first, dd, 0.0), axis=-1, keepdims=True)
    v1 = jnp.sum(jnp.where(first, 0.0, dd), axis=-1, keepdims=True)
    var = jnp.where(first, v0, v1) * inv
    return (d * lax.rsqrt(var + NORM_EPS)) * gn_w


def _ret_prompt_kernel(q_ref, k_ref, v_ref, lg_ref, gn_ref, o_ref, s_ref, *, n_chunks):
    tables = _ret_tables(lg_ref, RET_CHUNK)
    gn_w = gn_ref[...]

    decay, q_dec, k_dec, s_dec, same_head = tables
    group = RET_GROUP

    def body(cc, state):
        rows = [pl.ds(pl.multiple_of((cc * group + g) * RET_CHUNK, RET_CHUNK), RET_CHUNK)
                for g in range(group)]
        qs = [q_ref[r, :] for r in rows]
        ks = [k_ref[r, :] for r in rows]
        vs = [v_ref[r, :] for r in rows]
        first = _pair_masks(qs[0].shape)
        zero = jnp.zeros_like(qs[0])
        kbs = [k.astype(BF16) for k in ks]
        scores = [[_mm_nt(jnp.where(first, q, zero) if h == 0 else jnp.where(first, zero, q), kb)
                   for h in range(2)] for q, kb in zip(qs, kbs)]
        incs = [jnp.where(same_head, _mm((k * k_dec).T.astype(BF16), v), 0.0) for k, v in zip(ks, vs)]
        intra = [[_mm((sc[h] * decay[h]).astype(BF16), v) for h in range(2)] for sc, v in zip(scores, vs)]
        states = []
        for g in range(group):
            states.append(state)
            state = state * s_dec + incs[g]
        cross = [_mm(q, s.astype(BF16)) * q_dec for q, s in zip(qs, states)]
        for g in range(group):
            o = jnp.where(first, intra[g][0], intra[g][1]) + cross[g]
            o_ref[rows[g], :] = _group_norm(o, gn_w)
        return state

    state = lax.fori_loop(0, n_chunks // group, body, jnp.zeros((PAIR_W, PAIR_W), F32))
    s_ref[0] = state[:HEAD_DIM, :HEAD_DIM]
    s_ref[1] = state[HEAD_DIM:, HEAD_DIM:]


def _ret_prompt(rq, rk, rv, lg_tab, gn_w):
    b, l, _ = rq.shape
    seq = pl.BlockSpec((None, l, PAIR_W), lambda bi, hp: (bi, 0, hp))
    return pl.pallas_call(
        functools.partial(_ret_prompt_kernel, n_chunks=l // RET_CHUNK),
        grid=(b, N_PAIRS),
        in_specs=[seq, seq, seq,
                  pl.BlockSpec((None, 8, PAIR_W), lambda bi, hp: (hp, 0, 0)),
                  pl.BlockSpec((1, PAIR_W), lambda bi, hp: (0, hp))],
        out_specs=[seq, pl.BlockSpec((None, 2, HEAD_DIM, HEAD_DIM), lambda bi, hp: (bi, hp, 0, 0))],
        out_shape=[jax.ShapeDtypeStruct((b, l, HEAD_W), F32),
                   jax.ShapeDtypeStruct((b, N_HEADS, HEAD_DIM, HEAD_DIM), F32)],
        compiler_params=_params("parallel", "parallel"),
        name="ret_prompt",
    )(rq, rk, rv, lg_tab, gn_w)


def _ret_sample_kernel(q_ref, k_ref, v_ref, s0_ref, lg_ref, gn_ref, o_ref, s_ref, *, length):
    pad = RET_CHUNK - length
    zrow_f = jnp.zeros((pad, PAIR_W), F32)
    zblk = jnp.zeros((HEAD_DIM, HEAD_DIM), F32)
    pairs = range(N_PAIRS)
    lanes = [slice(hp * PAIR_W, (hp + 1) * PAIR_W) for hp in pairs]
    tables = [_ret_tables(lg_ref.at[hp], length) for hp in pairs]
    qs = [jnp.concatenate([q_ref[:, ln].astype(F32), zrow_f], axis=0).astype(BF16) for ln in lanes]
    ks = [jnp.concatenate([k_ref[:, ln], zrow_f], axis=0) for ln in lanes]
    vs = [jnp.concatenate([v_ref[:, ln].astype(F32), zrow_f], axis=0).astype(BF16) for ln in lanes]
    states = [jnp.concatenate([jnp.concatenate([s0_ref[2 * hp], zblk], axis=1),
                               jnp.concatenate([zblk, s0_ref[2 * hp + 1]], axis=1)], axis=0) for hp in pairs]
    first = _pair_masks(qs[0].shape)
    zero = jnp.zeros_like(qs[0])
    kbs = [k.astype(BF16) for k in ks]
    scores = [[_mm_nt(jnp.where(first, q, zero) if h == 0 else jnp.where(first, zero, q), kb)
               for h in range(2)] for q, kb in zip(qs, kbs)]
    cross = [_mm(q, s.astype(BF16)) * t[1] for q, s, t in zip(qs, states, tables)]
    incs = [jnp.where(t[4], _mm((k * t[2]).T.astype(BF16), v), 0.0) for k, v, t in zip(ks, vs, tables)]
    intra = [[_mm((sc[h] * t[0][h]).astype(BF16), v) for h in range(2)]
             for sc, v, t in zip(scores, vs, tables)]
    for hp in pairs:
        o = jnp.where(first, intra[hp][0], intra[hp][1]) + cross[hp]
        o_ref[:, lanes[hp]] = _group_norm(o[:length], gn_ref[:, lanes[hp]])
        state = states[hp] * tables[hp][3] + incs[hp]
        s_ref[2 * hp] = state[:HEAD_DIM, :HEAD_DIM]
        s_ref[2 * hp + 1] = state[HEAD_DIM:, HEAD_DIM:]


def _ret_sample(rq, rk, rv, state, lg_tab, gn_w, *, layer, length):
    rows = rq.shape[0]
    b = rows // length
    tok = pl.BlockSpec((length, HEAD_W), lambda bi: (bi, 0))
    st_in = pl.BlockSpec((None, None, N_HEADS, HEAD_DIM, HEAD_DIM), lambda bi: (layer, bi, 0, 0, 0))
    st = pl.BlockSpec((None, N_HEADS, HEAD_DIM, HEAD_DIM), lambda bi: (bi, 0, 0, 0))
    return pl.pallas_call(
        functools.partial(_ret_sample_kernel, length=length),
        grid=(b,),
        in_specs=[tok, tok, tok, st_in, _const_spec(lg_tab.shape), _const_spec(gn_w.shape)],
        out_specs=[tok, st],
        out_shape=[jax.ShapeDtypeStruct((rows, HEAD_W), F32),
                   jax.ShapeDtypeStruct(state.shape[1:], F32)],
        compiler_params=_params("parallel"),
        name="ret_sample",
    )(rq, rk, rv, state, lg_tab, gn_w)


def _ple(h1, p, proj, pnw, pgw):
    e = _rms(_mm(p.astype(BF16), proj), pnw)
    return h1 + jax.nn.sigmoid(_mm(h1.astype(BF16), pgw)) * e


def _silu(g):
    return g * jax.nn.sigmoid(g)


def _even_out_kernel(h_ref, fo_ref, ro_ref, g_ref, wo_ref, p_ref, proj_ref, pnw_ref, pgw_ref, o_ref):
    sg = _silu(g_ref[...])
    cat_f = (fo_ref[...] * sg[:, :HEAD_W]).astype(BF16)
    cat_r = (ro_ref[...] * sg[:, HEAD_W:]).astype(BF16)
    mix = _mm(cat_f, wo_ref[:HEAD_W, :]) + _mm(cat_r, wo_ref[HEAD_W:, :])
    o_ref[...] = _ple(h_ref[...] + mix, p_ref[...], proj_ref[...], pnw_ref[...], pgw_ref[...])


def _even_out(h, fo, ro, g, wo, p, proj, pnw, pgw, *, layer, tm):
    rows, d = h.shape
    row = lambda w: pl.BlockSpec((tm, w), lambda i: (i, 0))
    return pl.pallas_call(
        _even_out_kernel,
        grid=(rows // tm,),
        in_specs=[row(d), row(HEAD_W), row(HEAD_W), row(2 * HEAD_W), _const_spec(wo.shape),
                  pl.BlockSpec((None, tm, p.shape[-1]), lambda i: (layer, i, 0)),
                  _const_spec(proj.shape), _const_spec(pnw.shape), _const_spec(pgw.shape)],
        out_specs=row(d),
        out_shape=jax.ShapeDtypeStruct((rows, d), F32),
        compiler_params=_params("parallel"),
        name="even_out",
    )(h, fo, ro, g, wo, p, proj, pnw, pgw)


def _odd_kernel(h_ref, nw_ref, win_ref, cw_ref, cb_ref, gw_ref, grb_ref, gib_ref, lam_ref, wout_ref,
                p_ref, proj_ref, pnw_ref, pgw_ref, h0_ref, cbuf_ref, fnw_ref,
                o_ref, lruh_ref, lruc_ref,
                ext_scr, carry_scr, y_scr, sg_scr, *, tl, final_norm):
    nb, _, d = h_ref.shape
    w = wout_ref.shape[0]
    rows = nb * tl
    halo = (CONV_W - 1) * nb

    @pl.when(pl.program_id(1) == 0)
    def _():
        carry_scr[...] = h0_ref[...]
        ext_scr[0:halo, :] = cbuf_ref[...].reshape(halo, w)

    x = jnp.swapaxes(h_ref[...], 0, 1).reshape(rows, d)
    z = _mm(_rms(x, nw_ref[...]).astype(BF16), win_ref[...])
    xb = z[:, :w]
    sg_scr[...] = _silu(z[:, w:])
    ext_scr[halo:halo + rows, :] = xb
    xc = xb * cw_ref[CONV_W - 1:CONV_W, :] + cb_ref[...]
    for kk in range(1, CONV_W):
        xc = xc + ext_scr[halo - kk * nb:halo - kk * nb + rows, :] * cw_ref[CONV_W - 1 - kk:CONV_W - kk, :]
    last_steps = ext_scr[rows:rows + halo, :]
    lruc_ref[...] = last_steps.reshape(CONV_W - 1, nb, w)
    ext_scr[0:halo, :] = last_steps

    sp = jax.nn.softplus(-lam_ref[...])
    for n in range(LRU_BLOCKS):
        lanes = slice(n * LRU_BW, (n + 1) * LRU_BW)
        xn = xc[:, lanes]
        zz = _mm(xn.astype(BF16), gw_ref[n])
        r = jax.nn.sigmoid(zz[:, :LRU_BW] + grb_ref[:, lanes])
        gi = jax.nn.sigmoid(zz[:, LRU_BW:] + gib_ref[:, lanes])
        log_a = (-LRU_C * r) * sp[:, lanes]
        a = jnp.exp(log_a)
        one_minus = -jnp.tanh(log_a) * (a * a + 1.0)
        b = (jnp.sqrt(one_minus) * gi) * xn
        hs = carry_scr[:, lanes]
        for t in range(tl):
            step = slice(t * nb, (t + 1) * nb)
            hs = a[step] * hs + b[step]
            y_scr[step, lanes] = hs
        carry_scr[:, lanes] = hs
        lruh_ref[:, lanes] = hs

    mix = _mm((y_scr[...] * sg_scr[...]).astype(BF16), wout_ref[...])
    p = jnp.swapaxes(p_ref[...], 0, 1).reshape(rows, -1)
    h2 = _ple(x + mix, p, proj_ref[...], pnw_ref[...], pgw_ref[...])
    if final_norm:
        h2 = _rms(h2, fnw_ref[...])
    o_ref[...] = jnp.swapaxes(h2.reshape(tl, nb, d), 0, 1)


def _odd_layer(h, nw, win, cw, cb, gw, grb, gib, lam, wout, p, proj, pnw, pgw, h0, cbuf, fnw,
               *, layer, nb, tl, final_norm):
    b, l, d = h.shape
    w = wout.shape[0]
    seq = lambda width: pl.BlockSpec((nb, tl, width), lambda bi, li: (bi, li, 0))
    state = pl.BlockSpec((nb, w), lambda bi, li: (bi, 0))
    conv = pl.BlockSpec((CONV_W - 1, nb, w), lambda bi, li: (0, bi, 0))
    consts = [nw, win, cw, cb, gw, grb, gib, lam, wout]
    tail = [proj, pnw, pgw]
    rows = nb * tl
    return pl.pallas_call(
        functools.partial(_odd_kernel, tl=tl, final_norm=final_norm),
        grid=(b // nb, l // tl),
        in_specs=[seq(d)] + [_const_spec(c.shape) for c in consts]
                 + [pl.BlockSpec((None, nb, tl, p.shape[-1]), lambda bi, li: (layer, bi, li, 0))]
                 + [_const_spec(c.shape) for c in tail] + [state, conv, _const_spec(fnw.shape)],
        out_specs=[seq(d), state, conv],
        out_shape=[jax.ShapeDtypeStruct((b, l, d), F32),
                   jax.ShapeDtypeStruct((b, w), F32),
                   jax.ShapeDtypeStruct((CONV_W - 1, b, w), F32)],
        scratch_shapes=[pltpu.VMEM(((CONV_W - 1) * nb + rows, w), F32), pltpu.VMEM((nb, w), F32),
                        pltpu.VMEM((rows, w), F32), pltpu.VMEM((rows, w), F32)],
        compiler_params=_params("parallel", "arbitrary"),
        name="odd_layer",
    )(h, *consts, p, *tail, h0, cbuf, fnw)


def _fox_sample_body(c, n_steps, q_ref, kn_ref, vn_ref, cnc_ref, cnr_ref, tri_ref, k_pages, v_pages, f_pages,
                     o_ref, qbd_scr, m_scr, l_scr, acc_scr, rel_scr, *, t_new):
    pages_per_step = len(k_pages)
    cn_col = cnc_ref[...]

    def start():
        rel_scr[...] = jnp.zeros_like(rel_scr)
        q = q_ref[...].astype(F32)
        lane_head = lax.broadcasted_iota(jnp.int32, q.shape, 1) // HEAD_DIM
        zero = jnp.zeros_like(q)
        for h in range(N_HEADS):
            qbd_scr[h * t_new:(h + 1) * t_new, :] = jnp.where(lane_head == h, q, zero)
        m_scr[...] = jnp.full(m_scr.shape, NEG_INF, F32)
        l_scr[...] = jnp.zeros_like(l_scr)
        acc_scr[...] = jnp.zeros_like(acc_scr)

    if c == 0:
        start()
    qbd = qbd_scr[...].astype(BF16)

    def online(s, pv_fn):
        m = m_scr[...]
        m_new = jnp.maximum(m, jnp.max(s, axis=-1, keepdims=True))
        alpha = jnp.exp(m - m_new)
        p = jnp.exp(s - m_new)
        l_scr[...] = alpha * l_scr[...] + jnp.sum(p, axis=-1, keepdims=True)
        acc_scr[...] = alpha * acc_scr[...] + pv_fn(p.astype(BF16))
        m_scr[...] = m_new

    xs = jnp.concatenate([f[...] for f in f_pages], axis=0)
    hi = xs.astype(BF16)
    rest = xs - hi.astype(F32)
    mid = rest.astype(BF16)
    lo = (rest - mid.astype(F32)).astype(BF16)
    tri = tri_ref[...]
    sums = (_mm(hi, tri) + _mm(mid, tri)) + _mm(lo, tri)
    after = rel_scr[...]
    bias = [None] * pages_per_step
    for i in reversed(range(pages_per_step)):
        rows = slice(i * N_HEADS, (i + 1) * N_HEADS)
        rel = sums[rows, :PAGE] + after
        after = after + sums[rows, PAGE:]
        bias[i] = jnp.concatenate(
            [jnp.broadcast_to(rel[h:h + 1, :], (t_new, PAGE)) for h in range(N_HEADS)], axis=0)
    rel_scr[...] = after

    kt = jnp.concatenate([k[...].astype(BF16) for k in k_pages], axis=1)
    vt = jnp.concatenate([v[...].astype(BF16) for v in v_pages], axis=1)
    s = (_mm(qbd, kt) + cn_col) + jnp.concatenate(bias, axis=1)
    yield
    m = m_scr[...]
    m_new = jnp.maximum(m, jnp.max(s, axis=-1, keepdims=True))
    alpha = jnp.exp(m - m_new)
    p = jnp.exp(s - m_new)
    yield
    l_scr[...] = alpha * l_scr[...] + jnp.sum(p, axis=-1, keepdims=True)
    acc_scr[...] = alpha * acc_scr[...] + _mm_nt(p.astype(BF16), vt)
    m_scr[...] = m_new

    def finish():
        zpad = jnp.zeros((PAGE - t_new, HEAD_W), F32)
        kn = jnp.concatenate([kn_ref[...], zpad], axis=0).astype(BF16)
        vn = jnp.concatenate([vn_ref[...], zpad], axis=0).astype(BF16)
        s = (_mm_nt(qbd, kn) + cn_col) - cnr_ref[...]
        t_query = lax.broadcasted_iota(jnp.int32, s.shape, 0) % t_new
        t_key = lax.broadcasted_iota(jnp.int32, s.shape, 1)
        s = jnp.where(t_query >= t_key, s, NEG_INF)
        online(s, lambda p: _mm(p, vn))
        o = acc_scr[...] / l_scr[...]
        lane_head = lax.broadcasted_iota(jnp.int32, (t_new, HEAD_W), 1) // HEAD_DIM
        out = jnp.zeros((t_new, HEAD_W), F32)
        for h in range(N_HEADS):
            out = jnp.where(lane_head == h, o[h * t_new:(h + 1) * t_new, :], out)
        o_ref[...] = out

    if c == n_steps - 1:
        finish()


N_FOX_PROMPT_INPUTS = 5
N_FOX_SAMPLE_INPUTS = 6


N_PAGE_ARRAYS = 3
N_PAGE_SLOTS = 2


def _fox_kernel(tbl_ref, *refs, tq, tk, t_new, pages_per_step, layer):
    prompt_in = refs[:N_FOX_PROMPT_INPUTS]
    sample_in = refs[N_FOX_PROMPT_INPUTS:N_FOX_PROMPT_INPUTS + N_FOX_SAMPLE_INPUTS]
    rest = refs[N_FOX_PROMPT_INPUTS + N_FOX_SAMPLE_INPUTS:]
    caches = rest[:N_PAGE_ARRAYS]
    o_prompt_ref, o_sample_ref, kaug_scr = rest[N_PAGE_ARRAYS:N_PAGE_ARRAYS + 3]
    sample_scr = rest[N_PAGE_ARRAYS + 3:N_PAGE_ARRAYS + 8]
    page_bufs = rest[N_PAGE_ARRAYS + 8:2 * N_PAGE_ARRAYS + 8]
    sems = rest[2 * N_PAGE_ARRAYS + 8]
    n_steps = prompt_in[1].shape[0] // tq
    seq = pl.program_id(0) * N_PAIRS + pl.program_id(1)
    n_seqs = pl.num_programs(0) * N_PAIRS

    def page_copies(of_seq, step):
        slot = step % N_PAGE_SLOTS
        copies = []
        for i in range(pages_per_step):
            page = tbl_ref[of_seq, (n_steps - 1 - step) * pages_per_step + i]
            for a in range(N_PAGE_ARRAYS):
                copies.append(pltpu.make_async_copy(
                    caches[a].at[layer, page], page_bufs[a].at[slot, i], sems.at[slot, a]))
        return copies

    def start(copies):
        for cp in copies:
            cp.start()

    def sample_step(c):
        if c == 0:
            pl.when(seq == 0)(lambda: start(page_copies(seq, 0)))
        if c + 1 < n_steps:
            start(page_copies(seq, c + 1))
        else:
            pl.when(seq + 1 < n_seqs)(lambda: start(page_copies(seq + 1, 0)))
        for cp in page_copies(seq, c):
            cp.wait()
        slot = c % N_PAGE_SLOTS
        views = [[buf.at[slot, i] for i in range(pages_per_step)] for buf in page_bufs]
        return _fox_sample_body(c, n_steps, *sample_in, *views, o_sample_ref, *sample_scr, t_new=t_new)

    _fox_prompt_body(pl.program_id(2), *prompt_in, o_prompt_ref, kaug_scr, tq=tq, tk=tk, with_step=sample_step)


def _fox(qt, kb, vtb, cum_rows, q, k_new, v_new, cn_col, cn_rows, cache_kt, cache_vt, cache_ft, table, layer,
         *, tq, t_new):
    b, l, _ = kb.shape
    bs, n_pages = table.shape
    n_steps = l // tq
    assert bs == b * N_PAIRS and n_pages % n_steps == 0, "one sample sequence per prompt (batch, head pair)"
    pages_per_step = n_pages // n_steps
    n_rows = N_HEADS * t_new
    seq_of = lambda bi, hp: bi * N_PAIRS + hp
    tok = pl.BlockSpec((t_new, HEAD_W), lambda bi, hp, qi, tbl: (seq_of(bi, hp), 0))

    assert n_steps % N_PAGE_SLOTS == 0, "a sequence must start on buffer slot 0"
    caches = [cache_kt, cache_vt, cache_ft]
    page_bufs = [pltpu.VMEM((N_PAGE_SLOTS, pages_per_step) + c.shape[2:], F32) for c in caches]
    pos = jnp.arange(PAGE)
    later = (pos[:, None] > pos[None, :]).astype(BF16)
    tri = jnp.concatenate([later, jnp.ones((PAGE, PAGE), BF16)], axis=1)
    prompt_specs = [pl.BlockSpec((None, PAIR_W, tq), lambda bi, hp, qi, tbl: (bi, hp, qi)),
                    pl.BlockSpec((None, l, PAIR_W), lambda bi, hp, qi, tbl: (bi, 0, hp)),
                    pl.BlockSpec((None, PAIR_W, l), lambda bi, hp, qi, tbl: (bi, hp, 0)),
                    pl.BlockSpec((None, None, 2, tq), lambda bi, hp, qi, tbl: (bi, hp, 0, qi)),
                    pl.BlockSpec((None, None, 2, l), lambda bi, hp, qi, tbl: (bi, hp, 0, 0))]
    sample_specs = [tok, tok, tok,
                    pl.BlockSpec((None, n_rows, 1), lambda bi, hp, qi, tbl: (seq_of(bi, hp), 0, 0)),
                    pl.BlockSpec((None, n_rows, PAGE), lambda bi, hp, qi, tbl: (seq_of(bi, hp), 0, 0)),
                    pl.BlockSpec(tri.shape, lambda bi, hp, qi, tbl: (0, 0))]
    assert len(prompt_specs) == N_FOX_PROMPT_INPUTS and len(sample_specs) == N_FOX_SAMPLE_INPUTS
    return pl.pallas_call(
        functools.partial(_fox_kernel, tq=tq, tk=tq, t_new=t_new, pages_per_step=pages_per_step, layer=layer),
        grid_spec=pltpu.PrefetchScalarGridSpec(
            num_scalar_prefetch=1,
            grid=(b, N_PAIRS, n_steps),
            in_specs=prompt_specs + sample_specs + [pl.BlockSpec(memory_space=pl.ANY)] * N_PAGE_ARRAYS,
            out_specs=[pl.BlockSpec((None, tq, PAIR_W), lambda bi, hp, qi, tbl: (bi, qi, hp)), tok],
            scratch_shapes=[pltpu.VMEM((l, 2 * PAIR_W), BF16),
                            pltpu.VMEM((n_rows, HEAD_W), F32), pltpu.VMEM((n_rows, 1), F32),
                            pltpu.VMEM((n_rows, 1), F32), pltpu.VMEM((n_rows, HEAD_W), F32),
                            pltpu.VMEM((N_HEADS, PAGE), F32)]
                           + page_bufs + [pltpu.SemaphoreType.DMA((N_PAGE_SLOTS, N_PAGE_ARRAYS))]),
        out_shape=[jax.ShapeDtypeStruct((b, l, HEAD_W), F32),
                   jax.ShapeDtypeStruct((bs * t_new, HEAD_W), F32)],
        compiler_params=_params("arbitrary", "arbitrary", "arbitrary"),
        name="fox",
    )(table, qt, kb, vtb, cum_rows, cum_rows, q, k_new, v_new, cn_col, cn_rows, tri, *caches)


def _rope_tables(pos):
    inv = ROPE_BASE ** (-np.arange(HALF_DIM, dtype=np.float64) / HALF_DIM)
    ang = np.asarray(pos, np.float64)[:, None] * inv[None, :]
    cos_head = np.concatenate([np.cos(ang), np.cos(ang)], axis=-1)
    sin_head = np.concatenate([-np.sin(ang), np.sin(ang)], axis=-1)
    return (jnp.asarray(np.tile(cos_head, (1, N_HEADS)), F32), jnp.asarray(np.tile(sin_head, (1, N_HEADS)), F32))


def _log_gamma_table(log_gamma):
    pairs = log_gamma.reshape(N_PAIRS, 2)
    ones = jnp.ones((N_PAIRS, 1, PAIR_W), F32)
    first = pairs[:, 0][:, None, None] * ones
    second = pairs[:, 1][:, None, None] * ones
    mix = jnp.repeat(pairs, HEAD_DIM, axis=1)[:, None, :]
    return jnp.concatenate([first, second, mix, jnp.zeros((N_PAIRS, 5, PAIR_W), F32)], axis=1)


def kernel(x_prompt, x_sample, cache_fox_k, cache_fox_v, cache_fox_logf, page_table, state_ret, state_lru_h, state_lru_conv, p_prompt, p_sample, norm_w, w_in_even, b_forget, ret_gn_w, w_out_even, w_in_odd, conv_w, conv_b, gate_r_w, gate_r_b, gate_i_w, gate_i_b, lru_lambda, w_out_odd, ple_proj, ple_norm_w, ple_gate_w, final_norm_w):
    bp, lp, d = x_prompt.shape
    bs, ls, _ = x_sample.shape
    depth = norm_w.shape[0]
    n_pool = cache_fox_k.shape[1]
    n_pages = page_table.shape[1]
    past_len = n_pages * PAGE
    lru_w = w_out_odd.shape[1]
    rows_p, rows_s = bp * lp, bs * ls
    tm_p = ROW_TILE

    log_gamma = jnp.log1p(-jnp.exp2(-5.0 - jnp.arange(N_HEADS, dtype=F32)))
    lg_tab = _log_gamma_table(log_gamma)
    cos_p, sin_p = _rope_tables(np.arange(lp))
    cos_s, sin_s = _rope_tables(np.tile(past_len + np.arange(ls), bs))

    cache_kt = cache_fox_k.transpose(0, 1, 3, 4, 2).reshape(-1, n_pool, HEAD_W, PAGE)
    cache_vt = cache_fox_v.transpose(0, 1, 3, 4, 2).reshape(-1, n_pool, HEAD_W, PAGE)
    cache_ft = jnp.swapaxes(cache_fox_logf, 2, 3)

    row2 = lambda v: v.reshape(1, -1)
    hp = x_prompt.reshape(rows_p, d)
    hs = x_sample.reshape(rows_s, d)
    outs = {k: [] for k in ("fl_p", "rs_p", "lh_p", "lc_p", "fk_s", "fv_s", "fl_s", "rs_s", "lh_s", "lc_s")}
    y_p = y_s = None
    n_even = w_in_even.shape[0]
    stacked_kv = None

    for i in range(depth):
        j = i // 2
        proj = ple_proj[i].astype(BF16)
        pgw = ple_gate_w[i].astype(BF16)
        pnw = row2(ple_norm_w[i])
        nw = row2(norm_w[i])
        if i % 2 == 0:
            w = w_in_even[j]
            c0 = 3 * HEAD_W
            wf = w[:, :c0].astype(BF16)
            wl = jnp.pad(w[:, c0:c0 + N_HEADS], ((0, 0), (0, PAGE - N_HEADS))).astype(BF16)
            wr = w[:, c0 + N_HEADS:2 * c0 + N_HEADS].astype(BF16)
            wg = w[:, 2 * c0 + N_HEADS:].astype(BF16)
            bf = jnp.pad(b_forget[j], (0, PAGE - N_HEADS)).reshape(1, PAGE)
            wo = w_out_even[j].astype(BF16)
            gn = row2(ret_gn_w[j])
            wt = jnp.swapaxes(w, 0, 1)
            wqkvt = wt[:c0].astype(BF16)
            wlt = jnp.pad(wt[c0:c0 + N_HEADS], ((0, N_HEADS), (0, 0))).astype(BF16)
            bfc = jnp.pad(b_forget[j], (0, N_HEADS)).reshape(2 * N_HEADS, 1)

            fqt, fkt_all, fvt_all, fkb, fvtb, lft, rq, rk, rv, g = _even_in_prompt(
                hp.reshape(bp, lp, d), nw, wqkvt, wlt, wr, wg, bfc, cos_p, sin_p, stacked_kv,
                layer=j, n_layers=n_even, tm=tm_p)
            stacked_kv = (fkt_all, fvt_all)
            cum_rows = _seg_cumsum(lft.reshape(bp * N_HEADS, lp), seg=lp).reshape(bp, N_PAIRS, 2, lp)
            fq, fk, fv, lf, lft_s, rq_s, rk_s, rv_s, g_s = _even_in_sample(
                hs, nw, wf, wl, wr, wg, bf, cos_s, sin_s)
            cn = _seg_cumsum(lft_s, seg=ls).reshape(N_HEADS, bs, ls).transpose(1, 0, 2)
            cn_col = cn.reshape(bs, N_HEADS * ls, 1)
            cn_rows = jnp.pad(jnp.repeat(cn, ls, axis=1), ((0, 0), (0, 0), (0, PAGE - ls)))
            fo, fo_s = _fox(fqt, fkb, fvtb, cum_rows, fq, fk, fv, cn_col, cn_rows,
                            cache_kt, cache_vt, cache_ft, page_table, j, tq=FOX_BLOCK, t_new=ls)

            ro, s_p = _ret_prompt(rq, rk, rv, lg_tab, gn)
            hp = _even_out(hp, fo.reshape(rows_p, HEAD_W), ro.reshape(rows_p, HEAD_W),
                           g.reshape(rows_p, 2 * HEAD_W), wo, p_prompt.reshape(depth, rows_p, -1),
                           proj, pnw, pgw, layer=i, tm=tm_p)
            outs["fl_p"].append(jnp.swapaxes(lft, 1, 2))
            outs["rs_p"].append(s_p)

            ro, s_s = _ret_sample(rq_s, rk_s, rv_s, state_ret, lg_tab, gn, layer=j, length=ls)
            hs = _even_out(hs, fo_s, ro, g_s, wo, p_sample.reshape(depth, rows_s, -1),
                           proj, pnw, pgw, layer=i, tm=rows_s)
            outs["fk_s"].append(fk.reshape(bs, ls, N_HEADS, HEAD_DIM))
            outs["fv_s"].append(fv.reshape(bs, ls, N_HEADS, HEAD_DIM))
            outs["fl_s"].append(lf.reshape(bs, ls, N_HEADS))
            outs["rs_s"].append(s_s)
        else:
            final = i == depth - 1
            gw = jnp.concatenate([gate_r_w[j], gate_i_w[j]], axis=-1).astype(BF16)
            args = (nw, w_in_odd[j].astype(BF16), conv_w[j], row2(conv_b[j]), gw, row2(gate_r_b[j]),
                    row2(gate_i_b[j]), row2(lru_lambda[j]), w_out_odd[j].astype(BF16))
            tail = (proj, pnw, pgw)
            fnw = row2(final_norm_w)
            o, lh, lc = _odd_layer(hp.reshape(bp, lp, d), *args, p_prompt, *tail,
                                   jnp.zeros((bp, lru_w), F32), jnp.zeros((CONV_W - 1, bp, lru_w), F32),
                                   fnw, layer=i, nb=SEQ_PER_BLOCK, tl=STEPS_PER_BLOCK, final_norm=final)
            hp = o.reshape(rows_p, d)
            outs["lh_p"].append(lh)
            outs["lc_p"].append(jnp.swapaxes(lc, 0, 1))
            o, lh, lc = _odd_layer(hs.reshape(bs, ls, d), *args, p_sample, *tail,
                                   state_lru_h[j], jnp.swapaxes(state_lru_conv[j], 0, 1),
                                   fnw, layer=i, nb=SEQ_PER_BLOCK, tl=ls, final_norm=final)
            hs = o.reshape(rows_s, d)
            outs["lh_s"].append(lh)
            outs["lc_s"].append(jnp.swapaxes(lc, 0, 1))
            if final:
                y_p, y_s = hp.reshape(bp, lp, d), hs.reshape(bs, ls, d)

    st = lambda k: jnp.stack(outs[k])
    to_blhd = lambda t: t.reshape(n_even, bp, N_HEADS, HEAD_DIM, lp).transpose(0, 1, 4, 2, 3)
    return (y_p, y_s, to_blhd(stacked_kv[0]), to_blhd(stacked_kv[1]), st("fl_p"), st("rs_p"), st("lh_p"), st("lc_p"),
            st("fk_s"), st("fv_s"), st("fl_s"), st("rs_s"), st("lh_s"), st("lc_s"))
```

```python
import functools

import jax
import jax.numpy as jnp
import numpy as np
from jax import lax
from jax.experimental import pallas as pl
from jax.experimental.pallas import tpu as pltpu

F32 = jnp.float32
BF16 = jnp.bfloat16

HEAD_DIM = 64
HALF_DIM = HEAD_DIM // 2
N_HEADS = 8
HEAD_W = N_HEADS * HEAD_DIM
PAIR_W = 2 * HEAD_DIM
N_PAIRS = N_HEADS // 2
PAGE = 128
RET_CHUNK = 128
RET_GROUP = 16
ROPE_BASE = 10000.0
LRU_BLOCKS = 8
LRU_BW = 128
CONV_W = 4
SEQ_PER_BLOCK = 8
STEPS_PER_BLOCK = 64
ROW_TILE = 512
FOX_BLOCK = 512
LRU_C = 8.0
NORM_EPS = 1e-6
NEG_INF = -1e30
QK_SCALE = HEAD_DIM ** -0.5
VMEM_LIMIT = 56 * 1024 * 1024


def _params(*sem):
    return pltpu.CompilerParams(dimension_semantics=sem, vmem_limit_bytes=VMEM_LIMIT)


def _mm(a, b):
    return jnp.dot(a, b, preferred_element_type=F32)


def _mm_nt(a, b):
    return lax.dot_general(a, b, (((1,), (1,)), ((), ())), preferred_element_type=F32)


def _rms(x, w):
    return (x * lax.rsqrt(jnp.mean(x * x, axis=-1, keepdims=True) + NORM_EPS)) * w


def _const_spec(shape):
    nd = len(shape)
    return pl.BlockSpec(shape, lambda *_: (0,) * nd)


def _swap_halves(x):
    n = x.shape[-1]
    lane = lax.broadcasted_iota(jnp.int32, x.shape, x.ndim - 1)
    return jnp.where(lane % HEAD_DIM < HALF_DIM,
                     pltpu.roll(x, n - HALF_DIM, x.ndim - 1),
                     pltpu.roll(x, HALF_DIM, x.ndim - 1))


def _retention_gate_proj(ub, wr_ref, wg_ref, cos_ref, sin_ref, rq_ref, rk_ref, rv_ref, g_ref):
    zr = _mm(ub, wr_ref[...])
    cos = cos_ref[...]
    sin = sin_ref[...]
    rq = zr[:, :HEAD_W]
    rk = zr[:, HEAD_W:2 * HEAD_W]
    rq_ref[...] = (rq * cos + _swap_halves(rq) * sin).astype(BF16)
    rk_ref[...] = (rk * cos + _swap_halves(rk) * sin) * QK_SCALE
    rv_ref[...] = zr[:, 2 * HEAD_W:].astype(BF16)
    g_ref[...] = _mm(ub, wg_ref[...])


N_EVEN_IN_PROMPT_INPUTS = 9


def _even_in_prompt_kernel(*refs):
    h_ref, nw_ref, wqkvt_ref, wlt_ref, wr_ref, wg_ref, bfc_ref, cos_ref, sin_ref = refs[:N_EVEN_IN_PROMPT_INPUTS]
    (fqt_ref, fkt_ref, fvt_ref, fkb_ref, fvtb_ref, lft_ref, rq_ref, rk_ref, rv_ref, g_ref) = refs[-10:]
    ub = _rms(h_ref[...], nw_ref[...]).astype(BF16)
    zt = _mm_nt(wqkvt_ref[...], ub)
    fqt_ref[...] = (zt[:HEAD_W] * QK_SCALE).astype(BF16)
    fkt = zt[HEAD_W:2 * HEAD_W]
    fkt_ref[...] = fkt
    fkb_ref[...] = fkt.T.astype(BF16)
    fvt = zt[2 * HEAD_W:]
    fvt_ref[...] = fvt
    fvtb_ref[...] = fvt.astype(BF16)
    lft_ref[...] = jax.nn.log_sigmoid(_mm_nt(wlt_ref[...], ub) + bfc_ref[...])[:N_HEADS]
    _retention_gate_proj(ub, wr_ref, wg_ref, cos_ref, sin_ref, rq_ref, rk_ref, rv_ref, g_ref)


def _even_in_prompt(h, nw, wqkvt, wlt, wr, wg, bfc, cos, sin, stacked_kv, *, layer, n_layers, tm):
    b, l, d = h.shape
    tok = lambda w: pl.BlockSpec((None, tm, w), lambda bi, i: (bi, i, 0))
    feat = lambda r: pl.BlockSpec((None, r, tm), lambda bi, i: (bi, 0, i))
    stacked = pl.BlockSpec((None, None, HEAD_W, tm), lambda bi, i: (layer, bi, 0, i))
    pos = pl.BlockSpec((tm, HEAD_W), lambda bi, i: (i, 0))
    sd = jax.ShapeDtypeStruct
    consts = [nw, wqkvt, wlt, wr, wg, bfc]
    ins = [h, *consts, cos, sin]
    in_specs = [tok(d)] + [_const_spec(c.shape) for c in consts] + [pos, pos]
    assert len(ins) == N_EVEN_IN_PROMPT_INPUTS
    aliases = {}
    if stacked_kv is not None:
        aliases = {len(ins): 1, len(ins) + 1: 2}
        ins += list(stacked_kv)
        in_specs += [pl.BlockSpec(memory_space=pl.ANY)] * 2
    return pl.pallas_call(
        _even_in_prompt_kernel,
        grid=(b, l // tm),
        in_specs=in_specs,
        out_specs=[feat(HEAD_W), stacked, stacked, tok(HEAD_W), feat(HEAD_W), feat(N_HEADS),
                   tok(HEAD_W), tok(HEAD_W), tok(HEAD_W), tok(2 * HEAD_W)],
        out_shape=[sd((b, HEAD_W, l), BF16), sd((n_layers, b, HEAD_W, l), F32), sd((n_layers, b, HEAD_W, l), F32),
                   sd((b, l, HEAD_W), BF16), sd((b, HEAD_W, l), BF16), sd((b, N_HEADS, l), F32),
                   sd((b, l, HEAD_W), BF16), sd((b, l, HEAD_W), F32), sd((b, l, HEAD_W), BF16),
                   sd((b, l, 2 * HEAD_W), F32)],
        input_output_aliases=aliases,
        compiler_params=_params("parallel", "parallel"),
        name="even_in_prompt",
    )(*ins)


def _even_in_sample_kernel(h_ref, nw_ref, wf_ref, wl_ref, wr_ref, wg_ref, bf_ref, cos_ref, sin_ref,
                           fq_ref, fk_ref, fv_ref, lf_ref, lft_ref, rq_ref, rk_ref, rv_ref, g_ref):
    ub = _rms(h_ref[...], nw_ref[...]).astype(BF16)
    zf = _mm(ub, wf_ref[...])
    fq_ref[...] = (zf[:, :HEAD_W] * QK_SCALE).astype(BF16)
    fk_ref[...] = zf[:, HEAD_W:2 * HEAD_W]
    fv_ref[...] = zf[:, 2 * HEAD_W:]
    logf = jax.nn.log_sigmoid(_mm(ub, wl_ref[...]) + bf_ref[...])
    lf_ref[...] = logf[:, :N_HEADS]
    lft_ref[...] = logf.T[:N_HEADS]
    _retention_gate_proj(ub, wr_ref, wg_ref, cos_ref, sin_ref, rq_ref, rk_ref, rv_ref, g_ref)


def _even_in_sample(h, nw, wf, wl, wr, wg, bf, cos, sin):
    rows = h.shape[0]
    sd = jax.ShapeDtypeStruct
    ins = [h, nw, wf, wl, wr, wg, bf, cos, sin]
    outs = [sd((rows, HEAD_W), BF16), sd((rows, HEAD_W), F32), sd((rows, HEAD_W), F32),
            sd((rows, N_HEADS), F32), sd((N_HEADS, rows), F32),
            sd((rows, HEAD_W), BF16), sd((rows, HEAD_W), F32), sd((rows, HEAD_W), BF16),
            sd((rows, 2 * HEAD_W), F32)]
    return pl.pallas_call(
        _even_in_sample_kernel,
        grid=(1,),
        in_specs=[_const_spec(a.shape) for a in ins],
        out_specs=[_const_spec(o.shape) for o in outs],
        out_shape=outs,
        compiler_params=_params("arbitrary"),
        name="even_in_sample",
    )(*ins)


def _seg_cumsum_kernel(x_ref, o_ref, *, seg):
    x = x_ref[...]
    pos = lax.broadcasted_iota(jnp.int32, x.shape, 1) % seg
    s = 1
    while s < seg:
        x = x + jnp.where(pos >= s, pltpu.roll(x, s, 1), 0.0)
        s *= 2
    o_ref[...] = x


def _seg_cumsum(x, *, seg):
    r, n = x.shape
    return pl.pallas_call(
        functools.partial(_seg_cumsum_kernel, seg=seg),
        grid=(r // 8,),
        in_specs=[pl.BlockSpec((8, n), lambda i: (i, 0))],
        out_specs=pl.BlockSpec((8, n), lambda i: (i, 0)),
        out_shape=jax.ShapeDtypeStruct((r, n), F32),
        compiler_params=_params("parallel"),
        name="seg_cumsum",
    )(x)


def _pair_masks(shape):
    lane = lax.broadcasted_iota(jnp.int32, shape, len(shape) - 1)
    return lane < HEAD_DIM


def _split3(x):
    hi = x.astype(BF16).astype(F32)
    rest = x - hi
    mid = rest.astype(BF16).astype(F32)
    return hi, mid, rest - mid


AUG_ROWS = 8
SUM_ROWS = 16


def _fox_prompt_body(qi, qt_ref, k_ref, vt_ref, cr_ref, call_ref, o_ref, kaug_scr, *, tq, tk, with_step):
    seq = k_ref.shape[0]

    @pl.when(qi == 0)
    def _():
        c_all = call_ref[...]
        one = jnp.ones((1, seq), F32)
        zero_row = jnp.zeros((1, seq), F32)
        rows = []
        for h in range(2):
            rows += list(_split3(-c_all[h:h + 1, :])) + [one] * 3 + [zero_row] * (AUG_ROWS - 6)
        aug_t = jnp.concatenate(rows + [jnp.zeros((PAIR_W - 2 * AUG_ROWS, seq), F32)], axis=0)
        kaug_scr[:, :PAIR_W] = k_ref[...]
        kaug_scr[:, PAIR_W:] = aug_t.T.astype(BF16)

    qt = qt_ref[...]
    row = lax.broadcasted_iota(jnp.int32, qt.shape, 0)
    zero = jnp.zeros_like(qt)
    cr = cr_ref[...]
    w_heads = []
    for h in range(2):
        head_rows = (row < HEAD_DIM) if h == 0 else (row >= HEAD_DIM)
        pieces = _split3(cr[h:h + 1, :])
        bias = jnp.zeros(qt.shape, F32)
        for i in range(3):
            bias = jnp.where(row == h * AUG_ROWS + i, 1.0, bias)
            bias = jnp.where(row == h * AUG_ROWS + 3 + i, pieces[i], bias)
        w_heads.append(jnp.concatenate([jnp.where(head_rows, qt, zero), bias.astype(BF16)], axis=0))
    key_l = lax.broadcasted_iota(jnp.int32, (tk, tq), 0)
    qry_l = lax.broadcasted_iota(jnp.int32, (tk, tq), 1)
    ones = jnp.ones((SUM_ROWS, tk), BF16)

    def scores_of(j):
        kc = kaug_scr[j * tk:(j + 1) * tk, :]
        return tuple(_mm(kc, w_heads[h]) for h in range(2))

    def values_of(j, p_heads):
        start = j * tk
        return [_mm(jnp.concatenate([vt_ref[h * HEAD_DIM:(h + 1) * HEAD_DIM, pl.ds(start, tk)], ones], axis=0),
                    p_heads[h]) for h in range(2)]

    def softmax(s, m, diagonal):
        if diagonal:
            s = jnp.where(key_l <= qry_l, s, NEG_INF)
        m_new = jnp.maximum(m, jnp.max(s, axis=0, keepdims=True))
        return m_new, jnp.exp(m - m_new), jnp.exp(s - m_new).astype(BF16)

    def run(n_blocks, other):
        m_heads = [jnp.full((1, tq), NEG_INF, F32)] * 2
        acc_heads = [jnp.zeros((HEAD_DIM + SUM_ROWS, tq), F32)] * 2
        next(other)
        s_cur = scores_of(0)
        for j in range(n_blocks):
            s_next = scores_of(j + 1) if j + 1 < n_blocks else None
            if j == 0:
                next(other)
            if j == 1:
                for _ in other:
                    pass
            p_cur, alphas = [], []
            for h in range(2):
                m_heads[h], alpha, p = softmax(s_cur[h], m_heads[h], j + 1 == n_blocks)
                alphas.append(alpha)
                p_cur.append(p)
            pv = values_of(j, p_cur)
            if n_blocks == 1:
                for _ in other:
                    pass
            acc_heads = [alphas[h] * acc_heads[h] + pv[h] for h in range(2)]
            s_cur = s_next
        out = [acc[:HEAD_DIM, :] / acc[HEAD_DIM:HEAD_DIM + 1, :] for acc in acc_heads]
        o_ref[...] = jnp.concatenate(out, axis=0).T

    def branch(q_block):
        run(q_block + 1, with_step(q_block))

    for q_block in range(seq // tq):
        pl.when(qi == q_block)(functools.partial(branch, q_block))


def _ret_tables(lg_ref, length):
    n = RET_CHUNK
    lg_a = lg_ref[0:1, :]
    lg_b = lg_ref[1:2, :]
    lg_m = lg_ref[2:3, :]
    row = lax.broadcasted_iota(jnp.int32, (n, n), 0)
    col = lax.broadcasted_iota(jnp.int32, (n, n), 1)
    diff = row - col
    lower = diff >= 0
    dpos = jnp.where(lower, diff, 0).astype(F32)
    decay = (jnp.where(lower, jnp.exp(dpos * lg_a), 0.0), jnp.where(lower, jnp.exp(dpos * lg_b), 0.0))
    rowf = row.astype(F32)
    q_dec = jnp.exp((rowf + 1.0) * lg_m)
    k_dec = jnp.exp((length - 1.0 - rowf) * lg_m)
    s_dec = jnp.exp(float(length) * jnp.where(row < HEAD_DIM, lg_a, lg_b))
    same_head = (row < HEAD_DIM) == (col < HEAD_DIM)
    return decay, q_dec, k_dec, s_dec, same_head


def _group_norm(o, gn_w):
    first = _pair_masks(o.shape)
    inv = 1.0 / HEAD_DIM
    s0 = jnp.sum(jnp.where(first, o, 0.0), axis=-1, keepdims=True)
    s1 = jnp.sum(jnp.where(first, 0.0, o), axis=-1, keepdims=True)
    d = o - jnp.where(first, s0, s1) * inv
    dd = d * d
    v0 = jnp.sum(jnp.where(first, dd, 0.0), axis=-1, keepdims=True)
    v1 = jnp.sum(jnp.where(first, 0.0, dd), axis=-1, keepdims=True)
    var = jnp.where(first, v0, v1) * inv
    return (d * lax.rsqrt(var + NORM_EPS)) * gn_w


def _ret_prompt_kernel(q_ref, k_ref, v_ref, lg_ref, gn_ref, o_ref, s_ref, *, n_chunks):
    tables = _ret_tables(lg_ref, RET_CHUNK)
    gn_w = gn_ref[...]

    decay, q_dec, k_dec, s_dec, same_head = tables
    group = RET_GROUP

    def body(cc, state):
        rows = [pl.ds(pl.multiple_of((cc * group + g) * RET_CHUNK, RET_CHUNK), RET_CHUNK)
                for g in range(group)]
        qs = [q_ref[r, :] for r in rows]
        ks = [k_ref[r, :] for r in rows]
        vs = [v_ref[r, :] for r in rows]
        first = _pair_masks(qs[0].shape)
        zero = jnp.zeros_like(qs[0])
        kbs = [k.astype(BF16) for k in ks]
        scores = [[_mm_nt(jnp.where(first, q, zero) if h == 0 else jnp.where(first, zero, q), kb)
                   for h in range(2)] for q, kb in zip(qs, kbs)]
        incs = [jnp.where(same_head, _mm((k * k_dec).T.astype(BF16), v), 0.0) for k, v in zip(ks, vs)]
        intra = [[_mm((sc[h] * decay[h]).astype(BF16), v) for h in range(2)] for sc, v in zip(scores, vs)]
        states = []
        for g in range(group):
            states.append(state)
            state = state * s_dec + incs[g]
        cross = [_mm(q, s.astype(BF16)) * q_dec for q, s in zip(qs, states)]
        for g in range(group):
            o = jnp.where(first, intra[g][0], intra[g][1]) + cross[g]
            o_ref[rows[g], :] = _group_norm(o, gn_w)
        return state

    state = lax.fori_loop(0, n_chunks // group, body, jnp.zeros((PAIR_W, PAIR_W), F32))
    s_ref[0] = state[:HEAD_DIM, :HEAD_DIM]
    s_ref[1] = state[HEAD_DIM:, HEAD_DIM:]


def _ret_prompt(rq, rk, rv, lg_tab, gn_w):
    b, l, _ = rq.shape
    seq = pl.BlockSpec((None, l, PAIR_W), lambda bi, hp: (bi, 0, hp))
    return pl.pallas_call(
        functools.partial(_ret_prompt_kernel, n_chunks=l // RET_CHUNK),
        grid=(b, N_PAIRS),
        in_specs=[seq, seq, seq,
                  pl.BlockSpec((None, 8, PAIR_W), lambda bi, hp: (hp, 0, 0)),
                  pl.BlockSpec((1, PAIR_W), lambda bi, hp: (0, hp))],
        out_specs=[seq, pl.BlockSpec((None, 2, HEAD_DIM, HEAD_DIM), lambda bi, hp: (bi, hp, 0, 0))],
        out_shape=[jax.ShapeDtypeStruct((b, l, HEAD_W), F32),
                   jax.ShapeDtypeStruct((b, N_HEADS, HEAD_DIM, HEAD_DIM), F32)],
        compiler_params=_params("parallel", "parallel"),
        name="ret_prompt",
    )(rq, rk, rv, lg_tab, gn_w)


def _ret_sample_kernel(q_ref, k_ref, v_ref, s0_ref, lg_ref, gn_ref, o_ref, s_ref, *, length):
    pad = RET_CHUNK - length
    zrow_f = jnp.zeros((pad, PAIR_W), F32)
    zblk = jnp.zeros((HEAD_DIM, HEAD_DIM), F32)
    pairs = range(N_PAIRS)
    lanes = [slice(hp * PAIR_W, (hp + 1) * PAIR_W) for hp in pairs]
    tables = [_ret_tables(lg_ref.at[hp], length) for hp in pairs]
    qs = [jnp.concatenate([q_ref[:, ln].astype(F32), zrow_f], axis=0).astype(BF16) for ln in lanes]
    ks = [jnp.concatenate([k_ref[:, ln], zrow_f], axis=0) for ln in lanes]
    vs = [jnp.concatenate([v_ref[:, ln].astype(F32), zrow_f], axis=0).astype(BF16) for ln in lanes]
    states = [jnp.concatenate([jnp.concatenate([s0_ref[2 * hp], zblk], axis=1),
                               jnp.concatenate([zblk, s0_ref[2 * hp + 1]], axis=1)], axis=0) for hp in pairs]
    first = _pair_masks(qs[0].shape)
    zero = jnp.zeros_like(qs[0])
    kbs = [k.astype(BF16) for k in ks]
    scores = [[_mm_nt(jnp.where(first, q, zero) if h == 0 else jnp.where(first, zero, q), kb)
               for h in range(2)] for q, kb in zip(qs, kbs)]
    cross = [_mm(q, s.astype(BF16)) * t[1] for q, s, t in zip(qs, states, tables)]
    incs = [jnp.where(t[4], _mm((k * t[2]).T.astype(BF16), v), 0.0) for k, v, t in zip(ks, vs, tables)]
    intra = [[_mm((sc[h] * t[0][h]).astype(BF16), v) for h in range(2)]
             for sc, v, t in zip(scores, vs, tables)]
    for hp in pairs:
        o = jnp.where(first, intra[hp][0], intra[hp][1]) + cross[hp]
        o_ref[:, lanes[hp]] = _group_norm(o[:length], gn_ref[:, lanes[hp]])
        state = states[hp] * tables[hp][3] + incs[hp]
        s_ref[2 * hp] = state[:HEAD_DIM, :HEAD_DIM]
        s_ref[2 * hp + 1] = state[HEAD_DIM:, HEAD_DIM:]


def _ret_sample(rq, rk, rv, state, lg_tab, gn_w, *, layer, length):
    rows = rq.shape[0]
    b = rows // length
    tok = pl.BlockSpec((length, HEAD_W), lambda bi: (bi, 0))
    st_in = pl.BlockSpec((None, None, N_HEADS, HEAD_DIM, HEAD_DIM), lambda bi: (layer, bi, 0, 0, 0))
    st = pl.BlockSpec((None, N_HEADS, HEAD_DIM, HEAD_DIM), lambda bi: (bi, 0, 0, 0))
    return pl.pallas_call(
        functools.partial(_ret_sample_kernel, length=length),
        grid=(b,),
        in_specs=[tok, tok, tok, st_in, _const_spec(lg_tab.shape), _const_spec(gn_w.shape)],
        out_specs=[tok, st],
        out_shape=[jax.ShapeDtypeStruct((rows, HEAD_W), F32),
                   jax.ShapeDtypeStruct(state.shape[1:], F32)],
        compiler_params=_params("parallel"),
        name="ret_sample",
    )(rq, rk, rv, state, lg_tab, gn_w)


def _ple(h1, p, proj, pnw, pgw):
    e = _rms(_mm(p.astype(BF16), proj), pnw)
    return h1 + jax.nn.sigmoid(_mm(h1.astype(BF16), pgw)) * e


def _silu(g):
    return g * jax.nn.sigmoid(g)


def _even_out_kernel(h_ref, fo_ref, ro_ref, g_ref, wo_ref, p_ref, proj_ref, pnw_ref, pgw_ref, o_ref):
    sg = _silu(g_ref[...])
    cat_f = (fo_ref[...] * sg[:, :HEAD_W]).astype(BF16)
    cat_r = (ro_ref[...] * sg[:, HEAD_W:]).astype(BF16)
    mix = _mm(cat_f, wo_ref[:HEAD_W, :]) + _mm(cat_r, wo_ref[HEAD_W:, :])
    o_ref[...] = _ple(h_ref[...] + mix, p_ref[...], proj_ref[...], pnw_ref[...], pgw_ref[...])


def _even_out(h, fo, ro, g, wo, p, proj, pnw, pgw, *, layer, tm):
    rows, d = h.shape
    row = lambda w: pl.BlockSpec((tm, w), lambda i: (i, 0))
    return pl.pallas_call(
        _even_out_kernel,
        grid=(rows // tm,),
        in_specs=[row(d), row(HEAD_W), row(HEAD_W), row(2 * HEAD_W), _const_spec(wo.shape),
                  pl.BlockSpec((None, tm, p.shape[-1]), lambda i: (layer, i, 0)),
                  _const_spec(proj.shape), _const_spec(pnw.shape), _const_spec(pgw.shape)],
        out_specs=row(d),
        out_shape=jax.ShapeDtypeStruct((rows, d), F32),
        compiler_params=_params("parallel"),
        name="even_out",
    )(h, fo, ro, g, wo, p, proj, pnw, pgw)


def _odd_kernel(h_ref, nw_ref, win_ref, cw_ref, cb_ref, gw_ref, grb_ref, gib_ref, lam_ref, wout_ref,
                p_ref, proj_ref, pnw_ref, pgw_ref, h0_ref, cbuf_ref, fnw_ref,
                o_ref, lruh_ref, lruc_ref,
                ext_scr, carry_scr, y_scr, sg_scr, *, tl, final_norm):
    nb, _, d = h_ref.shape
    w = wout_ref.shape[0]
    rows = nb * tl
    halo = (CONV_W - 1) * nb

    @pl.when(pl.program_id(1) == 0)
    def _():
        carry_scr[...] = h0_ref[...]
        ext_scr[0:halo, :] = cbuf_ref[...].reshape(halo, w)

    x = jnp.swapaxes(h_ref[...], 0, 1).reshape(rows, d)
    z = _mm(_rms(x, nw_ref[...]).astype(BF16), win_ref[...])
    xb = z[:, :w]
    sg_scr[...] = _silu(z[:, w:])
    ext_scr[halo:halo + rows, :] = xb
    xc = xb * cw_ref[CONV_W - 1:CONV_W, :] + cb_ref[...]
    for kk in range(1, CONV_W):
        xc = xc + ext_scr[halo - kk * nb:halo - kk * nb + rows, :] * cw_ref[CONV_W - 1 - kk:CONV_W - kk, :]
    last_steps = ext_scr[rows:rows + halo, :]
    lruc_ref[...] = last_steps.reshape(CONV_W - 1, nb, w)
    ext_scr[0:halo, :] = last_steps

    sp = jax.nn.softplus(-lam_ref[...])
    for n in range(LRU_BLOCKS):
        lanes = slice(n * LRU_BW, (n + 1) * LRU_BW)
        xn = xc[:, lanes]
        zz = _mm(xn.astype(BF16), gw_ref[n])
        r = jax.nn.sigmoid(zz[:, :LRU_BW] + grb_ref[:, lanes])
        gi = jax.nn.sigmoid(zz[:, LRU_BW:] + gib_ref[:, lanes])
        log_a = (-LRU_C * r) * sp[:, lanes]
        a = jnp.exp(log_a)
        one_minus = -jnp.tanh(log_a) * (a * a + 1.0)
        root = jnp.where(one_minus > 0.0, one_minus * lax.rsqrt(one_minus), 0.0)
        b = (root * gi) * xn
        hs = carry_scr[:, lanes]
        for t in range(tl):
            step = slice(t * nb, (t + 1) * nb)
            hs = a[step] * hs + b[step]
            y_scr[step, lanes] = hs
        carry_scr[:, lanes] = hs
        lruh_ref[:, lanes] = hs

    mix = _mm((y_scr[...] * sg_scr[...]).astype(BF16), wout_ref[...])
    p = jnp.swapaxes(p_ref[...], 0, 1).reshape(rows, -1)
    h2 = _ple(x + mix, p, proj_ref[...], pnw_ref[...], pgw_ref[...])
    if final_norm:
        h2 = _rms(h2, fnw_ref[...])
    o_ref[...] = jnp.swapaxes(h2.reshape(tl, nb, d), 0, 1)


def _odd_layer(h, nw, win, cw, cb, gw, grb, gib, lam, wout, p, proj, pnw, pgw, h0, cbuf, fnw,
               *, layer, nb, tl, final_norm):
    b, l, d = h.shape
    w = wout.shape[0]
    seq = lambda width: pl.BlockSpec((nb, tl, width), lambda bi, li: (bi, li, 0))
    state = pl.BlockSpec((nb, w), lambda bi, li: (bi, 0))
    conv = pl.BlockSpec((CONV_W - 1, nb, w), lambda bi, li: (0, bi, 0))
    consts = [nw, win, cw, cb, gw, grb, gib, lam, wout]
    tail = [proj, pnw, pgw]
    rows = nb * tl
    return pl.pallas_call(
        functools.partial(_odd_kernel, tl=tl, final_norm=final_norm),
        grid=(b // nb, l // tl),
        in_specs=[seq(d)] + [_const_spec(c.shape) for c in consts]
                 + [pl.BlockSpec((None, nb, tl, p.shape[-1]), lambda bi, li: (layer, bi, li, 0))]
                 + [_const_spec(c.shape) for c in tail] + [state, conv, _const_spec(fnw.shape)],
        out_specs=[seq(d), state, conv],
        out_shape=[jax.ShapeDtypeStruct((b, l, d), F32),
                   jax.ShapeDtypeStruct((b, w), F32),
                   jax.ShapeDtypeStruct((CONV_W - 1, b, w), F32)],
        scratch_shapes=[pltpu.VMEM(((CONV_W - 1) * nb + rows, w), F32), pltpu.VMEM((nb, w), F32),
                        pltpu.VMEM((rows, w), F32), pltpu.VMEM((rows, w), F32)],
        compiler_params=_params("parallel", "arbitrary"),
        name="odd_layer",
    )(h, *consts, p, *tail, h0, cbuf, fnw)


def _fox_sample_body(c, n_steps, q_ref, kn_ref, vn_ref, cnc_ref, cnr_ref, tri_ref, k_pages, v_pages, f_pages,
                     o_ref, qbd_scr, m_scr, l_scr, acc_scr, rel_scr, *, t_new):
    pages_per_step = len(k_pages)
    cn_col = cnc_ref[...]

    def start():
        rel_scr[...] = jnp.zeros_like(rel_scr)
        q = q_ref[...].astype(F32)
        lane_head = lax.broadcasted_iota(jnp.int32, q.shape, 1) // HEAD_DIM
        zero = jnp.zeros_like(q)
        for h in range(N_HEADS):
            qbd_scr[h * t_new:(h + 1) * t_new, :] = jnp.where(lane_head == h, q, zero)
        m_scr[...] = jnp.full(m_scr.shape, NEG_INF, F32)
        l_scr[...] = jnp.zeros_like(l_scr)
        acc_scr[...] = jnp.zeros_like(acc_scr)

    if c == 0:
        start()
    qbd = qbd_scr[...].astype(BF16)

    def online(s, pv_fn):
        m = m_scr[...]
        m_new = jnp.maximum(m, jnp.max(s, axis=-1, keepdims=True))
        alpha = jnp.exp(m - m_new)
        p = jnp.exp(s - m_new)
        l_scr[...] = alpha * l_scr[...] + jnp.sum(p, axis=-1, keepdims=True)
        acc_scr[...] = alpha * acc_scr[...] + pv_fn(p.astype(BF16))
        m_scr[...] = m_new

    xs = jnp.concatenate([f[...] for f in f_pages], axis=0)
    hi = xs.astype(BF16)
    rest = xs - hi.astype(F32)
    mid = rest.astype(BF16)
    lo = (rest - mid.astype(F32)).astype(BF16)
    tri = tri_ref[...]
    sums = (_mm(hi, tri) + _mm(mid, tri)) + _mm(lo, tri)
    after = rel_scr[...]
    bias = [None] * pages_per_step
    for i in reversed(range(pages_per_step)):
        rows = slice(i * N_HEADS, (i + 1) * N_HEADS)
        rel = sums[rows, :PAGE] + after
        after = after + sums[rows, PAGE:]
        bias[i] = jnp.concatenate(
            [jnp.broadcast_to(rel[h:h + 1, :], (t_new, PAGE)) for h in range(N_HEADS)], axis=0)
    rel_scr[...] = after

    kt = jnp.concatenate([k[...].astype(BF16) for k in k_pages], axis=1)
    vt = jnp.concatenate([v[...].astype(BF16) for v in v_pages], axis=1)
    s = (_mm(qbd, kt) + cn_col) + jnp.concatenate(bias, axis=1)
    yield
    m = m_scr[...]
    m_new = jnp.maximum(m, jnp.max(s, axis=-1, keepdims=True))
    alpha = jnp.exp(m - m_new)
    p = jnp.exp(s - m_new)
    yield
    l_scr[...] = alpha * l_scr[...] + jnp.sum(p, axis=-1, keepdims=True)
    acc_scr[...] = alpha * acc_scr[...] + _mm_nt(p.astype(BF16), vt)
    m_scr[...] = m_new

    def finish():
        zpad = jnp.zeros((PAGE - t_new, HEAD_W), F32)
        kn = jnp.concatenate([kn_ref[...], zpad], axis=0).astype(BF16)
        vn = jnp.concatenate([vn_ref[...], zpad], axis=0).astype(BF16)
        s = (_mm_nt(qbd, kn) + cn_col) - cnr_ref[...]
        t_query = lax.broadcasted_iota(jnp.int32, s.shape, 0) % t_new
        t_key = lax.broadcasted_iota(jnp.int32, s.shape, 1)
        s = jnp.where(t_query >= t_key, s, NEG_INF)
        online(s, lambda p: _mm(p, vn))
        o = acc_scr[...] / l_scr[...]
        lane_head = lax.broadcasted_iota(jnp.int32, (t_new, HEAD_W), 1) // HEAD_DIM
        out = jnp.zeros((t_new, HEAD_W), F32)
        for h in range(N_HEADS):
            out = jnp.where(lane_head == h, o[h * t_new:(h + 1) * t_new, :], out)
        o_ref[...] = out

    if c == n_steps - 1:
        finish()


N_FOX_PROMPT_INPUTS = 5
N_FOX_SAMPLE_INPUTS = 6


N_PAGE_ARRAYS = 3
N_PAGE_SLOTS = 2


def _fox_kernel(tbl_ref, *refs, tq, tk, t_new, pages_per_step, layer):
    prompt_in = refs[:N_FOX_PROMPT_INPUTS]
    sample_in = refs[N_FOX_PROMPT_INPUTS:N_FOX_PROMPT_INPUTS + N_FOX_SAMPLE_INPUTS]
    rest = refs[N_FOX_PROMPT_INPUTS + N_FOX_SAMPLE_INPUTS:]
    caches = rest[:N_PAGE_ARRAYS]
    o_prompt_ref, o_sample_ref, kaug_scr = rest[N_PAGE_ARRAYS:N_PAGE_ARRAYS + 3]
    sample_scr = rest[N_PAGE_ARRAYS + 3:N_PAGE_ARRAYS + 8]
    page_bufs = rest[N_PAGE_ARRAYS + 8:2 * N_PAGE_ARRAYS + 8]
    sems = rest[2 * N_PAGE_ARRAYS + 8]
    n_steps = prompt_in[1].shape[0] // tq
    seq = pl.program_id(0) * N_PAIRS + pl.program_id(1)
    n_seqs = pl.num_programs(0) * N_PAIRS

    def page_copies(of_seq, step):
        slot = step % N_PAGE_SLOTS
        copies = []
        for i in range(pages_per_step):
            page = tbl_ref[of_seq, (n_steps - 1 - step) * pages_per_step + i]
            for a in range(N_PAGE_ARRAYS):
                copies.append(pltpu.make_async_copy(
                    caches[a].at[layer, page], page_bufs[a].at[slot, i], sems.at[slot, a]))
        return copies

    def start(copies):
        for cp in copies:
            cp.start()

    def sample_step(c):
        if c == 0:
            pl.when(seq == 0)(lambda: start(page_copies(seq, 0)))
        if c + 1 < n_steps:
            start(page_copies(seq, c + 1))
        else:
            pl.when(seq + 1 < n_seqs)(lambda: start(page_copies(seq + 1, 0)))
        for cp in page_copies(seq, c):
            cp.wait()
        slot = c % N_PAGE_SLOTS
        views = [[buf.at[slot, i] for i in range(pages_per_step)] for buf in page_bufs]
        return _fox_sample_body(c, n_steps, *sample_in, *views, o_sample_ref, *sample_scr, t_new=t_new)

    _fox_prompt_body(pl.program_id(2), *prompt_in, o_prompt_ref, kaug_scr, tq=tq, tk=tk, with_step=sample_step)


def _fox(qt, kb, vtb, cum_rows, q, k_new, v_new, cn_col, cn_rows, cache_kt, cache_vt, cache_ft, table, layer,
         *, tq, t_new):
    b, l, _ = kb.shape
    bs, n_pages = table.shape
    n_steps = l // tq
    assert bs == b * N_PAIRS and n_pages % n_steps == 0, "one sample sequence per prompt (batch, head pair)"
    pages_per_step = n_pages // n_steps
    n_rows = N_HEADS * t_new
    seq_of = lambda bi, hp: bi * N_PAIRS + hp
    tok = pl.BlockSpec((t_new, HEAD_W), lambda bi, hp, qi, tbl: (seq_of(bi, hp), 0))

    assert n_steps % N_PAGE_SLOTS == 0, "a sequence must start on buffer slot 0"
    caches = [cache_kt, cache_vt, cache_ft]
    page_bufs = [pltpu.VMEM((N_PAGE_SLOTS, pages_per_step) + c.shape[2:], F32) for c in caches]
    pos = jnp.arange(PAGE)
    later = (pos[:, None] > pos[None, :]).astype(BF16)
    tri = jnp.concatenate([later, jnp.ones((PAGE, PAGE), BF16)], axis=1)
    prompt_specs = [pl.BlockSpec((None, PAIR_W, tq), lambda bi, hp, qi, tbl: (bi, hp, qi)),
                    pl.BlockSpec((None, l, PAIR_W), lambda bi, hp, qi, tbl: (bi, 0, hp)),
                    pl.BlockSpec((None, PAIR_W, l), lambda bi, hp, qi, tbl: (bi, hp, 0)),
                    pl.BlockSpec((None, None, 2, tq), lambda bi, hp, qi, tbl: (bi, hp, 0, qi)),
                    pl.BlockSpec((None, None, 2, l), lambda bi, hp, qi, tbl: (bi, hp, 0, 0))]
    sample_specs = [tok, tok, tok,
                    pl.BlockSpec((None, n_rows, 1), lambda bi, hp, qi, tbl: (seq_of(bi, hp), 0, 0)),
                    pl.BlockSpec((None, n_rows, PAGE), lambda bi, hp, qi, tbl: (seq_of(bi, hp), 0, 0)),
                    pl.BlockSpec(tri.shape, lambda bi, hp, qi, tbl: (0, 0))]
    assert len(prompt_specs) == N_FOX_PROMPT_INPUTS and len(sample_specs) == N_FOX_SAMPLE_INPUTS
    return pl.pallas_call(
        functools.partial(_fox_kernel, tq=tq, tk=tq, t_new=t_new, pages_per_step=pages_per_step, layer=layer),
        grid_spec=pltpu.PrefetchScalarGridSpec(
            num_scalar_prefetch=1,
            grid=(b, N_PAIRS, n_steps),
            in_specs=prompt_specs + sample_specs + [pl.BlockSpec(memory_space=pl.ANY)] * N_PAGE_ARRAYS,
            out_specs=[pl.BlockSpec((None, tq, PAIR_W), lambda bi, hp, qi, tbl: (bi, qi, hp)), tok],
            scratch_shapes=[pltpu.VMEM((l, 2 * PAIR_W), BF16),
                            pltpu.VMEM((n_rows, HEAD_W), F32), pltpu.VMEM((n_rows, 1), F32),
                            pltpu.VMEM((n_rows, 1), F32), pltpu.VMEM((n_rows, HEAD_W), F32),
                            pltpu.VMEM((N_HEADS, PAGE), F32)]
                           + page_bufs + [pltpu.SemaphoreType.DMA((N_PAGE_SLOTS, N_PAGE_ARRAYS))]),
        out_shape=[jax.ShapeDtypeStruct((b, l, HEAD_W), F32),
                   jax.ShapeDtypeStruct((bs * t_new, HEAD_W), F32)],
        compiler_params=_params("arbitrary", "arbitrary", "arbitrary"),
        name="fox",
    )(table, qt, kb, vtb, cum_rows, cum_rows, q, k_new, v_new, cn_col, cn_rows, tri, *caches)


def _rope_tables(pos):
    inv = ROPE_BASE ** (-np.arange(HALF_DIM, dtype=np.float64) / HALF_DIM)
    ang = np.asarray(pos, np.float64)[:, None] * inv[None, :]
    cos_head = np.concatenate([np.cos(ang), np.cos(ang)], axis=-1)
    sin_head = np.concatenate([-np.sin(ang), np.sin(ang)], axis=-1)
    return (jnp.asarray(np.tile(cos_head, (1, N_HEADS)), F32), jnp.asarray(np.tile(sin_head, (1, N_HEADS)), F32))


def _log_gamma_table(log_gamma):
    pairs = log_gamma.reshape(N_PAIRS, 2)
    ones = jnp.ones((N_PAIRS, 1, PAIR_W), F32)
    first = pairs[:, 0][:, None, None] * ones
    second = pairs[:, 1][:, None, None] * ones
    mix = jnp.repeat(pairs, HEAD_DIM, axis=1)[:, None, :]
    return jnp.concatenate([first, second, mix, jnp.zeros((N_PAIRS, 5, PAIR_W), F32)], axis=1)


def kernel(x_prompt, x_sample, cache_fox_k, cache_fox_v, cache_fox_logf, page_table, state_ret, state_lru_h, state_lru_conv, p_prompt, p_sample, norm_w, w_in_even, b_forget, ret_gn_w, w_out_even, w_in_odd, conv_w, conv_b, gate_r_w, gate_r_b, gate_i_w, gate_i_b, lru_lambda, w_out_odd, ple_proj, ple_norm_w, ple_gate_w, final_norm_w):
    bp, lp, d = x_prompt.shape
    bs, ls, _ = x_sample.shape
    depth = norm_w.shape[0]
    n_pool = cache_fox_k.shape[1]
    n_pages = page_table.shape[1]
    past_len = n_pages * PAGE
    lru_w = w_out_odd.shape[1]
    rows_p, rows_s = bp * lp, bs * ls
    tm_p = ROW_TILE

    log_gamma = jnp.log1p(-jnp.exp2(-5.0 - jnp.arange(N_HEADS, dtype=F32)))
    lg_tab = _log_gamma_table(log_gamma)
    cos_p, sin_p = _rope_tables(np.arange(lp))
    cos_s, sin_s = _rope_tables(np.tile(past_len + np.arange(ls), bs))

    cache_kt = cache_fox_k.transpose(0, 1, 3, 4, 2).reshape(-1, n_pool, HEAD_W, PAGE)
    cache_vt = cache_fox_v.transpose(0, 1, 3, 4, 2).reshape(-1, n_pool, HEAD_W, PAGE)
    cache_ft = jnp.swapaxes(cache_fox_logf, 2, 3)

    row2 = lambda v: v.reshape(1, -1)
    hp = x_prompt.reshape(rows_p, d)
    hs = x_sample.reshape(rows_s, d)
    outs = {k: [] for k in ("fl_p", "rs_p", "lh_p", "lc_p", "fk_s", "fv_s", "fl_s", "rs_s", "lh_s", "lc_s")}
    y_p = y_s = None
    n_even = w_in_even.shape[0]
    stacked_kv = None

    for i in range(depth):
        j = i // 2
        proj = ple_proj[i].astype(BF16)
        pgw = ple_gate_w[i].astype(BF16)
        pnw = row2(ple_norm_w[i])
        nw = row2(norm_w[i])
        if i % 2 == 0:
            w = w_in_even[j]
            c0 = 3 * HEAD_W
            wf = w[:, :c0].astype(BF16)
            wl = jnp.pad(w[:, c0:c0 + N_HEADS], ((0, 0), (0, PAGE - N_HEADS))).astype(BF16)
            wr = w[:, c0 + N_HEADS:2 * c0 + N_HEADS].astype(BF16)
            wg = w[:, 2 * c0 + N_HEADS:].astype(BF16)
            bf = jnp.pad(b_forget[j], (0, PAGE - N_HEADS)).reshape(1, PAGE)
            wo = w_out_even[j].astype(BF16)
            gn = row2(ret_gn_w[j])
            wt = jnp.swapaxes(w, 0, 1)
            wqkvt = wt[:c0].astype(BF16)
            wlt = jnp.pad(wt[c0:c0 + N_HEADS], ((0, N_HEADS), (0, 0))).astype(BF16)
            bfc = jnp.pad(b_forget[j], (0, N_HEADS)).reshape(2 * N_HEADS, 1)

            fqt, fkt_all, fvt_all, fkb, fvtb, lft, rq, rk, rv, g = _even_in_prompt(
                hp.reshape(bp, lp, d), nw, wqkvt, wlt, wr, wg, bfc, cos_p, sin_p, stacked_kv,
                layer=j, n_layers=n_even, tm=tm_p)
            stacked_kv = (fkt_all, fvt_all)
            cum_rows = _seg_cumsum(lft.reshape(bp * N_HEADS, lp), seg=lp).reshape(bp, N_PAIRS, 2, lp)
            fq, fk, fv, lf, lft_s, rq_s, rk_s, rv_s, g_s = _even_in_sample(
                hs, nw, wf, wl, wr, wg, bf, cos_s, sin_s)
            cn = _seg_cumsum(lft_s, seg=ls).reshape(N_HEADS, bs, ls).transpose(1, 0, 2)
            cn_col = cn.reshape(bs, N_HEADS * ls, 1)
            cn_rows = jnp.pad(jnp.repeat(cn, ls, axis=1), ((0, 0), (0, 0), (0, PAGE - ls)))
            fo, fo_s = _fox(fqt, fkb, fvtb, cum_rows, fq, fk, fv, cn_col, cn_rows,
                            cache_kt, cache_vt, cache_ft, page_table, j, tq=FOX_BLOCK, t_new=ls)

            ro, s_p = _ret_prompt(rq, rk, rv, lg_tab, gn)
            hp = _even_out(hp, fo.reshape(rows_p, HEAD_W), ro.reshape(rows_p, HEAD_W),
                           g.reshape(rows_p, 2 * HEAD_W), wo, p_prompt.reshape(depth, rows_p, -1),
                           proj, pnw, pgw, layer=i, tm=tm_p)
            outs["fl_p"].append(jnp.swapaxes(lft, 1, 2))
            outs["rs_p"].append(s_p)

            ro, s_s = _ret_sample(rq_s, rk_s, rv_s, state_ret, lg_tab, gn, layer=j, length=ls)
            hs = _even_out(hs, fo_s, ro, g_s, wo, p_sample.reshape(depth, rows_s, -1),
                           proj, pnw, pgw, layer=i, tm=rows_s)
            outs["fk_s"].append(fk.reshape(bs, ls, N_HEADS, HEAD_DIM))
            outs["fv_s"].append(fv.reshape(bs, ls, N_HEADS, HEAD_DIM))
            outs["fl_s"].append(lf.reshape(bs, ls, N_HEADS))
            outs["rs_s"].append(s_s)
        else:
            final = i == depth - 1
            gw = jnp.concatenate([gate_r_w[j], gate_i_w[j]], axis=-1).astype(BF16)
            args = (nw, w_in_odd[j].astype(BF16), conv_w[j], row2(conv_b[j]), gw, row2(gate_r_b[j]),
                    row2(gate_i_b[j]), row2(lru_lambda[j]), w_out_odd[j].astype(BF16))
            tail = (proj, pnw, pgw)
            fnw = row2(final_norm_w)
            o, lh, lc = _odd_layer(hp.reshape(bp, lp, d), *args, p_prompt, *tail,
                                   jnp.zeros((bp, lru_w), F32), jnp.zeros((CONV_W - 1, bp, lru_w), F32),
                                   fnw, layer=i, nb=SEQ_PER_BLOCK, tl=STEPS_PER_BLOCK, final_norm=final)
            hp = o.reshape(rows_p, d)
            outs["lh_p"].append(lh)
            outs["lc_p"].append(jnp.swapaxes(lc, 0, 1))
            o, lh, lc = _odd_layer(hs.reshape(bs, ls, d), *args, p_sample, *tail,
                                   state_lru_h[j], jnp.swapaxes(state_lru_conv[j], 0, 1),
                                   fnw, layer=i, nb=SEQ_PER_BLOCK, tl=ls, final_norm=final)
            hs = o.reshape(rows_s, d)
            outs["lh_s"].append(lh)
            outs["lc_s"].append(jnp.swapaxes(lc, 0, 1))
            if final:
                y_p, y_s = hp.reshape(bp, lp, d), hs.reshape(bs, ls, d)

    st = lambda k: jnp.stack(outs[k])
    to_blhd = lambda t: t.reshape(n_even, bp, N_HEADS, HEAD_DIM, lp).transpose(0, 1, 4, 2, 3)
    return (y_p, y_s, to_blhd(stacked_kv[0]), to_blhd(stacked_kv[1]), st("fl_p"), st("rs_p"), st("lh_p"), st("lc_p"),
            st("fk_s"), st("fv_s"), st("fl_s"), st("rs_s"), st("lh_s"), st("lc_s"))
```

```python
import functools

import jax
import jax.numpy as jnp
import numpy as np
from jax import lax
from jax.experimental import pallas as pl
from jax.experimental.pallas import tpu as pltpu

F32 = jnp.float32
BF16 = jnp.bfloat16

HEAD_DIM = 64
HALF_DIM = HEAD_DIM // 2
N_HEADS = 8
HEAD_W = N_HEADS * HEAD_DIM
PAIR_W = 2 * HEAD_DIM
N_PAIRS = N_HEADS // 2
PAGE = 128
RET_CHUNK = 128
RET_GROUP = 16
ROPE_BASE = 10000.0
LRU_BLOCKS = 8
LRU_BW = 128
CONV_W = 4
SEQ_PER_BLOCK = 8
STEPS_PER_BLOCK = 64
ODD_PARTS = 2
ROW_TILE = 512
FOX_BLOCK = 512
LRU_C = 8.0
NORM_EPS = 1e-6
NEG_INF = -1e30
QK_SCALE = HEAD_DIM ** -0.5
VMEM_LIMIT = 56 * 1024 * 1024


def _params(*sem):
    return pltpu.CompilerParams(dimension_semantics=sem, vmem_limit_bytes=VMEM_LIMIT)


def _mm(a, b):
    return jnp.dot(a, b, preferred_element_type=F32)


def _mm_nt(a, b):
    return lax.dot_general(a, b, (((1,), (1,)), ((), ())), preferred_element_type=F32)


def _rms(x, w):
    return (x * lax.rsqrt(jnp.mean(x * x, axis=-1, keepdims=True) + NORM_EPS)) * w


def _const_spec(shape):
    nd = len(shape)
    return pl.BlockSpec(shape, lambda *_: (0,) * nd)


def _swap_halves(x):
    n = x.shape[-1]
    lane = lax.broadcasted_iota(jnp.int32, x.shape, x.ndim - 1)
    return jnp.where(lane % HEAD_DIM < HALF_DIM,
                     pltpu.roll(x, n - HALF_DIM, x.ndim - 1),
                     pltpu.roll(x, HALF_DIM, x.ndim - 1))


def _retention_gate_proj(ub, wr_ref, wg_ref, cos_ref, sin_ref, rq_ref, rk_ref, rv_ref, g_ref):
    zr = _mm(ub, wr_ref[...])
    cos = cos_ref[...]
    sin = sin_ref[...]
    rq = zr[:, :HEAD_W]
    rk = zr[:, HEAD_W:2 * HEAD_W]
    rq_ref[...] = (rq * cos + _swap_halves(rq) * sin).astype(BF16)
    rk_ref[...] = (rk * cos + _swap_halves(rk) * sin) * QK_SCALE
    rv_ref[...] = zr[:, 2 * HEAD_W:].astype(BF16)
    g_ref[...] = _mm(ub, wg_ref[...])


N_EVEN_IN_PROMPT_INPUTS = 9


def _even_in_prompt_kernel(*refs):
    h_ref, nw_ref, wqkvt_ref, wlt_ref, wr_ref, wg_ref, bfc_ref, cos_ref, sin_ref = refs[:N_EVEN_IN_PROMPT_INPUTS]
    (fqt_ref, fkt_ref, fvt_ref, fkb_ref, fvtb_ref, lft_ref, rq_ref, rk_ref, rv_ref, g_ref) = refs[-10:]
    ub = _rms(h_ref[...], nw_ref[...]).astype(BF16)
    zt = _mm_nt(wqkvt_ref[...], ub)
    fqt_ref[...] = (zt[:HEAD_W] * QK_SCALE).astype(BF16)
    fkt = zt[HEAD_W:2 * HEAD_W]
    fkt_ref[...] = fkt
    fkb_ref[...] = fkt.T.astype(BF16)
    fvt = zt[2 * HEAD_W:]
    fvt_ref[...] = fvt
    fvtb_ref[...] = fvt.astype(BF16)
    lft_ref[...] = jax.nn.log_sigmoid(_mm_nt(wlt_ref[...], ub) + bfc_ref[...])[:N_HEADS]
    _retention_gate_proj(ub, wr_ref, wg_ref, cos_ref, sin_ref, rq_ref, rk_ref, rv_ref, g_ref)


def _even_in_prompt(h, nw, wqkvt, wlt, wr, wg, bfc, cos, sin, stacked_kv, *, layer, n_layers, tm):
    b, l, d = h.shape
    tok = lambda w: pl.BlockSpec((None, tm, w), lambda bi, i: (bi, i, 0))
    feat = lambda r: pl.BlockSpec((None, r, tm), lambda bi, i: (bi, 0, i))
    stacked = pl.BlockSpec((None, None, HEAD_W, tm), lambda bi, i: (layer, bi, 0, i))
    pos = pl.BlockSpec((tm, HEAD_W), lambda bi, i: (i, 0))
    sd = jax.ShapeDtypeStruct
    consts = [nw, wqkvt, wlt, wr, wg, bfc]
    ins = [h, *consts, cos, sin]
    in_specs = [tok(d)] + [_const_spec(c.shape) for c in consts] + [pos, pos]
    assert len(ins) == N_EVEN_IN_PROMPT_INPUTS
    aliases = {}
    if stacked_kv is not None:
        aliases = {len(ins): 1, len(ins) + 1: 2}
        ins += list(stacked_kv)
        in_specs += [pl.BlockSpec(memory_space=pl.ANY)] * 2
    return pl.pallas_call(
        _even_in_prompt_kernel,
        grid=(b, l // tm),
        in_specs=in_specs,
        out_specs=[feat(HEAD_W), stacked, stacked, tok(HEAD_W), feat(HEAD_W), feat(N_HEADS),
                   tok(HEAD_W), tok(HEAD_W), tok(HEAD_W), tok(2 * HEAD_W)],
        out_shape=[sd((b, HEAD_W, l), BF16), sd((n_layers, b, HEAD_W, l), F32), sd((n_layers, b, HEAD_W, l), F32),
                   sd((b, l, HEAD_W), BF16), sd((b, HEAD_W, l), BF16), sd((b, N_HEADS, l), F32),
                   sd((b, l, HEAD_W), BF16), sd((b, l, HEAD_W), F32), sd((b, l, HEAD_W), BF16),
                   sd((b, l, 2 * HEAD_W), F32)],
        input_output_aliases=aliases,
        compiler_params=_params("parallel", "parallel"),
        name="even_in_prompt",
    )(*ins)


def _even_in_sample_kernel(h_ref, nw_ref, wf_ref, wl_ref, wr_ref, wg_ref, bf_ref, cos_ref, sin_ref,
                           fq_ref, fk_ref, fv_ref, lf_ref, lft_ref, rq_ref, rk_ref, rv_ref, g_ref):
    ub = _rms(h_ref[...], nw_ref[...]).astype(BF16)
    zf = _mm(ub, wf_ref[...])
    fq_ref[...] = (zf[:, :HEAD_W] * QK_SCALE).astype(BF16)
    fk_ref[...] = zf[:, HEAD_W:2 * HEAD_W]
    fv_ref[...] = zf[:, 2 * HEAD_W:]
    logf = jax.nn.log_sigmoid(_mm(ub, wl_ref[...]) + bf_ref[...])
    lf_ref[...] = logf[:, :N_HEADS]
    lft_ref[...] = logf.T[:N_HEADS]
    _retention_gate_proj(ub, wr_ref, wg_ref, cos_ref, sin_ref, rq_ref, rk_ref, rv_ref, g_ref)


def _even_in_sample(h, nw, wf, wl, wr, wg, bf, cos, sin):
    rows = h.shape[0]
    sd = jax.ShapeDtypeStruct
    ins = [h, nw, wf, wl, wr, wg, bf, cos, sin]
    outs = [sd((rows, HEAD_W), BF16), sd((rows, HEAD_W), F32), sd((rows, HEAD_W), F32),
            sd((rows, N_HEADS), F32), sd((N_HEADS, rows), F32),
            sd((rows, HEAD_W), BF16), sd((rows, HEAD_W), F32), sd((rows, HEAD_W), BF16),
            sd((rows, 2 * HEAD_W), F32)]
    return pl.pallas_call(
        _even_in_sample_kernel,
        grid=(1,),
        in_specs=[_const_spec(a.shape) for a in ins],
        out_specs=[_const_spec(o.shape) for o in outs],
        out_shape=outs,
        compiler_params=_params("arbitrary"),
        name="even_in_sample",
    )(*ins)


def _seg_cumsum_kernel(x_ref, o_ref, *, seg):
    x = x_ref[...]
    pos = lax.broadcasted_iota(jnp.int32, x.shape, 1) % seg
    s = 1
    while s < seg:
        x = x + jnp.where(pos >= s, pltpu.roll(x, s, 1), 0.0)
        s *= 2
    o_ref[...] = x


def _seg_cumsum(x, *, seg):
    r, n = x.shape
    return pl.pallas_call(
        functools.partial(_seg_cumsum_kernel, seg=seg),
        grid=(r // 8,),
        in_specs=[pl.BlockSpec((8, n), lambda i: (i, 0))],
        out_specs=pl.BlockSpec((8, n), lambda i: (i, 0)),
        out_shape=jax.ShapeDtypeStruct((r, n), F32),
        compiler_params=_params("parallel"),
        name="seg_cumsum",
    )(x)


def _pair_masks(shape):
    lane = lax.broadcasted_iota(jnp.int32, shape, len(shape) - 1)
    return lane < HEAD_DIM


def _split3(x):
    hi = x.astype(BF16).astype(F32)
    rest = x - hi
    mid = rest.astype(BF16).astype(F32)
    return hi, mid, rest - mid


AUG_ROWS = 8
SUM_ROWS = 16


def _fox_prompt_body(qi, qt_ref, k_ref, vt_ref, cr_ref, call_ref, o_ref, kaug_scr, *, tq, tk, with_step):
    seq = k_ref.shape[0]

    @pl.when(qi == 0)
    def _():
        c_all = call_ref[...]
        one = jnp.ones((1, seq), F32)
        zero_row = jnp.zeros((1, seq), F32)
        rows = []
        for h in range(2):
            rows += list(_split3(-c_all[h:h + 1, :])) + [one] * 3 + [zero_row] * (AUG_ROWS - 6)
        aug_t = jnp.concatenate(rows + [jnp.zeros((PAIR_W - 2 * AUG_ROWS, seq), F32)], axis=0)
        kaug_scr[:, :PAIR_W] = k_ref[...]
        kaug_scr[:, PAIR_W:] = aug_t.T.astype(BF16)

    qt = qt_ref[...]
    row = lax.broadcasted_iota(jnp.int32, qt.shape, 0)
    zero = jnp.zeros_like(qt)
    cr = cr_ref[...]
    w_heads = []
    for h in range(2):
        head_rows = (row < HEAD_DIM) if h == 0 else (row >= HEAD_DIM)
        pieces = _split3(cr[h:h + 1, :])
        bias = jnp.zeros(qt.shape, F32)
        for i in range(3):
            bias = jnp.where(row == h * AUG_ROWS + i, 1.0, bias)
            bias = jnp.where(row == h * AUG_ROWS + 3 + i, pieces[i], bias)
        w_heads.append(jnp.concatenate([jnp.where(head_rows, qt, zero), bias.astype(BF16)], axis=0))
    key_l = lax.broadcasted_iota(jnp.int32, (tk, tq), 0)
    qry_l = lax.broadcasted_iota(jnp.int32, (tk, tq), 1)
    ones = jnp.ones((SUM_ROWS, tk), BF16)

    def scores_of(j):
        kc = kaug_scr[j * tk:(j + 1) * tk, :]
        return tuple(_mm(kc, w_heads[h]) for h in range(2))

    def values_of(j, p_heads):
        start = j * tk
        return [_mm(jnp.concatenate([vt_ref[h * HEAD_DIM:(h + 1) * HEAD_DIM, pl.ds(start, tk)], ones], axis=0),
                    p_heads[h]) for h in range(2)]

    def softmax(s, m, diagonal):
        if diagonal:
            s = jnp.where(key_l <= qry_l, s, NEG_INF)
        m_new = jnp.maximum(m, jnp.max(s, axis=0, keepdims=True))
        return m_new, jnp.exp(m - m_new), jnp.exp(s - m_new).astype(BF16)

    def run(n_blocks, other):
        m_heads = [jnp.full((1, tq), NEG_INF, F32)] * 2
        acc_heads = [jnp.zeros((HEAD_DIM + SUM_ROWS, tq), F32)] * 2
        next(other)
        s_cur = scores_of(0)
        for j in range(n_blocks):
            s_next = scores_of(j + 1) if j + 1 < n_blocks else None
            if j == 0:
                next(other)
            p_cur, alphas = [], []
            for h in range(2):
                m_heads[h], alpha, p = softmax(s_cur[h], m_heads[h], j + 1 == n_blocks)
                alphas.append(alpha)
                p_cur.append(p)
            pv = values_of(j, p_cur)
            if j == 0:
                for _ in other:
                    pass
            acc_heads = [alphas[h] * acc_heads[h] + pv[h] for h in range(2)]
            s_cur = s_next
        out = [acc[:HEAD_DIM, :] / acc[HEAD_DIM:HEAD_DIM + 1, :] for acc in acc_heads]
        o_ref[...] = jnp.concatenate(out, axis=0).T

    def branch(q_block):
        run(q_block + 1, with_step(q_block))

    for q_block in range(seq // tq):
        pl.when(qi == q_block)(functools.partial(branch, q_block))


def _ret_tables(lg_ref, length):
    n = RET_CHUNK
    lg_a = lg_ref[0:1, :]
    lg_b = lg_ref[1:2, :]
    lg_m = lg_ref[2:3, :]
    row = lax.broadcasted_iota(jnp.int32, (n, n), 0)
    col = lax.broadcasted_iota(jnp.int32, (n, n), 1)
    diff = row - col
    lower = diff >= 0
    dpos = jnp.where(lower, diff, 0).astype(F32)
    decay = (jnp.where(lower, jnp.exp(dpos * lg_a), 0.0), jnp.where(lower, jnp.exp(dpos * lg_b), 0.0))
    rowf = row.astype(F32)
    q_dec = jnp.exp((rowf + 1.0) * lg_m)
    k_dec = jnp.exp((length - 1.0 - rowf) * lg_m)
    s_dec = jnp.exp(float(length) * jnp.where(row < HEAD_DIM, lg_a, lg_b))
    same_head = (row < HEAD_DIM) == (col < HEAD_DIM)
    return decay, q_dec, k_dec, s_dec, same_head


def _group_norm(o, gn_w):
    first = _pair_masks(o.shape)
    inv = 1.0 / HEAD_DIM
    s0 = jnp.sum(jnp.where(first, o, 0.0), axis=-1, keepdims=True)
    s1 = jnp.sum(jnp.where(first, 0.0, o), axis=-1, keepdims=True)
    d = o - jnp.where(first, s0, s1) * inv
    dd = d * d
    v0 = jnp.sum(jnp.where(first, dd, 0.0), axis=-1, keepdims=True)
    v1 = jnp.sum(jnp.where(first, 0.0, dd), axis=-1, keepdims=True)
    var = jnp.where(first, v0, v1) * inv
    return (d * lax.rsqrt(var + NORM_EPS)) * gn_w


def _ret_prompt_kernel(q_ref, k_ref, v_ref, lg_ref, gn_ref, o_ref, s_ref, *, n_chunks):
    tables = _ret_tables(lg_ref, RET_CHUNK)
    gn_w = gn_ref[...]

    decay, q_dec, k_dec, s_dec, same_head = tables
    group = RET_GROUP

    def body(cc, state):
        rows = [pl.ds(pl.multiple_of((cc * group + g) * RET_CHUNK, RET_CHUNK), RET_CHUNK)
                for g in range(group)]
        qs = [q_ref[r, :] for r in rows]
        ks = [k_ref[r, :] for r in rows]
        vs = [v_ref[r, :] for r in rows]
        first = _pair_masks(qs[0].shape)
        zero = jnp.zeros_like(qs[0])
        kbs = [k.astype(BF16) for k in ks]
        scores = [[_mm_nt(jnp.where(first, q, zero) if h == 0 else jnp.where(first, zero, q), kb)
                   for h in range(2)] for q, kb in zip(qs, kbs)]
        incs = [jnp.where(same_head, _mm((k * k_dec).T.astype(BF16), v), 0.0) for k, v in zip(ks, vs)]
        intra = [[_mm((sc[h] * decay[h]).astype(BF16), v) for h in range(2)] for sc, v in zip(scores, vs)]
        states = []
        for g in range(group):
            states.append(state)
            state = state * s_dec + incs[g]
        cross = [_mm(q, s.astype(BF16)) * q_dec for q, s in zip(qs, states)]
        for g in range(group):
            o = jnp.where(first, intra[g][0], intra[g][1]) + cross[g]
            o_ref[rows[g], :] = _group_norm(o, gn_w)
        return state

    state = lax.fori_loop(0, n_chunks // group, body, jnp.zeros((PAIR_W, PAIR_W), F32))
    s_ref[0] = state[:HEAD_DIM, :HEAD_DIM]
    s_ref[1] = state[HEAD_DIM:, HEAD_DIM:]


def _ret_prompt(rq, rk, rv, lg_tab, gn_w):
    b, l, _ = rq.shape
    seq = pl.BlockSpec((None, l, PAIR_W), lambda bi, hp: (bi, 0, hp))
    return pl.pallas_call(
        functools.partial(_ret_prompt_kernel, n_chunks=l // RET_CHUNK),
        grid=(b, N_PAIRS),
        in_specs=[seq, seq, seq,
                  pl.BlockSpec((None, 8, PAIR_W), lambda bi, hp: (hp, 0, 0)),
                  pl.BlockSpec((1, PAIR_W), lambda bi, hp: (0, hp))],
        out_specs=[seq, pl.BlockSpec((None, 2, HEAD_DIM, HEAD_DIM), lambda bi, hp: (bi, hp, 0, 0))],
        out_shape=[jax.ShapeDtypeStruct((b, l, HEAD_W), F32),
                   jax.ShapeDtypeStruct((b, N_HEADS, HEAD_DIM, HEAD_DIM), F32)],
        compiler_params=_params("parallel", "parallel"),
        name="ret_prompt",
    )(rq, rk, rv, lg_tab, gn_w)


def _ret_sample_kernel(q_ref, k_ref, v_ref, s0_ref, lg_ref, gn_ref, o_ref, s_ref, *, length):
    pad = RET_CHUNK - length
    zrow_f = jnp.zeros((pad, PAIR_W), F32)
    zblk = jnp.zeros((HEAD_DIM, HEAD_DIM), F32)
    pairs = range(N_PAIRS)
    lanes = [slice(hp * PAIR_W, (hp + 1) * PAIR_W) for hp in pairs]
    tables = [_ret_tables(lg_ref.at[hp], length) for hp in pairs]
    qs = [jnp.concatenate([q_ref[:, ln].astype(F32), zrow_f], axis=0).astype(BF16) for ln in lanes]
    ks = [jnp.concatenate([k_ref[:, ln], zrow_f], axis=0) for ln in lanes]
    vs = [jnp.concatenate([v_ref[:, ln].astype(F32), zrow_f], axis=0).astype(BF16) for ln in lanes]
    states = [jnp.concatenate([jnp.concatenate([s0_ref[2 * hp], zblk], axis=1),
                               jnp.concatenate([zblk, s0_ref[2 * hp + 1]], axis=1)], axis=0) for hp in pairs]
    first = _pair_masks(qs[0].shape)
    zero = jnp.zeros_like(qs[0])
    kbs = [k.astype(BF16) for k in ks]
    scores = [[_mm_nt(jnp.where(first, q, zero) if h == 0 else jnp.where(first, zero, q), kb)
               for h in range(2)] for q, kb in zip(qs, kbs)]
    cross = [_mm(q, s.astype(BF16)) * t[1] for q, s, t in zip(qs, states, tables)]
    incs = [jnp.where(t[4], _mm((k * t[2]).T.astype(BF16), v), 0.0) for k, v, t in zip(ks, vs, tables)]
    intra = [[_mm((sc[h] * t[0][h]).astype(BF16), v) for h in range(2)]
             for sc, v, t in zip(scores, vs, tables)]
    for hp in pairs:
        o = jnp.where(first, intra[hp][0], intra[hp][1]) + cross[hp]
        o_ref[:, lanes[hp]] = _group_norm(o[:length], gn_ref[:, lanes[hp]])
        state = states[hp] * tables[hp][3] + incs[hp]
        s_ref[2 * hp] = state[:HEAD_DIM, :HEAD_DIM]
        s_ref[2 * hp + 1] = state[HEAD_DIM:, HEAD_DIM:]


def _ret_sample(rq, rk, rv, state, lg_tab, gn_w, *, layer, length):
    rows = rq.shape[0]
    b = rows // length
    tok = pl.BlockSpec((length, HEAD_W), lambda bi: (bi, 0))
    st_in = pl.BlockSpec((None, None, N_HEADS, HEAD_DIM, HEAD_DIM), lambda bi: (layer, bi, 0, 0, 0))
    st = pl.BlockSpec((None, N_HEADS, HEAD_DIM, HEAD_DIM), lambda bi: (bi, 0, 0, 0))
    return pl.pallas_call(
        functools.partial(_ret_sample_kernel, length=length),
        grid=(b,),
        in_specs=[tok, tok, tok, st_in, _const_spec(lg_tab.shape), _const_spec(gn_w.shape)],
        out_specs=[tok, st],
        out_shape=[jax.ShapeDtypeStruct((rows, HEAD_W), F32),
                   jax.ShapeDtypeStruct(state.shape[1:], F32)],
        compiler_params=_params("parallel"),
        name="ret_sample",
    )(rq, rk, rv, state, lg_tab, gn_w)


def _ple(h1, p, proj, pnw, pgw):
    e = _rms(_mm(p.astype(BF16), proj), pnw)
    return h1 + jax.nn.sigmoid(_mm(h1.astype(BF16), pgw)) * e


def _silu(g):
    return g * jax.nn.sigmoid(g)


def _even_out_kernel(h_ref, fo_ref, ro_ref, g_ref, wo_ref, p_ref, proj_ref, pnw_ref, pgw_ref, o_ref):
    sg = _silu(g_ref[...])
    cat_f = (fo_ref[...] * sg[:, :HEAD_W]).astype(BF16)
    cat_r = (ro_ref[...] * sg[:, HEAD_W:]).astype(BF16)
    mix = _mm(cat_f, wo_ref[:HEAD_W, :]) + _mm(cat_r, wo_ref[HEAD_W:, :])
    o_ref[...] = _ple(h_ref[...] + mix, p_ref[...], proj_ref[...], pnw_ref[...], pgw_ref[...])


def _even_out(h, fo, ro, g, wo, p, proj, pnw, pgw, *, layer, tm):
    rows, d = h.shape
    row = lambda w: pl.BlockSpec((tm, w), lambda i: (i, 0))
    return pl.pallas_call(
        _even_out_kernel,
        grid=(rows // tm,),
        in_specs=[row(d), row(HEAD_W), row(HEAD_W), row(2 * HEAD_W), _const_spec(wo.shape),
                  pl.BlockSpec((None, tm, p.shape[-1]), lambda i: (layer, i, 0)),
                  _const_spec(proj.shape), _const_spec(pnw.shape), _const_spec(pgw.shape)],
        out_specs=row(d),
        out_shape=jax.ShapeDtypeStruct((rows, d), F32),
        compiler_params=_params("parallel"),
        name="even_out",
    )(h, fo, ro, g, wo, p, proj, pnw, pgw)


def _odd_kernel(h_ref, nw_ref, win_ref, cw_ref, cb_ref, gw_ref, grb_ref, gib_ref, lam_ref, wout_ref,
                p_ref, proj_ref, pnw_ref, pgw_ref, h0_ref, cbuf_ref, fnw_ref,
                o_ref, lruh_ref, lruc_ref,
                ext_scr, carry_scr, y_scr, sg_scr, *, tl, final_norm):
    nb, _, d = h_ref.shape
    w = wout_ref.shape[0]
    halo = (CONV_W - 1) * nb
    parts = ODD_PARTS if tl % (8 * ODD_PARTS) == 0 else 1
    tp = tl // parts
    rows = nb * tp

    @pl.when(pl.program_id(1) == 0)
    def _():
        carry_scr[...] = h0_ref[...]
        ext_scr[0:halo, :] = cbuf_ref[...].reshape(halo, w)

    sp = jax.nn.softplus(-lam_ref[...])
    xs = [None] * parts

    def project_in(i):
        x = jnp.swapaxes(h_ref[:, i * tp:(i + 1) * tp, :], 0, 1).reshape(rows, d)
        xs[i] = x
        z = _mm(_rms(x, nw_ref[...]).astype(BF16), win_ref[...])
        sg_scr[i * rows:(i + 1) * rows, :] = _silu(z[:, w:])
        ext_scr[halo + i * rows:halo + (i + 1) * rows, :] = z[:, :w]

    def recur(i):
        base = halo + i * rows
        xc = ext_scr[base:base + rows, :] * cw_ref[CONV_W - 1:CONV_W, :] + cb_ref[...]
        for kk in range(1, CONV_W):
            xc = xc + ext_scr[base - kk * nb:base - kk * nb + rows, :] * cw_ref[CONV_W - 1 - kk:CONV_W - kk, :]
        for n in range(LRU_BLOCKS):
            lanes = slice(n * LRU_BW, (n + 1) * LRU_BW)
            xn = xc[:, lanes]
            zz = _mm(xn.astype(BF16), gw_ref[n])
            r = jax.nn.sigmoid(zz[:, :LRU_BW] + grb_ref[:, lanes])
            gi = jax.nn.sigmoid(zz[:, LRU_BW:] + gib_ref[:, lanes])
            log_a = (-LRU_C * r) * sp[:, lanes]
            a = jnp.exp(log_a)
            one_minus = -jnp.tanh(log_a) * (a * a + 1.0)
            b = (jnp.sqrt(one_minus) * gi) * xn
            hs = carry_scr[:, lanes]
            for t in range(tp):
                step = slice(t * nb, (t + 1) * nb)
                hs = a[step] * hs + b[step]
                y_scr[i * rows + t * nb:i * rows + (t + 1) * nb, lanes] = hs
            carry_scr[:, lanes] = hs

    def project_out(i):
        part = slice(i * rows, (i + 1) * rows)
        mix = _mm((y_scr[part, :] * sg_scr[part, :]).astype(BF16), wout_ref[...])
        p = jnp.swapaxes(p_ref[:, i * tp:(i + 1) * tp, :], 0, 1).reshape(rows, -1)
        h2 = _ple(xs[i] + mix, p, proj_ref[...], pnw_ref[...], pgw_ref[...])
        if final_norm:
            h2 = _rms(h2, fnw_ref[...])
        o_ref[:, i * tp:(i + 1) * tp, :] = jnp.swapaxes(h2.reshape(tp, nb, d), 0, 1)

    for i in range(parts):
        project_in(i)
    for i in range(parts):
        recur(i)
        if i > 0:
            project_out(i - 1)
    project_out(parts - 1)

    last_steps = ext_scr[parts * rows:parts * rows + halo, :]
    lruh_ref[...] = carry_scr[...]
    lruc_ref[...] = last_steps.reshape(CONV_W - 1, nb, w)
    ext_scr[0:halo, :] = last_steps


def _odd_layer(h, nw, win, cw, cb, gw, grb, gib, lam, wout, p, proj, pnw, pgw, h0, cbuf, fnw,
               *, layer, nb, tl, final_norm):
    b, l, d = h.shape
    w = wout.shape[0]
    seq = lambda width: pl.BlockSpec((nb, tl, width), lambda bi, li: (bi, li, 0))
    state = pl.BlockSpec((nb, w), lambda bi, li: (bi, 0))
    conv = pl.BlockSpec((CONV_W - 1, nb, w), lambda bi, li: (0, bi, 0))
    consts = [nw, win, cw, cb, gw, grb, gib, lam, wout]
    tail = [proj, pnw, pgw]
    rows = nb * tl
    return pl.pallas_call(
        functools.partial(_odd_kernel, tl=tl, final_norm=final_norm),
        grid=(b // nb, l // tl),
        in_specs=[seq(d)] + [_const_spec(c.shape) for c in consts]
                 + [pl.BlockSpec((None, nb, tl, p.shape[-1]), lambda bi, li: (layer, bi, li, 0))]
                 + [_const_spec(c.shape) for c in tail] + [state, conv, _const_spec(fnw.shape)],
        out_specs=[seq(d), state, conv],
        out_shape=[jax.ShapeDtypeStruct((b, l, d), F32),
                   jax.ShapeDtypeStruct((b, w), F32),
                   jax.ShapeDtypeStruct((CONV_W - 1, b, w), F32)],
        scratch_shapes=[pltpu.VMEM(((CONV_W - 1) * nb + rows, w), F32), pltpu.VMEM((nb, w), F32),
                        pltpu.VMEM((rows, w), F32), pltpu.VMEM((rows, w), F32)],
        compiler_params=_params("parallel", "arbitrary"),
        name="odd_layer",
    )(h, *consts, p, *tail, h0, cbuf, fnw)


def _fox_sample_body(c, n_steps, q_ref, kn_ref, vn_ref, cnc_ref, cnr_ref, tri_ref, k_pages, v_pages, f_pages,
                     o_ref, qbd_scr, m_scr, l_scr, acc_scr, rel_scr, *, t_new):
    pages_per_step = len(k_pages)
    cn_col = cnc_ref[...]

    def start():
        rel_scr[...] = jnp.zeros_like(rel_scr)
        q = q_ref[...].astype(F32)
        lane_head = lax.broadcasted_iota(jnp.int32, q.shape, 1) // HEAD_DIM
        zero = jnp.zeros_like(q)
        for h in range(N_HEADS):
            qbd_scr[h * t_new:(h + 1) * t_new, :] = jnp.where(lane_head == h, q, zero)
        m_scr[...] = jnp.full(m_scr.shape, NEG_INF, F32)
        l_scr[...] = jnp.zeros_like(l_scr)
        acc_scr[...] = jnp.zeros_like(acc_scr)

    if c == 0:
        start()
    qbd = qbd_scr[...].astype(BF16)

    def online(s, pv_fn):
        m = m_scr[...]
        m_new = jnp.maximum(m, jnp.max(s, axis=-1, keepdims=True))
        alpha = jnp.exp(m - m_new)
        p = jnp.exp(s - m_new)
        l_scr[...] = alpha * l_scr[...] + jnp.sum(p, axis=-1, keepdims=True)
        acc_scr[...] = alpha * acc_scr[...] + pv_fn(p.astype(BF16))
        m_scr[...] = m_new

    xs = jnp.concatenate([f[...] for f in f_pages], axis=0)
    hi = xs.astype(BF16)
    rest = xs - hi.astype(F32)
    mid = rest.astype(BF16)
    lo = (rest - mid.astype(F32)).astype(BF16)
    tri = tri_ref[...]
    sums = (_mm(hi, tri) + _mm(mid, tri)) + _mm(lo, tri)
    after = rel_scr[...]
    bias = [None] * pages_per_step
    for i in reversed(range(pages_per_step)):
        rows = slice(i * N_HEADS, (i + 1) * N_HEADS)
        rel = sums[rows, :PAGE] + after
        after = after + sums[rows, PAGE:]
        bias[i] = jnp.concatenate(
            [jnp.broadcast_to(rel[h:h + 1, :], (t_new, PAGE)) for h in range(N_HEADS)], axis=0)
    rel_scr[...] = after

    kt = jnp.concatenate([k[...].astype(BF16) for k in k_pages], axis=1)
    vt = jnp.concatenate([v[...].astype(BF16) for v in v_pages], axis=1)
    s = (_mm(qbd, kt) + cn_col) + jnp.concatenate(bias, axis=1)
    yield
    m = m_scr[...]
    m_new = jnp.maximum(m, jnp.max(s, axis=-1, keepdims=True))
    alpha = jnp.exp(m - m_new)
    p = jnp.exp(s - m_new)
    yield
    l_scr[...] = alpha * l_scr[...] + jnp.sum(p, axis=-1, keepdims=True)
    acc_scr[...] = alpha * acc_scr[...] + _mm_nt(p.astype(BF16), vt)
    m_scr[...] = m_new

    def finish():
        zpad = jnp.zeros((PAGE - t_new, HEAD_W), F32)
        kn = jnp.concatenate([kn_ref[...], zpad], axis=0).astype(BF16)
        vn = jnp.concatenate([vn_ref[...], zpad], axis=0).astype(BF16)
        s = (_mm_nt(qbd, kn) + cn_col) - cnr_ref[...]
        t_query = lax.broadcasted_iota(jnp.int32, s.shape, 0) % t_new
        t_key = lax.broadcasted_iota(jnp.int32, s.shape, 1)
        s = jnp.where(t_query >= t_key, s, NEG_INF)
        online(s, lambda p: _mm(p, vn))
        o = acc_scr[...] / l_scr[...]
        lane_head = lax.broadcasted_iota(jnp.int32, (t_new, HEAD_W), 1) // HEAD_DIM
        out = jnp.zeros((t_new, HEAD_W), F32)
        for h in range(N_HEADS):
            out = jnp.where(lane_head == h, o[h * t_new:(h + 1) * t_new, :], out)
        o_ref[...] = out

    if c == n_steps - 1:
        finish()


N_FOX_PROMPT_INPUTS = 5
N_FOX_SAMPLE_INPUTS = 6


N_PAGE_ARRAYS = 3
N_PAGE_SLOTS = 2


def _fox_kernel(tbl_ref, *refs, tq, tk, t_new, pages_per_step, layer):
    prompt_in = refs[:N_FOX_PROMPT_INPUTS]
    sample_in = refs[N_FOX_PROMPT_INPUTS:N_FOX_PROMPT_INPUTS + N_FOX_SAMPLE_INPUTS]
    rest = refs[N_FOX_PROMPT_INPUTS + N_FOX_SAMPLE_INPUTS:]
    caches = rest[:N_PAGE_ARRAYS]
    o_prompt_ref, o_sample_ref, kaug_scr = rest[N_PAGE_ARRAYS:N_PAGE_ARRAYS + 3]
    sample_scr = rest[N_PAGE_ARRAYS + 3:N_PAGE_ARRAYS + 8]
    page_bufs = rest[N_PAGE_ARRAYS + 8:2 * N_PAGE_ARRAYS + 8]
    sems = rest[2 * N_PAGE_ARRAYS + 8]
    n_steps = prompt_in[1].shape[0] // tq
    seq = pl.program_id(0) * N_PAIRS + pl.program_id(1)
    n_seqs = pl.num_programs(0) * N_PAIRS

    def page_copies(of_seq, step):
        slot = step % N_PAGE_SLOTS
        copies = []
        for i in range(pages_per_step):
            page = tbl_ref[of_seq, (n_steps - 1 - step) * pages_per_step + i]
            for a in range(N_PAGE_ARRAYS):
                copies.append(pltpu.make_async_copy(
                    caches[a].at[layer, page], page_bufs[a].at[slot, i], sems.at[slot, a]))
        return copies

    def start(copies):
        for n, cp in enumerate(copies):
            cp.start(priority=n % 2)

    def sample_step(c):
        if c == 0:
            pl.when(seq == 0)(lambda: start(page_copies(seq, 0)))
        if c + 1 < n_steps:
            start(page_copies(seq, c + 1))
        else:
            pl.when(seq + 1 < n_seqs)(lambda: start(page_copies(seq + 1, 0)))
        for cp in page_copies(seq, c):
            cp.wait()
        slot = c % N_PAGE_SLOTS
        views = [[buf.at[slot, i] for i in range(pages_per_step)] for buf in page_bufs]
        return _fox_sample_body(c, n_steps, *sample_in, *views, o_sample_ref, *sample_scr, t_new=t_new)

    _fox_prompt_body(pl.program_id(2), *prompt_in, o_prompt_ref, kaug_scr, tq=tq, tk=tk, with_step=sample_step)


def _fox(qt, kb, vtb, cum_rows, q, k_new, v_new, cn_col, cn_rows, cache_kt, cache_vt, cache_ft, table, layer,
         *, tq, t_new):
    b, l, _ = kb.shape
    bs, n_pages = table.shape
    n_steps = l // tq
    assert bs == b * N_PAIRS and n_pages % n_steps == 0, "one sample sequence per prompt (batch, head pair)"
    pages_per_step = n_pages // n_steps
    n_rows = N_HEADS * t_new
    seq_of = lambda bi, hp: bi * N_PAIRS + hp
    tok = pl.BlockSpec((t_new, HEAD_W), lambda bi, hp, qi, tbl: (seq_of(bi, hp), 0))

    assert n_steps % N_PAGE_SLOTS == 0, "a sequence must start on buffer slot 0"
    caches = [cache_kt, cache_vt, cache_ft]
    page_bufs = [pltpu.VMEM((N_PAGE_SLOTS, pages_per_step) + c.shape[2:], F32) for c in caches]
    pos = jnp.arange(PAGE)
    later = (pos[:, None] > pos[None, :]).astype(BF16)
    tri = jnp.concatenate([later, jnp.ones((PAGE, PAGE), BF16)], axis=1)
    prompt_specs = [pl.BlockSpec((None, PAIR_W, tq), lambda bi, hp, qi, tbl: (bi, hp, qi)),
                    pl.BlockSpec((None, l, PAIR_W), lambda bi, hp, qi, tbl: (bi, 0, hp)),
                    pl.BlockSpec((None, PAIR_W, l), lambda bi, hp, qi, tbl: (bi, hp, 0)),
                    pl.BlockSpec((None, None, 2, tq), lambda bi, hp, qi, tbl: (bi, hp, 0, qi)),
                    pl.BlockSpec((None, None, 2, l), lambda bi, hp, qi, tbl: (bi, hp, 0, 0))]
    sample_specs = [tok, tok, tok,
                    pl.BlockSpec((None, n_rows, 1), lambda bi, hp, qi, tbl: (seq_of(bi, hp), 0, 0)),
                    pl.BlockSpec((None, n_rows, PAGE), lambda bi, hp, qi, tbl: (seq_of(bi, hp), 0, 0)),
                    pl.BlockSpec(tri.shape, lambda bi, hp, qi, tbl: (0, 0))]
    assert len(prompt_specs) == N_FOX_PROMPT_INPUTS and len(sample_specs) == N_FOX_SAMPLE_INPUTS
    return pl.pallas_call(
        functools.partial(_fox_kernel, tq=tq, tk=tq, t_new=t_new, pages_per_step=pages_per_step, layer=layer),
        grid_spec=pltpu.PrefetchScalarGridSpec(
            num_scalar_prefetch=1,
            grid=(b, N_PAIRS, n_steps),
            in_specs=prompt_specs + sample_specs + [pl.BlockSpec(memory_space=pl.ANY)] * N_PAGE_ARRAYS,
            out_specs=[pl.BlockSpec((None, tq, PAIR_W), lambda bi, hp, qi, tbl: (bi, qi, hp)), tok],
            scratch_shapes=[pltpu.VMEM((l, 2 * PAIR_W), BF16),
                            pltpu.VMEM((n_rows, HEAD_W), F32), pltpu.VMEM((n_rows, 1), F32),
                            pltpu.VMEM((n_rows, 1), F32), pltpu.VMEM((n_rows, HEAD_W), F32),
                            pltpu.VMEM((N_HEADS, PAGE), F32)]
                           + page_bufs + [pltpu.SemaphoreType.DMA((N_PAGE_SLOTS, N_PAGE_ARRAYS))]),
        out_shape=[jax.ShapeDtypeStruct((b, l, HEAD_W), F32),
                   jax.ShapeDtypeStruct((bs * t_new, HEAD_W), F32)],
        compiler_params=_params("arbitrary", "arbitrary", "arbitrary"),
        name="fox",
    )(table, qt, kb, vtb, cum_rows, cum_rows, q, k_new, v_new, cn_col, cn_rows, tri, *caches)


def _rope_tables(pos):
    inv = ROPE_BASE ** (-np.arange(HALF_DIM, dtype=np.float64) / HALF_DIM)
    ang = np.asarray(pos, np.float64)[:, None] * inv[None, :]
    cos_head = np.concatenate([np.cos(ang), np.cos(ang)], axis=-1)
    sin_head = np.concatenate([-np.sin(ang), np.sin(ang)], axis=-1)
    return (jnp.asarray(np.tile(cos_head, (1, N_HEADS)), F32), jnp.asarray(np.tile(sin_head, (1, N_HEADS)), F32))


def _log_gamma_table(log_gamma):
    pairs = log_gamma.reshape(N_PAIRS, 2)
    ones = jnp.ones((N_PAIRS, 1, PAIR_W), F32)
    first = pairs[:, 0][:, None, None] * ones
    second = pairs[:, 1][:, None, None] * ones
    mix = jnp.repeat(pairs, HEAD_DIM, axis=1)[:, None, :]
    return jnp.concatenate([first, second, mix, jnp.zeros((N_PAIRS, 5, PAIR_W), F32)], axis=1)


def kernel(x_prompt, x_sample, cache_fox_k, cache_fox_v, cache_fox_logf, page_table, state_ret, state_lru_h, state_lru_conv, p_prompt, p_sample, norm_w, w_in_even, b_forget, ret_gn_w, w_out_even, w_in_odd, conv_w, conv_b, gate_r_w, gate_r_b, gate_i_w, gate_i_b, lru_lambda, w_out_odd, ple_proj, ple_norm_w, ple_gate_w, final_norm_w):
    bp, lp, d = x_prompt.shape
    bs, ls, _ = x_sample.shape
    depth = norm_w.shape[0]
    n_pool = cache_fox_k.shape[1]
    n_pages = page_table.shape[1]
    past_len = n_pages * PAGE
    lru_w = w_out_odd.shape[1]
    rows_p, rows_s = bp * lp, bs * ls
    tm_p = ROW_TILE

    log_gamma = jnp.log1p(-jnp.exp2(-5.0 - jnp.arange(N_HEADS, dtype=F32)))
    lg_tab = _log_gamma_table(log_gamma)
    cos_p, sin_p = _rope_tables(np.arange(lp))
    cos_s, sin_s = _rope_tables(np.tile(past_len + np.arange(ls), bs))

    cache_kt = cache_fox_k.transpose(0, 1, 3, 4, 2).reshape(-1, n_pool, HEAD_W, PAGE)
    cache_vt = cache_fox_v.transpose(0, 1, 3, 4, 2).reshape(-1, n_pool, HEAD_W, PAGE)
    cache_ft = jnp.swapaxes(cache_fox_logf, 2, 3)

    row2 = lambda v: v.reshape(1, -1)
    hp = x_prompt.reshape(rows_p, d)
    hs = x_sample.reshape(rows_s, d)
    outs = {k: [] for k in ("fl_p", "rs_p", "lh_p", "lc_p", "fk_s", "fv_s", "fl_s", "rs_s", "lh_s", "lc_s")}
    y_p = y_s = None
    n_even = w_in_even.shape[0]
    stacked_kv = None

    for i in range(depth):
        j = i // 2
        proj = ple_proj[i].astype(BF16)
        pgw = ple_gate_w[i].astype(BF16)
        pnw = row2(ple_norm_w[i])
        nw = row2(norm_w[i])
        if i % 2 == 0:
            w = w_in_even[j]
            c0 = 3 * HEAD_W
            wf = w[:, :c0].astype(BF16)
            wl = jnp.pad(w[:, c0:c0 + N_HEADS], ((0, 0), (0, PAGE - N_HEADS))).astype(BF16)
            wr = w[:, c0 + N_HEADS:2 * c0 + N_HEADS].astype(BF16)
            wg = w[:, 2 * c0 + N_HEADS:].astype(BF16)
            bf = jnp.pad(b_forget[j], (0, PAGE - N_HEADS)).reshape(1, PAGE)
            wo = w_out_even[j].astype(BF16)
            gn = row2(ret_gn_w[j])
            wt = jnp.swapaxes(w, 0, 1)
            wqkvt = wt[:c0].astype(BF16)
            wlt = jnp.pad(wt[c0:c0 + N_HEADS], ((0, N_HEADS), (0, 0))).astype(BF16)
            bfc = jnp.pad(b_forget[j], (0, N_HEADS)).reshape(2 * N_HEADS, 1)

            fqt, fkt_all, fvt_all, fkb, fvtb, lft, rq, rk, rv, g = _even_in_prompt(
                hp.reshape(bp, lp, d), nw, wqkvt, wlt, wr, wg, bfc, cos_p, sin_p, stacked_kv,
                layer=j, n_layers=n_even, tm=tm_p)
            stacked_kv = (fkt_all, fvt_all)
            cum_rows = _seg_cumsum(lft.reshape(bp * N_HEADS, lp), seg=lp).reshape(bp, N_PAIRS, 2, lp)
            fq, fk, fv, lf, lft_s, rq_s, rk_s, rv_s, g_s = _even_in_sample(
                hs, nw, wf, wl, wr, wg, bf, cos_s, sin_s)
            cn = _seg_cumsum(lft_s, seg=ls).reshape(N_HEADS, bs, ls).transpose(1, 0, 2)
            cn_col = cn.reshape(bs, N_HEADS * ls, 1)
            cn_rows = jnp.pad(jnp.repeat(cn, ls, axis=1), ((0, 0), (0, 0), (0, PAGE - ls)))
            fo, fo_s = _fox(fqt, fkb, fvtb, cum_rows, fq, fk, fv, cn_col, cn_rows,
                            cache_kt, cache_vt, cache_ft, page_table, j, tq=FOX_BLOCK, t_new=ls)

            ro, s_p = _ret_prompt(rq, rk, rv, lg_tab, gn)
            hp = _even_out(hp, fo.reshape(rows_p, HEAD_W), ro.reshape(rows_p, HEAD_W),
                           g.reshape(rows_p, 2 * HEAD_W), wo, p_prompt.reshape(depth, rows_p, -1),
                           proj, pnw, pgw, layer=i, tm=tm_p)
            outs["fl_p"].append(jnp.swapaxes(lft, 1, 2))
            outs["rs_p"].append(s_p)

            ro, s_s = _ret_sample(rq_s, rk_s, rv_s, state_ret, lg_tab, gn, layer=j, length=ls)
            hs = _even_out(hs, fo_s, ro, g_s, wo, p_sample.reshape(depth, rows_s, -1),
                           proj, pnw, pgw, layer=i, tm=rows_s)
            outs["fk_s"].append(fk.reshape(bs, ls, N_HEADS, HEAD_DIM))
            outs["fv_s"].append(fv.reshape(bs, ls, N_HEADS, HEAD_DIM))
            outs["fl_s"].append(lf.reshape(bs, ls, N_HEADS))
            outs["rs_s"].append(s_s)
        else:
            final = i == depth - 1
            gw = jnp.concatenate([gate_r_w[j], gate_i_w[j]], axis=-1).astype(BF16)
            args = (nw, w_in_odd[j].astype(BF16), conv_w[j], row2(conv_b[j]), gw, row2(gate_r_b[j]),
                    row2(gate_i_b[j]), row2(lru_lambda[j]), w_out_odd[j].astype(BF16))
            tail = (proj, pnw, pgw)
            fnw = row2(final_norm_w)
            o, lh, lc = _odd_layer(hp.reshape(bp, lp, d), *args, p_prompt, *tail,
                                   jnp.zeros((bp, lru_w), F32), jnp.zeros((CONV_W - 1, bp, lru_w), F32),
                                   fnw, layer=i, nb=SEQ_PER_BLOCK, tl=STEPS_PER_BLOCK, final_norm=final)
            hp = o.reshape(rows_p, d)
            outs["lh_p"].append(lh)
            outs["lc_p"].append(jnp.swapaxes(lc, 0, 1))
            o, lh, lc = _odd_layer(hs.reshape(bs, ls, d), *args, p_sample, *tail,
                                   state_lru_h[j], jnp.swapaxes(state_lru_conv[j], 0, 1),
                                   fnw, layer=i, nb=SEQ_PER_BLOCK, tl=ls, final_norm=final)
            hs = o.reshape(rows_s, d)
            outs["lh_s"].append(lh)
            outs["lc_s"].append(jnp.swapaxes(lc, 0, 1))
            if final:
                y_p, y_s = hp.reshape(bp, lp, d), hs.reshape(bs, ls, d)

    st = lambda k: jnp.stack(outs[k])
    to_blhd = lambda t: t.reshape(n_even, bp, N_HEADS, HEAD_DIM, lp).transpose(0, 1, 4, 2, 3)
    return (y_p, y_s, to_blhd(stacked_kv[0]), to_blhd(stacked_kv[1]), st("fl_p"), st("rs_p"), st("lh_p"), st("lc_p"),
            st("fk_s"), st("fv_s"), st("fl_s"), st("rs_s"), st("lh_s"), st("lc_s"))
```
